```python
import math
import jax, jax.numpy as jnp
from jax import lax
import numpy as np

D_MODEL = 1024
BATCH = 4
SEQ = 8192
DEPTH = 2

N_EVEN = (DEPTH + 1) // 2
N_ODD = DEPTH // 2
D_FF = 2816
FFN_RES = 0.5
NORM_EPS = 1e-6
CHUNK = 64
CONV_W = 4

GLA_HEADS = 4
GLA_DK = 64
GLA_DV = 128
GLA_LORA = 16
GLA_GATE_NORM = 16.0
GLA_QK = GLA_HEADS * GLA_DK
GLA_V = GLA_HEADS * GLA_DV
GDN_HEADS = 4
GDN_DK = 128
GDN_DV = 128
GDN_QK = GDN_HEADS * GDN_DK
GDN_V = GDN_HEADS * GDN_DV
GDN_CONV_CH = 2 * GDN_QK + GDN_V
RWKV_HEADS = 8
RWKV_N = 64
RWKV_W = RWKV_HEADS * RWKV_N
RWKV_W_LORA = 64
RWKV_A_LORA = 64
RWKV_G_LORA = 128
RWKV_GN_EPS = 64e-5
RWKV_SHIFT = 3 * RWKV_W + RWKV_W_LORA + RWKV_A_LORA + RWKV_G_LORA
LRU_WIDTH = 512
LRU_BLOCKS = 8
LRU_BW = LRU_WIDTH // LRU_BLOCKS
LRU_C = 8.0

EVEN_SPLITS = (GLA_QK, GLA_QK, GLA_V, GLA_V, GLA_LORA, GDN_CONV_CH, GDN_V, GDN_HEADS, GDN_HEADS)
EVEN_IN = 2 * GLA_QK + 2 * GLA_V + GLA_LORA + GDN_CONV_CH + GDN_V + 2 * GDN_HEADS
EVEN_OUT = GLA_V + GDN_V
RWKV_SPLITS = (RWKV_W, RWKV_W, RWKV_W, RWKV_W_LORA, RWKV_A_LORA, RWKV_G_LORA)
ODD_IN = RWKV_SHIFT + 2 * LRU_WIDTH
ODD_OUT = RWKV_W + LRU_WIDTH

kernel_name = 'hybrid_gla_gdn_rwkv7_rglru_macaron'


def _split(x, sizes):
    out, o = [], 0
    for s in sizes:
        out.append(x[..., o:o + s])
        o += s
    return out


def _rmsnorm(x, w):
    xf = x.astype(jnp.float32)
    y = xf * lax.rsqrt(jnp.mean(xf * xf, -1, keepdims=True) + NORM_EPS) * w
    return y.astype(x.dtype)


def _heads(x, n_heads):
    return x.reshape(x.shape[:-1] + (n_heads, x.shape[-1] // n_heads))


def _l2norm(x):
    xf = x.astype(jnp.float32)
    return xf * lax.rsqrt(jnp.sum(xf * xf, -1, keepdims=True) + NORM_EPS)


def _head_rmsnorm(x, n_heads, w):
    xh = _heads(x.astype(jnp.float32), n_heads)
    xh = xh * lax.rsqrt(jnp.mean(xh * xh, -1, keepdims=True) + NORM_EPS) * w
    return xh.reshape(x.shape).astype(x.dtype)


def _swiglu(x, w_gate, w_up, w_down):
    return (jax.nn.silu(x @ w_gate) * (x @ w_up)) @ w_down


def _causal_dwconv(x, w):
    K, T = w.shape[0], x.shape[1]
    xp = jnp.pad(x, ((0, 0), (K - 1, 0), (0, 0)))
    y = xp[:, 0:T] * w[0]
    for j in range(1, K):
        y = y + xp[:, j:j + T] * w[j]
    return y


def _token_shift(x):
    return jnp.pad(x, ((0, 0), (1, 0), (0, 0)))[:, :-1]


def _to_chunks(x):
    B, T, H, d = x.shape
    return x.reshape(B, T // CHUNK, CHUNK, H, d).transpose(0, 3, 1, 2, 4)


def _from_chunks(o):
    B, H, N, C, d = o.shape
    return o.transpose(0, 2, 3, 1, 4).reshape(B, N * C, H * d)


def _gla_chunked(q, k, v, log_a):
    C = q.shape[-2]
    causal = jnp.tril(jnp.ones((C, C), bool))
    b = jnp.cumsum(log_a, axis=-2)
    q_e = q * jnp.exp(b)
    att = jnp.where(causal, jnp.einsum('bhncd,bhnsd->bhncs', q_e, k * jnp.exp(-b)), 0.0)
    o_intra = jnp.einsum('bhncs,bhnsv->bhncv', att, v)
    b_last = b[..., -1:, :]
    k_dec = k * jnp.exp(b_last - b)
    a_last = jnp.exp(b_last[..., 0, :])

    def step(S, xs):
        qe, kd, vv, al = xs
        o = jnp.einsum('bhcd,bhdv->bhcv', qe, S)
        S = S * al[..., None] + jnp.einsum('bhcd,bhcv->bhdv', kd, vv)
        return S, o

    B, H, _, _, dk = q.shape
    S0 = jnp.zeros((B, H, dk, v.shape[-1]), q.dtype)
    _, o_inter = lax.scan(step, S0, tuple(jnp.moveaxis(t, 2, 0) for t in (q_e, k_dec, v, a_last)))
    return o_intra + jnp.moveaxis(o_inter, 0, 2)


def _gated_delta_chunked(q, k, v, g, beta):
    C, dv = q.shape[-2], v.shape[-1]
    tril = jnp.tril(jnp.ones((C, C), bool))
    strict = jnp.tril(jnp.ones((C, C), bool), -1)
    gc = jnp.cumsum(g, axis=-1)
    diff = gc[..., :, None] - gc[..., None, :]
    decay = jnp.where(tril, jnp.exp(jnp.where(tril, diff, 0.0)), 0.0)
    kb = k * beta[..., None]
    L = jnp.where(strict, jnp.einsum('bhncd,bhnsd->bhncs', kb, k) * decay, 0.0)
    rhs = jnp.concatenate([v * beta[..., None], kb * jnp.exp(gc)[..., None]], -1)
    sol = lax.linalg.triangular_solve(L + jnp.eye(C, dtype=L.dtype), rhs,
                                      left_side=True, lower=True, unit_diagonal=True)
    u, w = sol[..., :dv], sol[..., dv:]
    att = jnp.where(tril, jnp.einsum('bhncd,bhnsd->bhncs', q, k) * decay, 0.0)
    q_e = q * jnp.exp(gc)[..., None]
    k_dec = k * jnp.exp(gc[..., -1:] - gc)[..., None]
    g_last = jnp.exp(gc[..., -1])

    def step(S, xs):
        qe, kd, uu, ww, aa, gl = xs
        v_new = uu - jnp.einsum('bhcd,bhdv->bhcv', ww, S)
        o = jnp.einsum('bhcd,bhdv->bhcv', qe, S) + jnp.einsum('bhcs,bhsv->bhcv', aa, v_new)
        S = S * gl[..., None, None] + jnp.einsum('bhcd,bhcv->bhdv', kd, v_new)
        return S, o

    B, H, _, _, dk = q.shape
    S0 = jnp.zeros((B, H, dk, dv), q.dtype)
    _, o = lax.scan(step, S0, tuple(jnp.moveaxis(t, 2, 0) for t in (q_e, k_dec, u, w, att, g_last)))
    return jnp.moveaxis(o, 0, 2)


def _rwkv7_scan(r, w, k, v, kk, a):
    B, T, H, N = r.shape

    def step(S, xs):
        rt, wt, kt, vt, kkt, at = xs
        sa = jnp.einsum('bhij,bhj->bhi', S, -kkt)
        S = S * wt[:, :, None, :] + sa[..., None] * (kkt * at)[:, :, None, :] + vt[..., None] * kt[:, :, None, :]
        return S, jnp.einsum('bhij,bhj->bhi', S, rt)

    S0 = jnp.zeros((B, H, N, N), r.dtype)
    _, y = lax.scan(step, S0, tuple(jnp.moveaxis(t, 1, 0) for t in (r, w, k, v, kk, a)))
    return jnp.moveaxis(y, 0, 1)


def _linear_scan(a, b):
    def combine(c1, c2):
        a1, b1 = c1
        a2, b2 = c2
        return a1 * a2, a2 * b1 + b2
    _, h = lax.associative_scan(combine, (a, b), axis=1)
    return h


def _even_mixer(h, w_in, w_out, gla_lora_w2, gla_lora_b, gla_norm, gdn_conv, gdn_a_log, gdn_dt_bias, gdn_norm):
    f32 = jnp.float32
    p = h @ w_in
    gq, gk, gv, gg, glr, dqkv, dz, da, db = _split(p, EVEN_SPLITS)
    log_a = jax.nn.log_sigmoid((glr @ gla_lora_w2 + gla_lora_b).astype(f32)) / GLA_GATE_NORM
    o_gla = _gla_chunked(_to_chunks(_heads(gq.astype(f32) * GLA_DK ** -0.5, GLA_HEADS)),
                         _to_chunks(_heads(gk.astype(f32), GLA_HEADS)),
                         _to_chunks(_heads(gv.astype(f32), GLA_HEADS)),
                         _to_chunks(_heads(log_a, GLA_HEADS)))
    y_gla = _head_rmsnorm(_from_chunks(o_gla).astype(h.dtype), GLA_HEADS, gla_norm) * jax.nn.silu(gg)
    c = jax.nn.silu(_causal_dwconv(dqkv, gdn_conv))
    cq, ck, cv = _split(c, (GDN_QK, GDN_QK, GDN_V))
    q = _l2norm(_heads(cq, GDN_HEADS)) * GDN_DK ** -0.5
    k = _l2norm(_heads(ck, GDN_HEADS))
    v = _heads(cv.astype(f32), GDN_HEADS)
    beta = jax.nn.sigmoid(db.astype(f32))
    g = -jnp.exp(gdn_a_log.astype(f32)) * jax.nn.softplus((da + gdn_dt_bias).astype(f32))
    o_gdn = _gated_delta_chunked(_to_chunks(q), _to_chunks(k), _to_chunks(v),
                                 _to_chunks(g[..., None])[..., 0], _to_chunks(beta[..., None])[..., 0])
    y_gdn = _head_rmsnorm(_from_chunks(o_gdn).astype(h.dtype), GDN_HEADS, gdn_norm) * jax.nn.silu(dz)
    return jnp.concatenate([y_gla, y_gdn], -1) @ w_out


def _odd_mixer(h, w_in, w_out, rwkv_mu, rwkv_w0, rwkv_w2, rwkv_a0, rwkv_a2, rwkv_g2, rwkv_k_k, rwkv_k_a,
               rwkv_r_k, rwkv_ln_w, rwkv_ln_b, lru_conv_w, lru_conv_b, lru_wa, lru_ba, lru_wx, lru_bx, lru_lambda):
    f32 = jnp.float32
    B, T, _ = h.shape
    p = h @ w_in
    ps, lx, ly = _split(p, (RWKV_SHIFT, LRU_WIDTH, LRU_WIDTH))
    ps = ps + (_token_shift(ps) - ps) * rwkv_mu
    r, k, v, wl, al, gl = _split(ps, RWKV_SPLITS)
    w = -jax.nn.softplus(-(rwkv_w0 + jnp.tanh(wl) @ rwkv_w2).astype(f32)) - 0.5
    decay = jnp.exp(-jnp.exp(w))
    a = jax.nn.sigmoid(rwkv_a0 + al @ rwkv_a2)
    g = jax.nn.sigmoid(gl) @ rwkv_g2
    kk = _l2norm(_heads(k * rwkv_k_k, RWKV_HEADS))
    k = k * (1.0 + (a - 1.0) * rwkv_k_a)
    rh, kh, vh = _heads(r, RWKV_HEADS), _heads(k, RWKV_HEADS), _heads(v, RWKV_HEADS)
    y = _rwkv7_scan(rh.astype(f32), _heads(decay, RWKV_HEADS), kh.astype(f32), vh.astype(f32),
                    kk, _heads(a, RWKV_HEADS).astype(f32))
    mu = jnp.mean(y, -1, keepdims=True)
    var = jnp.mean(jnp.square(y - mu), -1, keepdims=True)
    y = ((y - mu) * lax.rsqrt(var + RWKV_GN_EPS)).reshape(B, T, RWKV_W) * rwkv_ln_w + rwkv_ln_b
    bonus = (jnp.sum(rh * kh * rwkv_r_k, -1, keepdims=True) * vh).reshape(B, T, RWKV_W)
    y_rwkv = ((y.astype(h.dtype) + bonus) * g)
    xb = _causal_dwconv(lx, lru_conv_w) + lru_conv_b
    xblk = _heads(xb, LRU_BLOCKS)
    gate_r = jax.nn.sigmoid(jnp.einsum('btki,kij->btkj', xblk, lru_wa).reshape(B, T, LRU_WIDTH) + lru_ba).astype(f32)
    gate_i = jax.nn.sigmoid(jnp.einsum('btki,kij->btkj', xblk, lru_wx).reshape(B, T, LRU_WIDTH) + lru_bx).astype(f32)
    log_a = -LRU_C * gate_r * jax.nn.softplus(-lru_lambda.astype(f32))
    mult = jnp.sqrt(jnp.maximum(-jnp.expm1(2.0 * log_a), 0.0))
    hl = _linear_scan(jnp.exp(log_a), mult * gate_i * xb.astype(f32))
    y_lru = hl.astype(h.dtype) * jax.nn.gelu(ly)
    return jnp.concatenate([y_rwkv, y_lru], -1) @ w_out


def setup_inputs(seed: int = 0) -> dict:
    key = jax.random.key(seed)
    ks = jax.random.split(key, 34)
    f32 = jnp.float32

    def nrm(i, shape, scale):
        return jax.random.normal(ks[i], shape, f32) * scale

    def uni(i, shape, lo, hi):
        return jax.random.uniform(ks[i], shape, f32, lo, hi)

    E, O = N_EVEN, N_ODD
    dt = jnp.exp(uni(12, (E, GDN_HEADS), math.log(1e-3), math.log(1e-1)))
    s = uni(33, (O, LRU_WIDTH), 0.9, 0.999) ** (1.0 / LRU_C)
    return {
        'x': nrm(0, (BATCH, SEQ, D_MODEL), 1.0),
        'norm_w': 1.0 + nrm(1, (DEPTH, 6, D_MODEL), 0.05),
        'ffn_w_gate': nrm(2, (DEPTH, 2, D_MODEL, D_FF), D_MODEL ** -0.5),
        'ffn_w_up': nrm(3, (DEPTH, 2, D_MODEL, D_FF), D_MODEL ** -0.5),
        'ffn_w_down': nrm(4, (DEPTH, 2, D_FF, D_MODEL), D_FF ** -0.5),
        'even_w_in': nrm(5, (E, D_MODEL, EVEN_IN), D_MODEL ** -0.5),
        'even_w_out': nrm(6, (E, EVEN_OUT, D_MODEL), EVEN_OUT ** -0.5),
        'gla_lora_w2': nrm(7, (E, GLA_LORA, GLA_QK), GLA_LORA ** -0.5),
        'gla_lora_b': nrm(8, (E, GLA_QK), 0.1),
        'gla_norm': 1.0 + nrm(9, (E, GLA_DV), 0.05),
        'gdn_conv': nrm(10, (E, CONV_W, GDN_CONV_CH), CONV_W ** -0.5),
        'gdn_a_log': jnp.log(uni(11, (E, GDN_HEADS), 1.0, 16.0)),
        'gdn_dt_bias': dt + jnp.log(-jnp.expm1(-dt)),
        'gdn_norm': 1.0 + nrm(13, (E, GDN_DV), 0.05),
        'odd_w_in': nrm(14, (O, D_MODEL, ODD_IN), D_MODEL ** -0.5),
        'odd_w_out': nrm(15, (O, ODD_OUT, D_MODEL), ODD_OUT ** -0.5),
        'rwkv_mu': uni(16, (O, RWKV_SHIFT), 0.0, 1.0),
        'rwkv_w0': -6.5 + 5.0 * jnp.linspace(0.0, 1.0, RWKV_W, dtype=f32) ** 0.85 + nrm(17, (O, RWKV_W), 0.05),
        'rwkv_w2': nrm(18, (O, RWKV_W_LORA, RWKV_W), RWKV_W_LORA ** -0.5),
        'rwkv_a0': nrm(19, (O, RWKV_W), 0.1),
        'rwkv_a2': nrm(20, (O, RWKV_A_LORA, RWKV_W), RWKV_A_LORA ** -0.5),
        'rwkv_g2': nrm(21, (O, RWKV_G_LORA, RWKV_W), RWKV_G_LORA ** -0.5),
        'rwkv_k_k': 0.85 + nrm(22, (O, RWKV_W), 0.05),
        'rwkv_k_a': 1.0 + nrm(23, (O, RWKV_W), 0.05),
        'rwkv_r_k': nrm(24, (O, RWKV_HEADS, RWKV_N), 0.1),
        'rwkv_ln_w': 1.0 + nrm(25, (O, RWKV_W), 0.05),
        'rwkv_ln_b': nrm(26, (O, RWKV_W), 0.02),
        'lru_conv_w': nrm(27, (O, CONV_W, LRU_WIDTH), CONV_W ** -0.5),
        'lru_conv_b': nrm(28, (O, LRU_WIDTH), 0.02),
        'lru_wa': nrm(29, (O, LRU_BLOCKS, LRU_BW, LRU_BW), LRU_BW ** -0.5),
        'lru_ba': nrm(30, (O, LRU_WIDTH), 0.02),
        'lru_wx': nrm(31, (O, LRU_BLOCKS, LRU_BW, LRU_BW), LRU_BW ** -0.5),
        'lru_bx': nrm(32, (O, LRU_WIDTH), 0.02),
        'lru_lambda': jnp.log(s) - jnp.log1p(-s),
    }


def reference(x, norm_w, ffn_w_gate, ffn_w_up, ffn_w_down, even_w_in, even_w_out, gla_lora_w2, gla_lora_b,
              gla_norm, gdn_conv, gdn_a_log, gdn_dt_bias, gdn_norm, odd_w_in, odd_w_out, rwkv_mu, rwkv_w0,
              rwkv_w2, rwkv_a0, rwkv_a2, rwkv_g2, rwkv_k_k, rwkv_k_a, rwkv_r_k, rwkv_ln_w, rwkv_ln_b,
              lru_conv_w, lru_conv_b, lru_wa, lru_ba, lru_wx, lru_bx, lru_lambda):
    for i in range(DEPTH):
        j = i // 2
        hh = _rmsnorm(x, norm_w[i, 0])
        x = x + FFN_RES * _rmsnorm(_swiglu(hh, ffn_w_gate[i, 0], ffn_w_up[i, 0], ffn_w_down[i, 0]), norm_w[i, 1])
        hh = _rmsnorm(x, norm_w[i, 2])
        if i % 2 == 0:
            m = _even_mixer(hh, even_w_in[j], even_w_out[j], gla_lora_w2[j], gla_lora_b[j], gla_norm[j],
                            gdn_conv[j], gdn_a_log[j], gdn_dt_bias[j], gdn_norm[j])
        else:
            m = _odd_mixer(hh, odd_w_in[j], odd_w_out[j], rwkv_mu[j], rwkv_w0[j], rwkv_w2[j], rwkv_a0[j],
                           rwkv_a2[j], rwkv_g2[j], rwkv_k_k[j], rwkv_k_a[j], rwkv_r_k[j], rwkv_ln_w[j],
                           rwkv_ln_b[j], lru_conv_w[j], lru_conv_b[j], lru_wa[j], lru_ba[j], lru_wx[j],
                           lru_bx[j], lru_lambda[j])
        x = x + _rmsnorm(m, norm_w[i, 3])
        hh = _rmsnorm(x, norm_w[i, 4])
        x = x + FFN_RES * _rmsnorm(_swiglu(hh, ffn_w_gate[i, 1], ffn_w_up[i, 1], ffn_w_down[i, 1]), norm_w[i, 5])
    return x
```

```python
import functools

import jax
import jax.numpy as jnp
from jax import lax
from jax.experimental import pallas as pl
from jax.experimental.pallas import tpu as pltpu

F32 = jnp.float32
BF16 = jnp.bfloat16

D_MODEL = 1024
D_FF = 2816
FFN_RES = 0.5
NORM_EPS = 1e-6
CONV_W = 4

GLA_HEADS = 4
GLA_DK = 64
GLA_DV = 128
GLA_LORA = 16
GLA_GATE_NORM = 16.0
GLA_QK = GLA_HEADS * GLA_DK
GLA_V = GLA_HEADS * GLA_DV
GLA_CHUNK = 64

GDN_HEADS = 4
GDN_DK = 128
GDN_DV = 128
GDN_QK = GDN_HEADS * GDN_DK
GDN_V = GDN_HEADS * GDN_DV
GDN_CONV_CH = 2 * GDN_QK + GDN_V
GDN_CHUNK = 128

RWKV_HEADS = 8
RWKV_N = 64
RWKV_W = RWKV_HEADS * RWKV_N
RWKV_W_LORA = 64
RWKV_A_LORA = 64
RWKV_G_LORA = 128
RWKV_GN_EPS = 64e-5
RWKV_SHIFT = 3 * RWKV_W + RWKV_W_LORA + RWKV_A_LORA + RWKV_G_LORA
RWKV_CHUNK = 64

LRU_WIDTH = 512
LRU_BLOCKS = 8
LRU_BW = LRU_WIDTH // LRU_BLOCKS
LRU_C = 8.0

EVEN_OUT = GLA_V + GDN_V
ODD_OUT = RWKV_W + LRU_WIDTH

LANES = 128
SUBLANES = 8
SMALL_GLR = 0
SMALL_DA = GLA_LORA
SMALL_DB = GLA_LORA + GDN_HEADS

TILE_FFN = 512
TILE_PREP = 512
TILE_CHUNK = 256
FF_CHUNK = 1408
VMEM_LIMIT = 56 * 1024 * 1024


def _mm(a, b):
    return jnp.dot(a.astype(BF16), b.astype(BF16), preferred_element_type=F32)


def _mm_nt(a, b):
    return lax.dot_general(a.astype(BF16), b.astype(BF16), (((1,), (1,)), ((), ())),
                           preferred_element_type=F32)


def _rms(x, w):
    return x * lax.rsqrt(jnp.mean(x * x, axis=-1, keepdims=True) + NORM_EPS) * w


def _sigmoid(x):
    return 1.0 / (1.0 + jnp.exp(-x))


def _silu(x):
    return x * _sigmoid(x)


def _softplus(x):
    return jnp.maximum(x, 0.0) + jnp.log1p(jnp.exp(-jnp.abs(x)))


def _gelu_tanh(x):
    c = 0.7978845608028654
    return 0.5 * x * (1.0 + jnp.tanh(c * (x + 0.044715 * (x * x * x))))


def _seg_cumsum(x, seg):
    rows = lax.broadcasted_iota(jnp.int32, x.shape, 0) & (seg - 1)
    d = 1
    while d < seg:
        x = x + jnp.where(rows >= d, pltpu.roll(x, d, 0), 0.0)
        d *= 2
    return x


def _shift_rows(x, carry, s):
    sh = pltpu.roll(x, s, 0)
    c = pltpu.roll(carry, s, 0)
    rows = lax.broadcasted_iota(jnp.int32, c.shape, 0)
    head = jnp.where(rows < s, c, sh[0:SUBLANES])
    return jnp.concatenate([head, sh[SUBLANES:]], axis=0)


def _group_sum(x, ones_bd):
    hi = x.astype(BF16)
    r1 = x - hi.astype(F32)
    mid = r1.astype(BF16)
    lo = (r1 - mid.astype(F32)).astype(BF16)
    dot = functools.partial(jnp.dot, preferred_element_type=F32)
    return dot(hi, ones_bd) + dot(mid, ones_bd) + dot(lo, ones_bd)


def _split(a):
    hi = a.astype(BF16)
    return hi, (a - hi.astype(F32)).astype(BF16)


def _mm3(a, b, nt=False):
    dims = (((1,), (1 if nt else 0,)), ((), ()))
    dot = functools.partial(lax.dot_general, dimension_numbers=dims, preferred_element_type=F32)
    ah, al = _split(a)
    bh, bl = _split(b)
    return dot(ah, bh) + dot(al, bh) + dot(ah, bl)


def _inv_unit_lower(low, n, mm=_mm):
    r = lax.broadcasted_iota(jnp.int32, low.shape, 0)
    c = lax.broadcasted_iota(jnp.int32, low.shape, 1)
    neg = -low
    t = jnp.where(r == c, 1.0, 0.0) + neg
    p = neg
    k = 2
    while k < n:
        p = mm(p, p)
        t = t + mm(p, t)
        k *= 2
    return t


def _const_spec(shape):
    nd = len(shape)
    return pl.BlockSpec(shape, lambda *_: (0,) * nd, pipeline_mode=pl.Buffered(1))


def _row(v):
    return v.reshape(1, -1).astype(F32)


def _ffn_kernel(x_ref, nw_ref, wg_ref, wu_ref, wd_ref, o_ref):
    x = x_ref[...]
    h = _rms(x, nw_ref[0:1, :]).astype(BF16)
    acc = jnp.zeros(x.shape, F32)
    for c in range(D_FF // FF_CHUNK):
        sl = slice(c * FF_CHUNK, (c + 1) * FF_CHUNK)
        g = jnp.dot(h, wg_ref[:, sl], preferred_element_type=F32)
        u = jnp.dot(h, wu_ref[:, sl], preferred_element_type=F32)
        a = (_silu(g) * u).astype(BF16)
        acc = acc + jnp.dot(a, wd_ref[sl, :], preferred_element_type=F32)
    o_ref[...] = x + FFN_RES * _rms(acc, nw_ref[1:2, :])


def _ffn(x2, nw_pre, nw_post, wg, wu, wd):
    m, d = x2.shape
    tm = min(TILE_FFN, m)
    nw = jnp.stack([nw_pre, nw_post]).astype(F32)
    return pl.pallas_call(
        _ffn_kernel,
        grid=(m // tm,),
        in_specs=[
            pl.BlockSpec((tm, d), lambda i: (i, 0)),
            _const_spec((2, d)),
            _const_spec((d, D_FF)),
            _const_spec((d, D_FF)),
            _const_spec((D_FF, d)),
        ],
        out_specs=pl.BlockSpec((tm, d), lambda i: (i, 0)),
        out_shape=jax.ShapeDtypeStruct((m, d), F32),
        compiler_params=pltpu.CompilerParams(dimension_semantics=("arbitrary",),
                                             vmem_limit_bytes=VMEM_LIMIT),
        name="ffn",
    )(x2, nw, wg.astype(BF16), wu.astype(BF16), wd.astype(BF16))


def _even_prep_kernel(tiles_per_seq,
                      x_ref, nw_ref, wa_ref, ws_ref, wc_ref, wz_ref, lw2_ref, lb_ref, conv_ref, hp_ref,
                      qe_ref, kt_ref, kd_ref, al_ref, gv_ref, sgg_ref, dq_ref, dk_ref, dv_ref, gb_ref,
                      sdz_ref, carry_ref):
    first = (pl.program_id(0) % tiles_per_seq) == 0
    tm = x_ref.shape[0]
    h = _rms(x_ref[...], nw_ref[...]).astype(BF16)

    pa = jnp.dot(h, wa_ref[...], preferred_element_type=F32)
    ps = jnp.dot(h, ws_ref[...], preferred_element_type=F32)
    z = _mm(ps, lw2_ref[...]) + lb_ref[...]
    log_a = -_softplus(-z) * (1.0 / GLA_GATE_NORM)
    b = _seg_cumsum(log_a, GLA_CHUNK)
    gk = pa[:, GLA_QK:2 * GLA_QK]
    qe_ref[...] = pa[:, 0:GLA_QK] * (GLA_DK ** -0.5) * jnp.exp(b)
    kt_ref[...] = gk * jnp.exp(-b)
    for c in range(tm // GLA_CHUNK):
        cs = slice(c * GLA_CHUNK, (c + 1) * GLA_CHUNK)
        b_last = b[(c + 1) * GLA_CHUNK - 1:(c + 1) * GLA_CHUNK, :]
        al_ref[c] = jnp.exp(b_last)
        kd_ref[cs, :] = gk[cs] * jnp.exp(b_last - b[cs])
    gv_ref[...] = pa[:, 2 * GLA_QK:2 * GLA_QK + GLA_V]
    sgg_ref[...] = _silu(pa[:, 2 * GLA_QK + GLA_V:])

    pc = jnp.dot(h, wc_ref[...], preferred_element_type=F32)
    carry = jnp.where(first, 0.0, carry_ref[...])
    conv = pc * conv_ref[CONV_W - 1:CONV_W, :]
    for s in range(1, CONV_W):
        conv = conv + _shift_rows(pc, carry, s) * conv_ref[CONV_W - 1 - s:CONV_W - s, :]
    carry_ref[...] = pc[tm - SUBLANES:tm, :]
    c = _silu(conv)
    for hd in range(GDN_HEADS):
        lo, hi = hd * GDN_DK, (hd + 1) * GDN_DK
        cq = c[:, lo:hi]
        ck = c[:, GDN_QK + lo:GDN_QK + hi]
        dq_ref[:, lo:hi] = cq * lax.rsqrt(jnp.sum(cq * cq, -1, keepdims=True) + NORM_EPS) * (GDN_DK ** -0.5)
        dk_ref[:, lo:hi] = ck * lax.rsqrt(jnp.sum(ck * ck, -1, keepdims=True) + NORM_EPS)
    dv_ref[...] = c[:, 2 * GDN_QK:]

    g = -jnp.exp(hp_ref[0:1, :]) * _softplus(ps + hp_ref[1:2, :])
    gc = _seg_cumsum(g, GDN_CHUNK)
    lane = lax.broadcasted_iota(jnp.int32, ps.shape, 1)
    is_g = (lane >= SMALL_DA) & (lane < SMALL_DB)
    gb_ref[...] = jnp.where(is_g, gc, _sigmoid(ps))

    sdz_ref[...] = _silu(jnp.dot(h, wz_ref[...], preferred_element_type=F32))


def _even_prep(x2, seq, nw, w_in, lora_w2, lora_b, conv_w, a_log, dt_bias):
    m, d = x2.shape
    tm = min(TILE_PREP, seq)
    o = 0
    w_a = w_in[:, 0:2 * GLA_QK + 2 * GLA_V]
    o = 2 * GLA_QK + 2 * GLA_V
    w_glr = w_in[:, o:o + GLA_LORA]
    o += GLA_LORA
    w_c = w_in[:, o:o + GDN_CONV_CH]
    o += GDN_CONV_CH
    w_z = w_in[:, o:o + GDN_V]
    o += GDN_V
    w_da = w_in[:, o:o + GDN_HEADS]
    w_db = w_in[:, o + GDN_HEADS:o + 2 * GDN_HEADS]
    n_small = GLA_LORA + 2 * GDN_HEADS
    w_s = jnp.concatenate([w_glr, w_da, w_db, jnp.zeros((d, LANES - n_small), F32)], axis=1)
    lw2 = jnp.concatenate([lora_w2, jnp.zeros((LANES - GLA_LORA, GLA_QK), F32)], axis=0)
    pad_l = jnp.zeros((SMALL_DA,), F32)
    pad_r = jnp.zeros((LANES - SMALL_DB,), F32)
    hp = jnp.stack([jnp.concatenate([pad_l, a_log.astype(F32), pad_r]),
                    jnp.concatenate([pad_l, dt_bias.astype(F32), pad_r])])

    def tok(n):
        return pl.BlockSpec((tm, n), lambda i: (i, 0))

    outs = [
        (GLA_QK, tok(GLA_QK)), (GLA_QK, tok(GLA_QK)), (GLA_QK, tok(GLA_QK)),
        (None, pl.BlockSpec((tm // GLA_CHUNK, 1, GLA_QK), lambda i: (i, 0, 0))),
        (GLA_V, tok(GLA_V)), (GLA_V, tok(GLA_V)),
        (GDN_QK, tok(GDN_QK)), (GDN_QK, tok(GDN_QK)), (GDN_V, tok(GDN_V)),
        (LANES, tok(LANES)), (GDN_V, tok(GDN_V)),
    ]
    out_shape = [jax.ShapeDtypeStruct((m, n), F32) if n is not None
                 else jax.ShapeDtypeStruct((m // GLA_CHUNK, 1, GLA_QK), F32) for n, _ in outs]
    return pl.pallas_call(
        functools.partial(_even_prep_kernel, seq // tm),
        grid=(m // tm,),
        in_specs=[
            tok(d), _const_spec((1, d)),
            _const_spec(w_a.shape), _const_spec(w_s.shape), _const_spec(w_c.shape), _const_spec(w_z.shape),
            _const_spec(lw2.shape), _const_spec((1, GLA_QK)), _const_spec(conv_w.shape), _const_spec(hp.shape),
        ],
        out_specs=[s for _, s in outs],
        out_shape=out_shape,
        scratch_shapes=[pltpu.VMEM((SUBLANES, GDN_CONV_CH), F32)],
        compiler_params=pltpu.CompilerParams(dimension_semantics=("arbitrary",),
                                             vmem_limit_bytes=VMEM_LIMIT),
        name="even_prep",
    )(x2, _row(nw), w_a.astype(BF16), w_s.astype(BF16), w_c.astype(BF16), w_z.astype(BF16),
      lw2.astype(BF16), _row(lora_b), conv_w.astype(F32), hp)


def _even_chunk_kernel(x_ref, qe_ref, kt_ref, kd_ref, al_ref, gv_ref, sgg_ref, dq_ref, dk_ref, dv_ref,
                       gb_ref, sdz_ref, wo_ref, gn_ref, dn_ref, nw_ref,
                       o_ref, s_gla, s_gdn, o_scr):
    tc = x_ref.shape[0]

    @pl.when(pl.program_id(1) == 0)
    def _():
        s_gla[...] = jnp.zeros(s_gla.shape, F32)
        s_gdn[...] = jnp.zeros(s_gdn.shape, F32)

    n = GDN_CHUNK
    r_i = lax.broadcasted_iota(jnp.int32, (n, n), 0)
    c_i = lax.broadcasted_iota(jnp.int32, (n, n), 1)
    tril = c_i <= r_i
    strict = c_i < r_i
    bd_incl = ((c_i & (GLA_CHUNK - 1)) <= (r_i & (GLA_CHUNK - 1))) & ((c_i >= GLA_CHUNK) == (r_i >= GLA_CHUNK))
    lane = lax.broadcasted_iota(jnp.int32, (1, LANES), 1)
    m0 = jnp.where(lane < GLA_DK, 1.0, 0.0)
    m1 = 1.0 - m0

    def body(s, carry):
        r0 = pl.multiple_of(s * n, n)
        rows = pl.ds(r0, n)

        gbt = gb_ref[rows, :]
        for hd in range(GDN_HEADS):
            hl = slice(hd * GDN_DK, (hd + 1) * GDN_DK)
            q = dq_ref[rows, hl]
            k = dk_ref[rows, hl]
            v = dv_ref[rows, hl]
            gc = jnp.broadcast_to(gbt[:, SMALL_DA + hd:SMALL_DA + hd + 1], (n, n))
            beta = jnp.broadcast_to(gbt[:, SMALL_DB + hd:SMALL_DB + hd + 1], (n, n))
            decay = jnp.where(tril, jnp.exp(jnp.where(tril, gc - gc.T, 0.0)), 0.0)
            kb = k * beta
            low = jnp.where(strict, _mm3(kb, k, nt=True) * decay, 0.0)
            att = _mm_nt(q, k) * decay
            t = _inv_unit_lower(low, n, mm=_mm3)
            eg = jnp.exp(gc)
            sol = _mm3(t, jnp.concatenate([v * beta, kb * eg], axis=1))
            u = sol[:, 0:GDN_DV]
            w = sol[:, GDN_DV:]
            g_last = gc[n - 1:n, :]
            k_dec = k * jnp.exp(g_last - gc)
            st = s_gdn[hd]
            v_new = u - _mm3(w, st)
            o_scr[rows, GLA_V + hd * GDN_DV:GLA_V + (hd + 1) * GDN_DV] = _mm(q * eg, st) + _mm(att, v_new)
            s_gdn[hd] = st * jnp.exp(g_last) + _mm3(k_dec.T, v_new)

        for cc in range(n // GLA_CHUNK):
            crow = pl.ds(r0 + cc * GLA_CHUNK, GLA_CHUNK)
            for p in range(GLA_HEADS // 2):
                pls = slice(p * LANES, (p + 1) * LANES)
                qe = qe_ref[crow, pls]
                kt = kt_ref[crow, pls]
                kd = kd_ref[crow, pls]
                vs = jnp.concatenate([gv_ref[crow, (2 * p) * GLA_DV:(2 * p + 1) * GLA_DV],
                                      gv_ref[crow, (2 * p + 1) * GLA_DV:(2 * p + 2) * GLA_DV]], axis=0)
                qs = jnp.concatenate([qe * m0, qe * m1], axis=0)
                ks = jnp.concatenate([kd * m0, kd * m1], axis=0)
                att = jnp.where(bd_incl, _mm_nt(qs, jnp.concatenate([kt, kt], axis=0)), 0.0)
                st = s_gla[p]
                o = _mm(att, vs) + _mm_nt(qs, st)
                al = al_ref[s * (n // GLA_CHUNK) + cc][:, pls]
                s_gla[p] = st * al + _mm(vs.T, ks)
                o_scr[crow, (2 * p) * GLA_DV:(2 * p + 1) * GLA_DV] = o[0:GLA_CHUNK]
                o_scr[crow, (2 * p + 1) * GLA_DV:(2 * p + 2) * GLA_DV] = o[GLA_CHUNK:]
        return carry

    lax.fori_loop(0, tc // n, body, 0)

    for hd in range(GLA_HEADS + GDN_HEADS):
        hl = slice(hd * LANES, (hd + 1) * LANES)
        oh = o_scr[:, hl]
        nw = gn_ref[...] if hd < GLA_HEADS else dn_ref[...]
        gate = sgg_ref[:, hl] if hd < GLA_HEADS else sdz_ref[:, (hd - GLA_HEADS) * LANES:(hd - GLA_HEADS + 1) * LANES]
        o_scr[:, hl] = oh * lax.rsqrt(jnp.mean(oh * oh, -1, keepdims=True) + NORM_EPS) * nw * gate
    y = jnp.dot(o_scr[...].astype(BF16), wo_ref[...], preferred_element_type=F32)
    o_ref[...] = x_ref[...] + _rms(y, nw_ref[...])


def _even_chunk(x2, batch, seq, prep, w_out, gla_norm, gdn_norm, nw):
    m, d = x2.shape
    tc = min(TILE_CHUNK, seq)
    nt = seq // tc
    qe, kt, kd, al, gv, sgg, dq, dk, dv, gb, sdz = prep

    def tok(n):
        return pl.BlockSpec((tc, n), lambda b, j: (b * nt + j, 0))

    return pl.pallas_call(
        _even_chunk_kernel,
        grid=(batch, nt),
        in_specs=[
            tok(d), tok(GLA_QK), tok(GLA_QK), tok(GLA_QK),
            pl.BlockSpec((tc // GLA_CHUNK, 1, GLA_QK), lambda b, j: (b * nt + j, 0, 0)),
            tok(GLA_V), tok(GLA_V), tok(GDN_QK), tok(GDN_QK), tok(GDN_V), tok(LANES), tok(GDN_V),
            _const_spec((EVEN_OUT, d)), _const_spec((1, GLA_DV)), _const_spec((1, GDN_DV)), _const_spec((1, d)),
        ],
        out_specs=tok(d),
        out_shape=jax.ShapeDtypeStruct((m, d), F32),
        scratch_shapes=[
            pltpu.VMEM((GLA_HEADS // 2, GLA_DV, LANES), F32),
            pltpu.VMEM((GDN_HEADS, GDN_DK, GDN_DV), F32),
            pltpu.VMEM((tc, EVEN_OUT), F32),
        ],
        compiler_params=pltpu.CompilerParams(dimension_semantics=("arbitrary", "arbitrary"),
                                             vmem_limit_bytes=VMEM_LIMIT),
        name="even_chunk",
    )(x2, qe, kt, kd, al, gv, sgg, dq, dk, dv, gb, sdz, w_out.astype(BF16), _row(gla_norm), _row(gdn_norm),
      _row(nw))


def _odd_prep_kernel(tiles_per_seq,
                     x_ref, nw_ref, wp_ref, wl_ref, mu_ref, w0_ref, w2_ref, a0_ref, a2_ref, g2_ref,
                     kk_ref, ka_ref, rk_ref, ones_ref, cw_ref, cb_ref, wa_ref, ba_ref, wx_ref, bx_ref, lam_ref,
                     rt_ref, kt_ref, bt_ref, kh_ref, v_ref, wc_ref, g_ref, bonus_ref, la_ref, lb_ref, gly_ref,
                     carry_p, carry_l):
    first = (pl.program_id(0) % tiles_per_seq) == 0
    tm = x_ref.shape[0]
    h = _rms(x_ref[...], nw_ref[...]).astype(BF16)

    ps = jnp.dot(h, wp_ref[...], preferred_element_type=F32)
    carry = jnp.where(first, 0.0, carry_p[...])
    prev = _shift_rows(ps, carry, 1)
    carry_p[...] = ps[tm - SUBLANES:tm, :]
    ps = ps + (prev - ps) * mu_ref[...]
    r = ps[:, 0:RWKV_W]
    k = ps[:, RWKV_W:2 * RWKV_W]
    v = ps[:, 2 * RWKV_W:3 * RWKV_W]
    lora = ps[:, 3 * RWKV_W:3 * RWKV_W + LANES]
    gl = ps[:, 3 * RWKV_W + LANES:]
    w = -_softplus(-(w0_ref[...] + _mm(jnp.tanh(lora), w2_ref[...]))) - 0.5
    logw = -jnp.exp(w)
    a = _sigmoid(a0_ref[...] + _mm(lora, a2_ref[...]))
    g_ref[...] = _mm(_sigmoid(gl), g2_ref[...])
    ones_bd = ones_ref[...]
    kk = k * kk_ref[...]
    kk = kk * lax.rsqrt(_group_sum(kk * kk, ones_bd) + NORM_EPS)
    k = k * (1.0 + (a - 1.0) * ka_ref[...])
    bonus_ref[...] = _group_sum(r * k * rk_ref[...], ones_bd) * v
    gcum = _seg_cumsum(logw, RWKV_CHUNK)
    for c in range(tm // RWKV_CHUNK):
        wc_ref[c] = jnp.exp(gcum[(c + 1) * RWKV_CHUNK - 1:(c + 1) * RWKV_CHUNK, :])
    e_neg = jnp.exp(-gcum)
    rt_ref[...] = r * jnp.exp(gcum)
    kt_ref[...] = k * e_neg
    bt_ref[...] = kk * a * e_neg
    kh_ref[...] = kk * jnp.exp(gcum - logw)
    v_ref[...] = v

    pq = jnp.dot(h, wl_ref[...], preferred_element_type=F32)
    lx = pq[:, 0:LRU_WIDTH]
    carry2 = jnp.where(first, 0.0, carry_l[...])
    xb = lx * cw_ref[CONV_W - 1:CONV_W, :] + cb_ref[...]
    for s in range(1, CONV_W):
        xb = xb + _shift_rows(lx, carry2, s) * cw_ref[CONV_W - 1 - s:CONV_W - s, :]
    carry_l[...] = lx[tm - SUBLANES:tm, :]
    gate_r = _sigmoid(_mm(xb, wa_ref[...]) + ba_ref[...])
    gate_i = _sigmoid(_mm(xb, wx_ref[...]) + bx_ref[...])
    log_a = -LRU_C * gate_r * _softplus(-lam_ref[...])
    mult = jnp.sqrt(jnp.maximum(-jnp.tanh(log_a) * (jnp.exp(2.0 * log_a) + 1.0), 0.0))
    la_ref[...] = jnp.exp(log_a)
    lb_ref[...] = mult * gate_i * xb
    gly_ref[...] = _gelu_tanh(pq[:, LRU_WIDTH:])


def _block_diag(w):
    nb, n, _ = w.shape
    eye = jnp.eye(nb, dtype=w.dtype)
    return (eye[:, None, :, None] * w[:, :, None, :]).reshape(nb * n, nb * n)


def _odd_prep(x2, seq, nw, w_in, mu, w0, w2, a0, a2, g2, k_k, k_a, r_k, ones_bd,
              conv_w, conv_b, wa, ba, wx, bx, lam):
    m, d = x2.shape
    tm = min(TILE_PREP, seq)
    w_p = w_in[:, 0:RWKV_SHIFT]
    w_l = w_in[:, RWKV_SHIFT:]
    w2p = jnp.concatenate([w2, jnp.zeros((RWKV_A_LORA, RWKV_W), F32)], axis=0)
    a2p = jnp.concatenate([jnp.zeros((RWKV_W_LORA, RWKV_W), F32), a2], axis=0)

    def tok(n):
        return pl.BlockSpec((tm, n), lambda i: (i, 0))

    wspec = pl.BlockSpec((tm // RWKV_CHUNK, 1, RWKV_W), lambda i: (i, 0, 0))
    out_specs = [tok(RWKV_W)] * 5 + [wspec] + [tok(RWKV_W)] * 2 + [tok(LRU_WIDTH)] * 3
    out_shape = ([jax.ShapeDtypeStruct((m, RWKV_W), F32)] * 5
                 + [jax.ShapeDtypeStruct((m // RWKV_CHUNK, 1, RWKV_W), F32)]
                 + [jax.ShapeDtypeStruct((m, RWKV_W), F32)] * 2
                 + [jax.ShapeDtypeStruct((m, LRU_WIDTH), F32)] * 3)
    vec = _const_spec((1, RWKV_W))
    sq = _const_spec((RWKV_W, RWKV_W))
    return pl.pallas_call(
        functools.partial(_odd_prep_kernel, seq // tm),
        grid=(m // tm,),
        in_specs=[
            tok(d), _const_spec((1, d)), _const_spec(w_p.shape), _const_spec(w_l.shape),
            _const_spec((1, RWKV_SHIFT)), vec, _const_spec(w2p.shape), vec, _const_spec(a2p.shape),
            _const_spec(g2.shape), vec, vec, vec, sq,
            _const_spec(conv_w.shape), vec, sq, vec, sq, vec, vec,
        ],
        out_specs=out_specs,
        out_shape=out_shape,
        scratch_shapes=[pltpu.VMEM((SUBLANES, RWKV_SHIFT), F32), pltpu.VMEM((SUBLANES, LRU_WIDTH), F32)],
        compiler_params=pltpu.CompilerParams(dimension_semantics=("arbitrary",),
                                             vmem_limit_bytes=VMEM_LIMIT),
        name="odd_prep",
    )(x2, _row(nw), w_p.astype(BF16), w_l.astype(BF16), _row(mu), _row(w0), w2p.astype(BF16), _row(a0),
      a2p.astype(BF16), g2.astype(BF16), _row(k_k), _row(k_a), _row(r_k), ones_bd,
      conv_w.astype(F32), _row(conv_b), _block_diag(wa).astype(BF16), _row(ba),
      _block_diag(wx).astype(BF16), _row(bx), _row(lam))


def _odd_chunk_kernel(x_ref, rt_ref, kt_ref, bt_ref, kh_ref, v_ref, wc_ref, g_ref, bonus_ref, la_ref, lb_ref,
                      gly_ref, wo_ref, lnw_ref, lnb_ref, ones_ref, nw_ref,
                      o_ref, s_rwkv, h_lru, y_scr):
    tc = x_ref.shape[0]

    @pl.when(pl.program_id(1) == 0)
    def _():
        s_rwkv[...] = jnp.zeros(s_rwkv.shape, F32)
        h_lru[...] = jnp.zeros(h_lru.shape, F32)

    n = 2 * RWKV_CHUNK
    r_i = lax.broadcasted_iota(jnp.int32, (n, n), 0)
    c_i = lax.broadcasted_iota(jnp.int32, (n, n), 1)
    same = (c_i >= RWKV_CHUNK) == (r_i >= RWKV_CHUNK)
    bd_incl = ((c_i & (RWKV_CHUNK - 1)) <= (r_i & (RWKV_CHUNK - 1))) & same
    bd_strict = ((c_i & (RWKV_CHUNK - 1)) < (r_i & (RWKV_CHUNK - 1))) & same
    lane = lax.broadcasted_iota(jnp.int32, (1, LANES), 1)
    m0 = jnp.where(lane < RWKV_N, 1.0, 0.0)
    m1 = 1.0 - m0

    def stack(z):
        return jnp.concatenate([z * m0, z * m1], axis=0)

    def body(c, carry):
        r0 = pl.multiple_of(c * RWKV_CHUNK, RWKV_CHUNK)
        rows = pl.ds(r0, RWKV_CHUNK)
        for p in range(RWKV_HEADS // 2):
            pls = slice(p * LANES, (p + 1) * LANES)
            rt = rt_ref[rows, pls]
            kt = kt_ref[rows, pls]
            bt = bt_ref[rows, pls]
            kh = kh_ref[rows, pls]
            vv = v_ref[rows, pls]
            wc = wc_ref[c][:, pls]
            lhs = jnp.concatenate([stack(kh), stack(rt)], axis=0)
            mb = _mm_nt(lhs, jnp.concatenate([bt, bt], axis=0))
            mk = _mm_nt(lhs, jnp.concatenate([kt, kt], axis=0))
            m_bb = jnp.where(bd_strict, mb[0:n], 0.0)
            m_rb = jnp.where(bd_incl, mb[n:], 0.0)
            m_bk = jnp.where(bd_strict, mk[0:n], 0.0)
            m_rk = jnp.where(bd_incl, mk[n:], 0.0)
            t = _inv_unit_lower(m_bb, RWKV_CHUNK)
            vs = stack(vv)
            ks = stack(kt)
            bs = stack(bt)
            sol = _mm(t, jnp.concatenate([lhs[0:n], _mm(m_bk, vs)], axis=1))
            kkt = sol[:, 0:LANES]
            uv = sol[:, LANES:]
            corr = _mm(m_rb, sol)
            r_eff = lhs[n:] - corr[:, 0:LANES]
            y_loc = _mm(m_rk, vs) - corr[:, LANES:]
            st = s_rwkv[p]
            ut = _mm(st, kkt.T) + uv.T
            ys = _mm_nt(r_eff, st) + y_loc
            y_scr[rows, pls] = ys[0:RWKV_CHUNK] + ys[RWKV_CHUNK:]
            s_rwkv[p] = (st + _mm(vs.T, ks) - _mm(ut, bs)) * wc
        return carry

    lax.fori_loop(0, tc // RWKV_CHUNK, body, 0)

    a = la_ref[...]
    b = lb_ref[...]
    rows_i = lax.broadcasted_iota(jnp.int32, a.shape, 0)
    d = 1
    while d < tc:
        keep = rows_i >= d
        b = jnp.where(keep, a * pltpu.roll(b, d, 0) + b, b)
        a = jnp.where(keep, a * pltpu.roll(a, d, 0), a)
        d *= 2
    hl = b + a * h_lru[SUBLANES - 1:SUBLANES, :]
    h_lru[...] = hl[tc - SUBLANES:tc, :]
    y_lru = hl * gly_ref[...]

    ones_bd = ones_ref[...]
    y = y_scr[...]
    mu = _group_sum(y, ones_bd) * (1.0 / RWKV_N)
    yc = y - mu
    var = _group_sum(yc * yc, ones_bd) * (1.0 / RWKV_N)
    yn = yc * lax.rsqrt(var + RWKV_GN_EPS) * lnw_ref[...] + lnb_ref[...]
    y_rwkv = (yn + bonus_ref[...]) * g_ref[...]
    out = (jnp.dot(y_rwkv.astype(BF16), wo_ref[0:RWKV_W, :], preferred_element_type=F32)
           + jnp.dot(y_lru.astype(BF16), wo_ref[RWKV_W:, :], preferred_element_type=F32))
    o_ref[...] = x_ref[...] + _rms(out, nw_ref[...])


def _odd_chunk(x2, batch, seq, prep, w_out, ln_w, ln_b, ones_bd, nw):
    m, d = x2.shape
    tc = min(TILE_CHUNK, seq)
    nt = seq // tc
    rt, kt, bt, kh, v, wc, g, bonus, la, lb, gly = prep

    def tok(n):
        return pl.BlockSpec((tc, n), lambda b, j: (b * nt + j, 0))

    vec = _const_spec((1, RWKV_W))
    return pl.pallas_call(
        _odd_chunk_kernel,
        grid=(batch, nt),
        in_specs=[
            tok(d), tok(RWKV_W), tok(RWKV_W), tok(RWKV_W), tok(RWKV_W), tok(RWKV_W),
            pl.BlockSpec((tc // RWKV_CHUNK, 1, RWKV_W), lambda b, j: (b * nt + j, 0, 0)),
            tok(RWKV_W), tok(RWKV_W), tok(LRU_WIDTH), tok(LRU_WIDTH), tok(LRU_WIDTH),
            _const_spec((ODD_OUT, d)), vec, vec, _const_spec((RWKV_W, RWKV_W)), _const_spec((1, d)),
        ],
        out_specs=tok(d),
        out_shape=jax.ShapeDtypeStruct((m, d), F32),
        scratch_shapes=[
            pltpu.VMEM((RWKV_HEADS // 2, LANES, LANES), F32),
            pltpu.VMEM((SUBLANES, LRU_WIDTH), F32),
            pltpu.VMEM((tc, RWKV_W), F32),
        ],
        compiler_params=pltpu.CompilerParams(dimension_semantics=("arbitrary", "arbitrary"),
                                             vmem_limit_bytes=VMEM_LIMIT),
        name="odd_chunk",
    )(x2, rt, kt, bt, kh, v, wc, g, bonus, la, lb, gly, w_out.astype(BF16), _row(ln_w), _row(ln_b), ones_bd,
      _row(nw))


def _even_layer(x2, batch, seq, nw, w_in, w_out, lora_w2, lora_b, gla_norm, conv_w, a_log, dt_bias, gdn_norm):
    prep = _even_prep(x2, seq, nw[2], w_in, lora_w2, lora_b, conv_w, a_log, dt_bias)
    return _even_chunk(x2, batch, seq, prep, w_out, gla_norm, gdn_norm, nw[3])


def _odd_layer(x2, batch, seq, nw, w_in, w_out, mu, w0, w2, a0, a2, g2, k_k, k_a, r_k, ln_w, ln_b,
               conv_w, conv_b, wa, ba, wx, bx, lam):
    grp = jnp.arange(RWKV_W) // RWKV_N
    ones_bd = (grp[:, None] == grp[None, :]).astype(BF16)
    prep = _odd_prep(x2, seq, nw[2], w_in, mu, w0, w2, a0, a2, g2, k_k, k_a, r_k, ones_bd,
                     conv_w, conv_b, wa, ba, wx, bx, lam)
    return _odd_chunk(x2, batch, seq, prep, w_out, ln_w, ln_b, ones_bd, nw[3])


def kernel(x, norm_w, ffn_w_gate, ffn_w_up, ffn_w_down, even_w_in, even_w_out, gla_lora_w2, gla_lora_b, gla_norm, gdn_conv, gdn_a_log, gdn_dt_bias, gdn_norm, odd_w_in, odd_w_out, rwkv_mu, rwkv_w0, rwkv_w2, rwkv_a0, rwkv_a2, rwkv_g2, rwkv_k_k, rwkv_k_a, rwkv_r_k, rwkv_ln_w, rwkv_ln_b, lru_conv_w, lru_conv_b, lru_wa, lru_ba, lru_wx, lru_bx, lru_lambda):
    batch, seq, d = x.shape
    depth = norm_w.shape[0]
    x2 = x.reshape(batch * seq, d)
    for i in range(depth):
        j = i // 2
        nw = norm_w[i]
        x2 = _ffn(x2, nw[0], nw[1], ffn_w_gate[i, 0], ffn_w_up[i, 0], ffn_w_down[i, 0])
        if i % 2 == 0:
            x2 = _even_layer(x2, batch, seq, nw, even_w_in[j], even_w_out[j], gla_lora_w2[j], gla_lora_b[j],
                             gla_norm[j], gdn_conv[j], gdn_a_log[j], gdn_dt_bias[j], gdn_norm[j])
        else:
            x2 = _odd_layer(x2, batch, seq, nw, odd_w_in[j], odd_w_out[j], rwkv_mu[j], rwkv_w0[j], rwkv_w2[j],
                            rwkv_a0[j], rwkv_a2[j], rwkv_g2[j], rwkv_k_k[j], rwkv_k_a[j],
                            rwkv_r_k[j].reshape(-1), rwkv_ln_w[j], rwkv_ln_b[j], lru_conv_w[j], lru_conv_b[j],
                            lru_wa[j], lru_ba[j], lru_wx[j], lru_bx[j], lru_lambda[j])
        x2 = _ffn(x2, nw[4], nw[5], ffn_w_gate[i, 1], ffn_w_up[i, 1], ffn_w_down[i, 1])
    return x2.reshape(batch, seq, d)
```

```python
import functools

import jax
import jax.numpy as jnp
from jax import lax
from jax.experimental import pallas as pl
from jax.experimental.pallas import tpu as pltpu

F32 = jnp.float32
BF16 = jnp.bfloat16

D_MODEL = 1024
D_FF = 2816
FFN_RES = 0.5
NORM_EPS = 1e-6
CONV_W = 4

GLA_HEADS = 4
GLA_DK = 64
GLA_DV = 128
GLA_LORA = 16
GLA_GATE_NORM = 16.0
GLA_QK = GLA_HEADS * GLA_DK
GLA_V = GLA_HEADS * GLA_DV
GLA_CHUNK = 64

GDN_HEADS = 4
GDN_DK = 128
GDN_DV = 128
GDN_QK = GDN_HEADS * GDN_DK
GDN_V = GDN_HEADS * GDN_DV
GDN_CONV_CH = 2 * GDN_QK + GDN_V
GDN_CHUNK = 128

RWKV_HEADS = 8
RWKV_N = 64
RWKV_W = RWKV_HEADS * RWKV_N
RWKV_W_LORA = 64
RWKV_A_LORA = 64
RWKV_G_LORA = 128
RWKV_GN_EPS = 64e-5
RWKV_SHIFT = 3 * RWKV_W + RWKV_W_LORA + RWKV_A_LORA + RWKV_G_LORA
RWKV_CHUNK = 64

LRU_WIDTH = 512
LRU_BLOCKS = 8
LRU_BW = LRU_WIDTH // LRU_BLOCKS
LRU_C = 8.0

EVEN_OUT = GLA_V + GDN_V
ODD_OUT = RWKV_W + LRU_WIDTH

LANES = 128
SUBLANES = 8
SMALL_GLR = 0
SMALL_DA = GLA_LORA
SMALL_DB = GLA_LORA + GDN_HEADS

TILE_FFN = 512
TILE_PREP = 512
TILE_CHUNK = 128
FF_CHUNK = 1408
VMEM_LIMIT = 56 * 1024 * 1024


def _mm(a, b):
    return jnp.dot(a.astype(BF16), b.astype(BF16), preferred_element_type=F32)


def _mm_nt(a, b):
    return lax.dot_general(a.astype(BF16), b.astype(BF16), (((1,), (1,)), ((), ())),
                           preferred_element_type=F32)


def _rms(x, w):
    return x * lax.rsqrt(jnp.mean(x * x, axis=-1, keepdims=True) + NORM_EPS) * w


def _sigmoid(x):
    return 1.0 / (1.0 + jnp.exp(-x))


def _silu(x):
    return x * _sigmoid(x)


def _softplus(x):
    return jnp.maximum(x, 0.0) + jnp.log1p(jnp.exp(-jnp.abs(x)))


def _gelu_tanh(x):
    c = 0.7978845608028654
    return 0.5 * x * (1.0 + jnp.tanh(c * (x + 0.044715 * (x * x * x))))


def _seg_cumsum(x, seg):
    rows = lax.broadcasted_iota(jnp.int32, x.shape, 0) & (seg - 1)
    d = 1
    while d < seg:
        x = x + jnp.where(rows >= d, pltpu.roll(x, d, 0), 0.0)
        d *= 2
    return x


def _shift_rows(x, carry, s):
    sh = pltpu.roll(x, s, 0)
    c = pltpu.roll(carry, s, 0)
    rows = lax.broadcasted_iota(jnp.int32, c.shape, 0)
    head = jnp.where(rows < s, c, sh[0:SUBLANES])
    return jnp.concatenate([head, sh[SUBLANES:]], axis=0)


def _group_sum(x, ones_bd):
    hi = x.astype(BF16)
    r1 = x - hi.astype(F32)
    mid = r1.astype(BF16)
    lo = (r1 - mid.astype(F32)).astype(BF16)
    dot = functools.partial(jnp.dot, preferred_element_type=F32)
    return dot(hi, ones_bd) + dot(mid, ones_bd) + dot(lo, ones_bd)


def _split(a):
    hi = a.astype(BF16)
    return hi, (a - hi.astype(F32)).astype(BF16)


def _mm3(a, b, nt=False):
    dims = (((1,), (1 if nt else 0,)), ((), ()))
    dot = functools.partial(lax.dot_general, dimension_numbers=dims, preferred_element_type=F32)
    ah, al = _split(a)
    bh, bl = _split(b)
    return dot(ah, bh) + dot(al, bh) + dot(ah, bl)


def _inv_unit_lower(low, n, mm=_mm):
    r = lax.broadcasted_iota(jnp.int32, low.shape, 0)
    c = lax.broadcasted_iota(jnp.int32, low.shape, 1)
    neg = -low
    t = jnp.where(r == c, 1.0, 0.0) + neg
    p = neg
    k = 2
    while k < n:
        p = mm(p, p)
        t = t + mm(p, t)
        k *= 2
    return t


def _inv_unit_lower_many(lows, n, mm=_mm):
    r = lax.broadcasted_iota(jnp.int32, lows[0].shape, 0)
    c = lax.broadcasted_iota(jnp.int32, lows[0].shape, 1)
    eye = jnp.where(r == c, 1.0, 0.0)
    ps = [-low for low in lows]
    ts = [eye + p for p in ps]
    k = 2
    while k < n:
        ps = [mm(p, p) for p in ps]
        ts = [t + mm(p, t) for p, t in zip(ps, ts)]
        k *= 2
    return ts


def _const_spec(shape):
    nd = len(shape)
    return pl.BlockSpec(shape, lambda *_: (0,) * nd, pipeline_mode=pl.Buffered(1))


def _row(v):
    return v.reshape(1, -1).astype(F32)


def _ffn_kernel(x_ref, nw_ref, wg_ref, wu_ref, wd_ref, o_ref):
    x = x_ref[...]
    h = _rms(x, nw_ref[0:1, :]).astype(BF16)
    acc = jnp.zeros(x.shape, F32)
    for c in range(D_FF // FF_CHUNK):
        sl = slice(c * FF_CHUNK, (c + 1) * FF_CHUNK)
        g = jnp.dot(h, wg_ref[:, sl], preferred_element_type=F32)
        u = jnp.dot(h, wu_ref[:, sl], preferred_element_type=F32)
        a = (_silu(g) * u).astype(BF16)
        acc = acc + jnp.dot(a, wd_ref[sl, :], preferred_element_type=F32)
    o_ref[...] = x + FFN_RES * _rms(acc, nw_ref[1:2, :])


def _ffn(x2, nw_pre, nw_post, wg, wu, wd):
    m, d = x2.shape
    tm = min(TILE_FFN, m)
    nw = jnp.stack([nw_pre, nw_post]).astype(F32)
    return pl.pallas_call(
        _ffn_kernel,
        grid=(m // tm,),
        in_specs=[
            pl.BlockSpec((tm, d), lambda i: (i, 0)),
            _const_spec((2, d)),
            _const_spec((d, D_FF)),
            _const_spec((d, D_FF)),
            _const_spec((D_FF, d)),
        ],
        out_specs=pl.BlockSpec((tm, d), lambda i: (i, 0)),
        out_shape=jax.ShapeDtypeStruct((m, d), F32),
        compiler_params=pltpu.CompilerParams(dimension_semantics=("arbitrary",),
                                             vmem_limit_bytes=VMEM_LIMIT),
        name="ffn",
    )(x2, nw, wg.astype(BF16), wu.astype(BF16), wd.astype(BF16))


def _even_prep_kernel(tiles_per_seq,
                      x_ref, nw_ref, wa_ref, ws_ref, wc_ref, wz_ref, lw2_ref, lb_ref, conv_ref, hp_ref,
                      qe_ref, kt_ref, kd_ref, al_ref, gv_ref, sgg_ref, dq_ref, dk_ref, dv_ref, gb_ref,
                      sdz_ref, carry_ref):
    first = (pl.program_id(0) % tiles_per_seq) == 0
    tm = x_ref.shape[0]
    h = _rms(x_ref[...], nw_ref[...]).astype(BF16)

    pa = jnp.dot(h, wa_ref[...], preferred_element_type=F32)
    ps = jnp.dot(h, ws_ref[...], preferred_element_type=F32)
    z = _mm(ps, lw2_ref[...]) + lb_ref[...]
    log_a = -_softplus(-z) * (1.0 / GLA_GATE_NORM)
    b = _seg_cumsum(log_a, GLA_CHUNK)
    gk = pa[:, GLA_QK:2 * GLA_QK]
    qe_ref[...] = pa[:, 0:GLA_QK] * (GLA_DK ** -0.5) * jnp.exp(b)
    kt_ref[...] = gk * jnp.exp(-b)
    for c in range(tm // GLA_CHUNK):
        cs = slice(c * GLA_CHUNK, (c + 1) * GLA_CHUNK)
        b_last = b[(c + 1) * GLA_CHUNK - 1:(c + 1) * GLA_CHUNK, :]
        al_ref[c] = jnp.exp(b_last)
        kd_ref[cs, :] = gk[cs] * jnp.exp(b_last - b[cs])
    gv_ref[...] = pa[:, 2 * GLA_QK:2 * GLA_QK + GLA_V]
    sgg_ref[...] = _silu(pa[:, 2 * GLA_QK + GLA_V:])

    pc = jnp.dot(h, wc_ref[...], preferred_element_type=F32)
    carry = jnp.where(first, 0.0, carry_ref[...])
    conv = pc * conv_ref[CONV_W - 1:CONV_W, :]
    for s in range(1, CONV_W):
        conv = conv + _shift_rows(pc, carry, s) * conv_ref[CONV_W - 1 - s:CONV_W - s, :]
    carry_ref[...] = pc[tm - SUBLANES:tm, :]
    c = _silu(conv)
    for hd in range(GDN_HEADS):
        lo, hi = hd * GDN_DK, (hd + 1) * GDN_DK
        cq = c[:, lo:hi]
        ck = c[:, GDN_QK + lo:GDN_QK + hi]
        dq_ref[:, lo:hi] = cq * lax.rsqrt(jnp.sum(cq * cq, -1, keepdims=True) + NORM_EPS) * (GDN_DK ** -0.5)
        dk_ref[:, lo:hi] = ck * lax.rsqrt(jnp.sum(ck * ck, -1, keepdims=True) + NORM_EPS)
    dv_ref[...] = c[:, 2 * GDN_QK:]

    g = -jnp.exp(hp_ref[0:1, :]) * _softplus(ps + hp_ref[1:2, :])
    gc = _seg_cumsum(g, GDN_CHUNK)
    lane = lax.broadcasted_iota(jnp.int32, ps.shape, 1)
    is_g = (lane >= SMALL_DA) & (lane < SMALL_DB)
    gb_ref[...] = jnp.where(is_g, gc, _sigmoid(ps))

    sdz_ref[...] = _silu(jnp.dot(h, wz_ref[...], preferred_element_type=F32))


def _even_prep(x2, seq, nw, w_in, lora_w2, lora_b, conv_w, a_log, dt_bias):
    m, d = x2.shape
    tm = min(TILE_PREP, seq)
    o = 0
    w_a = w_in[:, 0:2 * GLA_QK + 2 * GLA_V]
    o = 2 * GLA_QK + 2 * GLA_V
    w_glr = w_in[:, o:o + GLA_LORA]
    o += GLA_LORA
    w_c = w_in[:, o:o + GDN_CONV_CH]
    o += GDN_CONV_CH
    w_z = w_in[:, o:o + GDN_V]
    o += GDN_V
    w_da = w_in[:, o:o + GDN_HEADS]
    w_db = w_in[:, o + GDN_HEADS:o + 2 * GDN_HEADS]
    n_small = GLA_LORA + 2 * GDN_HEADS
    w_s = jnp.concatenate([w_glr, w_da, w_db, jnp.zeros((d, LANES - n_small), F32)], axis=1)
    lw2 = jnp.concatenate([lora_w2, jnp.zeros((LANES - GLA_LORA, GLA_QK), F32)], axis=0)
    pad_l = jnp.zeros((SMALL_DA,), F32)
    pad_r = jnp.zeros((LANES - SMALL_DB,), F32)
    hp = jnp.stack([jnp.concatenate([pad_l, a_log.astype(F32), pad_r]),
                    jnp.concatenate([pad_l, dt_bias.astype(F32), pad_r])])

    def tok(n):
        return pl.BlockSpec((tm, n), lambda i: (i, 0))

    outs = [
        (GLA_QK, tok(GLA_QK)), (GLA_QK, tok(GLA_QK)), (GLA_QK, tok(GLA_QK)),
        (None, pl.BlockSpec((tm // GLA_CHUNK, 1, GLA_QK), lambda i: (i, 0, 0))),
        (GLA_V, tok(GLA_V)), (GLA_V, tok(GLA_V)),
        (GDN_QK, tok(GDN_QK)), (GDN_QK, tok(GDN_QK)), (GDN_V, tok(GDN_V)),
        (LANES, tok(LANES)), (GDN_V, tok(GDN_V)),
    ]
    out_shape = [jax.ShapeDtypeStruct((m, n), F32) if n is not None
                 else jax.ShapeDtypeStruct((m // GLA_CHUNK, 1, GLA_QK), F32) for n, _ in outs]
    return pl.pallas_call(
        functools.partial(_even_prep_kernel, seq // tm),
        grid=(m // tm,),
        in_specs=[
            tok(d), _const_spec((1, d)),
            _const_spec(w_a.shape), _const_spec(w_s.shape), _const_spec(w_c.shape), _const_spec(w_z.shape),
            _const_spec(lw2.shape), _const_spec((1, GLA_QK)), _const_spec(conv_w.shape), _const_spec(hp.shape),
        ],
        out_specs=[s for _, s in outs],
        out_shape=out_shape,
        scratch_shapes=[pltpu.VMEM((SUBLANES, GDN_CONV_CH), F32)],
        compiler_params=pltpu.CompilerParams(dimension_semantics=("arbitrary",),
                                             vmem_limit_bytes=VMEM_LIMIT),
        name="even_prep",
    )(x2, _row(nw), w_a.astype(BF16), w_s.astype(BF16), w_c.astype(BF16), w_z.astype(BF16),
      lw2.astype(BF16), _row(lora_b), conv_w.astype(F32), hp)


def _even_chunk_kernel(x_ref, qe_ref, kt_ref, kd_ref, al_ref, gv_ref, sgg_ref, dq_ref, dk_ref, dv_ref,
                       gb_ref, sdz_ref, wo_ref, gn_ref, dn_ref, nw_ref,
                       o_ref, s_gla, s_gdn, o_scr):
    nb, tc, d = x_ref.shape

    @pl.when(pl.program_id(0) == 0)
    def _():
        s_gla[...] = jnp.zeros(s_gla.shape, F32)
        s_gdn[...] = jnp.zeros(s_gdn.shape, F32)

    n = GDN_CHUNK
    r_i = lax.broadcasted_iota(jnp.int32, (n, n), 0)
    c_i = lax.broadcasted_iota(jnp.int32, (n, n), 1)
    tril = c_i <= r_i
    strict = c_i < r_i
    bd_incl = ((c_i & (GLA_CHUNK - 1)) <= (r_i & (GLA_CHUNK - 1))) & ((c_i >= GLA_CHUNK) == (r_i >= GLA_CHUNK))
    lane = lax.broadcasted_iota(jnp.int32, (1, LANES), 1)
    m0 = jnp.where(lane < GLA_DK, 1.0, 0.0)
    m1 = 1.0 - m0

    def body(s, carry):
        r0 = pl.multiple_of(s * n, n)
        rows = pl.ds(r0, n)
        probs = [(b, hd) for b in range(nb) for hd in range(GDN_HEADS)]
        np_ = len(probs)
        hls = [slice(hd * GDN_DK, (hd + 1) * GDN_DK) for _, hd in probs]
        ld = lambda ref: [ref[b, rows, hls[i]] for i, (b, _) in enumerate(probs)]
        q, k, v = ld(dq_ref), ld(dk_ref), ld(dv_ref)
        gbt = [gb_ref[b, rows, :] for b in range(nb)]
        gc = [jnp.broadcast_to(gbt[b][:, SMALL_DA + hd:SMALL_DA + hd + 1], (n, n)) for b, hd in probs]
        beta = [jnp.broadcast_to(gbt[b][:, SMALL_DB + hd:SMALL_DB + hd + 1], (n, n)) for b, hd in probs]
        decay = [jnp.where(tril, jnp.exp(jnp.where(tril, g - g.T, 0.0)), 0.0) for g in gc]
        kb = [k[i] * beta[i] for i in range(np_)]
        low = [jnp.where(strict, _mm3(kb[i], k[i], nt=True) * decay[i], 0.0) for i in range(np_)]
        att = [_mm_nt(q[i], k[i]) * decay[i] for i in range(np_)]
        t = _inv_unit_lower_many(low, n, mm=_mm3)
        eg = [jnp.exp(g) for g in gc]
        sol = [_mm3(t[i], jnp.concatenate([v[i] * beta[i], kb[i] * eg[i]], axis=1)) for i in range(np_)]
        g_last = [g[n - 1:n, :] for g in gc]
        k_dec_t = [(k[i] * jnp.exp(g_last[i] - gc[i])).T for i in range(np_)]
        st = [s_gdn[b, hd] for b, hd in probs]
        v_new = [sol[i][:, 0:GDN_DV] - _mm3(sol[i][:, GDN_DV:], st[i]) for i in range(np_)]
        o = [_mm(q[i] * eg[i], st[i]) + _mm(att[i], v_new[i]) for i in range(np_)]
        upd = [_mm3(k_dec_t[i], v_new[i]) for i in range(np_)]
        for i, (b, hd) in enumerate(probs):
            o_scr[pl.ds(b * tc + r0, n), GLA_V + hd * GDN_DV:GLA_V + (hd + 1) * GDN_DV] = o[i]
            s_gdn[b, hd] = st[i] * jnp.exp(g_last[i]) + upd[i]

        gprobs = [(b, p) for b in range(nb) for p in range(GLA_HEADS // 2)]
        ng = len(gprobs)
        pls = [slice(p * LANES, (p + 1) * LANES) for _, p in gprobs]
        for cc in range(n // GLA_CHUNK):
            crow = pl.ds(r0 + cc * GLA_CHUNK, GLA_CHUNK)
            ldg = lambda ref: [ref[b, crow, pls[i]] for i, (b, _) in enumerate(gprobs)]
            qe, kt, kd = ldg(qe_ref), ldg(kt_ref), ldg(kd_ref)
            vs = [jnp.concatenate([gv_ref[b, crow, (2 * p) * GLA_DV:(2 * p + 1) * GLA_DV],
                                   gv_ref[b, crow, (2 * p + 1) * GLA_DV:(2 * p + 2) * GLA_DV]], axis=0)
                  for b, p in gprobs]
            qs = [jnp.concatenate([z * m0, z * m1], axis=0) for z in qe]
            ks = [jnp.concatenate([z * m0, z * m1], axis=0) for z in kd]
            att = [jnp.where(bd_incl, _mm_nt(qs[i], jnp.concatenate([kt[i], kt[i]], axis=0)), 0.0)
                   for i in range(ng)]
            sg = [s_gla[b, p] for b, p in gprobs]
            og = [_mm(att[i], vs[i]) + _mm_nt(qs[i], sg[i]) for i in range(ng)]
            ug = [_mm(vs[i].T, ks[i]) for i in range(ng)]
            for i, (b, p) in enumerate(gprobs):
                ocrow = pl.ds(b * tc + r0 + cc * GLA_CHUNK, GLA_CHUNK)
                al = al_ref[b, s * (n // GLA_CHUNK) + cc][:, pls[i]]
                s_gla[b, p] = sg[i] * al + ug[i]
                o_scr[ocrow, (2 * p) * GLA_DV:(2 * p + 1) * GLA_DV] = og[i][0:GLA_CHUNK]
                o_scr[ocrow, (2 * p + 1) * GLA_DV:(2 * p + 2) * GLA_DV] = og[i][GLA_CHUNK:]
        return carry

    lax.fori_loop(0, tc // n, body, 0)

    sgg = sgg_ref[...].reshape(nb * tc, GLA_V)
    sdz = sdz_ref[...].reshape(nb * tc, GDN_V)
    for hd in range(GLA_HEADS + GDN_HEADS):
        hl = slice(hd * LANES, (hd + 1) * LANES)
        oh = o_scr[:, hl]
        nw = gn_ref[...] if hd < GLA_HEADS else dn_ref[...]
        gate = sgg[:, hl] if hd < GLA_HEADS else sdz[:, (hd - GLA_HEADS) * LANES:(hd - GLA_HEADS + 1) * LANES]
        o_scr[:, hl] = oh * lax.rsqrt(jnp.mean(oh * oh, -1, keepdims=True) + NORM_EPS) * nw * gate
    y = jnp.dot(o_scr[...].astype(BF16), wo_ref[...], preferred_element_type=F32)
    o_ref[...] = x_ref[...] + _rms(y, nw_ref[...]).reshape(nb, tc, d)


def _even_chunk(x2, batch, seq, prep, w_out, gla_norm, gdn_norm, nw):
    m, d = x2.shape
    tc = min(TILE_CHUNK, seq)
    qe, kt, kd, al, gv, sgg, dq, dk, dv, gb, sdz = [a.reshape((batch, a.shape[0] // batch) + a.shape[1:])
                                                     for a in prep]

    def tok(n):
        return pl.BlockSpec((batch, tc, n), lambda j: (0, j, 0))

    out = pl.pallas_call(
        _even_chunk_kernel,
        grid=(seq // tc,),
        in_specs=[
            tok(d), tok(GLA_QK), tok(GLA_QK), tok(GLA_QK),
            pl.BlockSpec((batch, tc // GLA_CHUNK, 1, GLA_QK), lambda j: (0, j, 0, 0)),
            tok(GLA_V), tok(GLA_V), tok(GDN_QK), tok(GDN_QK), tok(GDN_V), tok(LANES), tok(GDN_V),
            _const_spec((EVEN_OUT, d)), _const_spec((1, GLA_DV)), _const_spec((1, GDN_DV)), _const_spec((1, d)),
        ],
        out_specs=tok(d),
        out_shape=jax.ShapeDtypeStruct((batch, seq, d), F32),
        scratch_shapes=[
            pltpu.VMEM((batch, GLA_HEADS // 2, GLA_DV, LANES), F32),
            pltpu.VMEM((batch, GDN_HEADS, GDN_DK, GDN_DV), F32),
            pltpu.VMEM((batch * tc, EVEN_OUT), F32),
        ],
        compiler_params=pltpu.CompilerParams(dimension_semantics=("arbitrary",),
                                             vmem_limit_bytes=VMEM_LIMIT),
        name="even_chunk",
    )(x2.reshape(batch, seq, d), qe, kt, kd, al, gv, sgg, dq, dk, dv, gb, sdz, w_out.astype(BF16),
      _row(gla_norm), _row(gdn_norm), _row(nw))
    return out.reshape(m, d)


def _odd_prep_kernel(tiles_per_seq,
                     x_ref, nw_ref, wp_ref, wl_ref, mu_ref, w0_ref, w2_ref, a0_ref, a2_ref, g2_ref,
                     kk_ref, ka_ref, rk_ref, ones_ref, cw_ref, cb_ref, wa_ref, ba_ref, wx_ref, bx_ref, lam_ref,
                     rt_ref, kt_ref, bt_ref, kh_ref, v_ref, wc_ref, g_ref, bonus_ref, la_ref, lb_ref, gly_ref,
                     carry_p, carry_l):
    first = (pl.program_id(0) % tiles_per_seq) == 0
    tm = x_ref.shape[0]
    h = _rms(x_ref[...], nw_ref[...]).astype(BF16)

    ps = jnp.dot(h, wp_ref[...], preferred_element_type=F32)
    carry = jnp.where(first, 0.0, carry_p[...])
    prev = _shift_rows(ps, carry, 1)
    carry_p[...] = ps[tm - SUBLANES:tm, :]
    ps = ps + (prev - ps) * mu_ref[...]
    r = ps[:, 0:RWKV_W]
    k = ps[:, RWKV_W:2 * RWKV_W]
    v = ps[:, 2 * RWKV_W:3 * RWKV_W]
    lora = ps[:, 3 * RWKV_W:3 * RWKV_W + LANES]
    gl = ps[:, 3 * RWKV_W + LANES:]
    w = -_softplus(-(w0_ref[...] + _mm(jnp.tanh(lora), w2_ref[...]))) - 0.5
    logw = -jnp.exp(w)
    a = _sigmoid(a0_ref[...] + _mm(lora, a2_ref[...]))
    g_ref[...] = _mm(_sigmoid(gl), g2_ref[...])
    ones_bd = ones_ref[...]
    kk = k * kk_ref[...]
    kk = kk * lax.rsqrt(_group_sum(kk * kk, ones_bd) + NORM_EPS)
    k = k * (1.0 + (a - 1.0) * ka_ref[...])
    bonus_ref[...] = _group_sum(r * k * rk_ref[...], ones_bd) * v
    gcum = _seg_cumsum(logw, RWKV_CHUNK)
    for c in range(tm // RWKV_CHUNK):
        wc_ref[c] = jnp.exp(gcum[(c + 1) * RWKV_CHUNK - 1:(c + 1) * RWKV_CHUNK, :])
    e_neg = jnp.exp(-gcum)
    rt_ref[...] = r * jnp.exp(gcum)
    kt_ref[...] = k * e_neg
    bt_ref[...] = kk * a * e_neg
    kh_ref[...] = kk * jnp.exp(gcum - logw)
    v_ref[...] = v

    pq = jnp.dot(h, wl_ref[...], preferred_element_type=F32)
    lx = pq[:, 0:LRU_WIDTH]
    carry2 = jnp.where(first, 0.0, carry_l[...])
    xb = lx * cw_ref[CONV_W - 1:CONV_W, :] + cb_ref[...]
    for s in range(1, CONV_W):
        xb = xb + _shift_rows(lx, carry2, s) * cw_ref[CONV_W - 1 - s:CONV_W - s, :]
    carry_l[...] = lx[tm - SUBLANES:tm, :]
    gate_r = _sigmoid(_mm(xb, wa_ref[...]) + ba_ref[...])
    gate_i = _sigmoid(_mm(xb, wx_ref[...]) + bx_ref[...])
    log_a = -LRU_C * gate_r * _softplus(-lam_ref[...])
    mult = jnp.sqrt(jnp.maximum(-jnp.tanh(log_a) * (jnp.exp(2.0 * log_a) + 1.0), 0.0))
    la_ref[...] = jnp.exp(log_a)
    lb_ref[...] = mult * gate_i * xb
    gly_ref[...] = _gelu_tanh(pq[:, LRU_WIDTH:])


def _block_diag(w):
    nb, n, _ = w.shape
    eye = jnp.eye(nb, dtype=w.dtype)
    return (eye[:, None, :, None] * w[:, :, None, :]).reshape(nb * n, nb * n)


def _odd_prep(x2, seq, nw, w_in, mu, w0, w2, a0, a2, g2, k_k, k_a, r_k, ones_bd,
              conv_w, conv_b, wa, ba, wx, bx, lam):
    m, d = x2.shape
    tm = min(TILE_PREP, seq)
    w_p = w_in[:, 0:RWKV_SHIFT]
    w_l = w_in[:, RWKV_SHIFT:]
    w2p = jnp.concatenate([w2, jnp.zeros((RWKV_A_LORA, RWKV_W), F32)], axis=0)
    a2p = jnp.concatenate([jnp.zeros((RWKV_W_LORA, RWKV_W), F32), a2], axis=0)

    def tok(n):
        return pl.BlockSpec((tm, n), lambda i: (i, 0))

    wspec = pl.BlockSpec((tm // RWKV_CHUNK, 1, RWKV_W), lambda i: (i, 0, 0))
    out_specs = [tok(RWKV_W)] * 5 + [wspec] + [tok(RWKV_W)] * 2 + [tok(LRU_WIDTH)] * 3
    out_shape = ([jax.ShapeDtypeStruct((m, RWKV_W), F32)] * 5
                 + [jax.ShapeDtypeStruct((m // RWKV_CHUNK, 1, RWKV_W), F32)]
                 + [jax.ShapeDtypeStruct((m, RWKV_W), F32)] * 2
                 + [jax.ShapeDtypeStruct((m, LRU_WIDTH), F32)] * 3)
    vec = _const_spec((1, RWKV_W))
    sq = _const_spec((RWKV_W, RWKV_W))
    return pl.pallas_call(
        functools.partial(_odd_prep_kernel, seq // tm),
        grid=(m // tm,),
        in_specs=[
            tok(d), _const_spec((1, d)), _const_spec(w_p.shape), _const_spec(w_l.shape),
            _const_spec((1, RWKV_SHIFT)), vec, _const_spec(w2p.shape), vec, _const_spec(a2p.shape),
            _const_spec(g2.shape), vec, vec, vec, sq,
            _const_spec(conv_w.shape), vec, sq, vec, sq, vec, vec,
        ],
        out_specs=out_specs,
        out_shape=out_shape,
        scratch_shapes=[pltpu.VMEM((SUBLANES, RWKV_SHIFT), F32), pltpu.VMEM((SUBLANES, LRU_WIDTH), F32)],
        compiler_params=pltpu.CompilerParams(dimension_semantics=("arbitrary",),
                                             vmem_limit_bytes=VMEM_LIMIT),
        name="odd_prep",
    )(x2, _row(nw), w_p.astype(BF16), w_l.astype(BF16), _row(mu), _row(w0), w2p.astype(BF16), _row(a0),
      a2p.astype(BF16), g2.astype(BF16), _row(k_k), _row(k_a), _row(r_k), ones_bd,
      conv_w.astype(F32), _row(conv_b), _block_diag(wa).astype(BF16), _row(ba),
      _block_diag(wx).astype(BF16), _row(bx), _row(lam))


def _odd_chunk_kernel(x_ref, rt_ref, kt_ref, bt_ref, kh_ref, v_ref, wc_ref, g_ref, bonus_ref, la_ref, lb_ref,
                      gly_ref, wo_ref, lnw_ref, lnb_ref, ones_ref, nw_ref,
                      o_ref, s_rwkv, h_lru, y_scr):
    nb, tc, d = x_ref.shape

    @pl.when(pl.program_id(0) == 0)
    def _():
        s_rwkv[...] = jnp.zeros(s_rwkv.shape, F32)
        h_lru[...] = jnp.zeros(h_lru.shape, F32)

    n = 2 * RWKV_CHUNK
    r_i = lax.broadcasted_iota(jnp.int32, (n, n), 0)
    c_i = lax.broadcasted_iota(jnp.int32, (n, n), 1)
    same = (c_i >= RWKV_CHUNK) == (r_i >= RWKV_CHUNK)
    bd_incl = ((c_i & (RWKV_CHUNK - 1)) <= (r_i & (RWKV_CHUNK - 1))) & same
    bd_strict = ((c_i & (RWKV_CHUNK - 1)) < (r_i & (RWKV_CHUNK - 1))) & same
    lane = lax.broadcasted_iota(jnp.int32, (1, LANES), 1)
    m0 = jnp.where(lane < RWKV_N, 1.0, 0.0)
    m1 = 1.0 - m0

    def stack(z):
        return jnp.concatenate([z * m0, z * m1], axis=0)

    def body(c, carry):
        r0 = pl.multiple_of(c * RWKV_CHUNK, RWKV_CHUNK)
        rows = pl.ds(r0, RWKV_CHUNK)
        probs = [(b, p) for b in range(nb) for p in range(RWKV_HEADS // 2)]
        pls = [slice(p * LANES, (p + 1) * LANES) for _, p in probs]
        np_ = len(probs)
        ld = lambda ref: [ref[b, rows, pls[i]] for i, (b, _) in enumerate(probs)]
        rt, kt, bt, kh, vv = ld(rt_ref), ld(kt_ref), ld(bt_ref), ld(kh_ref), ld(v_ref)
        lhs = [jnp.concatenate([stack(kh[i]), stack(rt[i])], axis=0) for i in range(np_)]
        mb = [_mm_nt(lhs[i], jnp.concatenate([bt[i], bt[i]], axis=0)) for i in range(np_)]
        mk = [_mm_nt(lhs[i], jnp.concatenate([kt[i], kt[i]], axis=0)) for i in range(np_)]
        m_rb = [jnp.where(bd_incl, mb[i][n:], 0.0) for i in range(np_)]
        m_bk = [jnp.where(bd_strict, mk[i][0:n], 0.0) for i in range(np_)]
        m_rk = [jnp.where(bd_incl, mk[i][n:], 0.0) for i in range(np_)]
        t = _inv_unit_lower_many([jnp.where(bd_strict, mb[i][0:n], 0.0) for i in range(np_)], RWKV_CHUNK)
        vs = [stack(z) for z in vv]
        ks = [stack(z) for z in kt]
        bs = [stack(z) for z in bt]
        x = [_mm(m_bk[i], vs[i]) for i in range(np_)]
        sol = [_mm(t[i], jnp.concatenate([lhs[i][0:n], x[i]], axis=1)) for i in range(np_)]
        corr = [_mm(m_rb[i], sol[i]) for i in range(np_)]
        y_loc = [_mm(m_rk[i], vs[i]) - corr[i][:, LANES:] for i in range(np_)]
        r_eff = [lhs[i][n:] - corr[i][:, 0:LANES] for i in range(np_)]
        vtk = [_mm(vs[i].T, ks[i]) for i in range(np_)]
        kkt_t = [sol[i][:, 0:LANES].T for i in range(np_)]
        uv_t = [sol[i][:, LANES:].T for i in range(np_)]
        st = [s_rwkv[b, p] for b, p in probs]
        ut = [_mm(st[i], kkt_t[i]) + uv_t[i] for i in range(np_)]
        ys = [_mm_nt(r_eff[i], st[i]) + y_loc[i] for i in range(np_)]
        upd = [_mm(ut[i], bs[i]) for i in range(np_)]
        for i, (b, p) in enumerate(probs):
            y_scr[pl.ds(b * tc + r0, RWKV_CHUNK), pls[i]] = ys[i][0:RWKV_CHUNK] + ys[i][RWKV_CHUNK:]
            s_rwkv[b, p] = (st[i] + vtk[i] - upd[i]) * wc_ref[b, c][:, pls[i]]
        return carry

    lax.fori_loop(0, tc // RWKV_CHUNK, body, 0)

    hls = []
    for b in range(nb):
        a = la_ref[b]
        bb = lb_ref[b]
        rows_i = lax.broadcasted_iota(jnp.int32, a.shape, 0)
        dd = 1
        while dd < tc:
            keep = rows_i >= dd
            bb = jnp.where(keep, a * pltpu.roll(bb, dd, 0) + bb, bb)
            a = jnp.where(keep, a * pltpu.roll(a, dd, 0), a)
            dd *= 2
        hl = bb + a * h_lru[b, SUBLANES - 1:SUBLANES, :]
        h_lru[b] = hl[tc - SUBLANES:tc, :]
        hls.append(hl)
    y_lru = jnp.concatenate(hls, axis=0) * gly_ref[...].reshape(nb * tc, LRU_WIDTH)

    ones_bd = ones_ref[...]
    y = y_scr[...]
    mu = _group_sum(y, ones_bd) * (1.0 / RWKV_N)
    yc = y - mu
    var = _group_sum(yc * yc, ones_bd) * (1.0 / RWKV_N)
    yn = yc * lax.rsqrt(var + RWKV_GN_EPS) * lnw_ref[...] + lnb_ref[...]
    y_rwkv = (yn + bonus_ref[...].reshape(nb * tc, RWKV_W)) * g_ref[...].reshape(nb * tc, RWKV_W)
    out = (jnp.dot(y_rwkv.astype(BF16), wo_ref[0:RWKV_W, :], preferred_element_type=F32)
           + jnp.dot(y_lru.astype(BF16), wo_ref[RWKV_W:, :], preferred_element_type=F32))
    o_ref[...] = x_ref[...] + _rms(out, nw_ref[...]).reshape(nb, tc, d)


def _odd_chunk(x2, batch, seq, prep, w_out, ln_w, ln_b, ones_bd, nw):
    m, d = x2.shape
    tc = min(TILE_CHUNK, seq)
    rt, kt, bt, kh, v, wc, g, bonus, la, lb, gly = [a.reshape((batch, a.shape[0] // batch) + a.shape[1:])
                                                     for a in prep]

    def tok(n):
        return pl.BlockSpec((batch, tc, n), lambda j: (0, j, 0))

    vec = _const_spec((1, RWKV_W))
    out = pl.pallas_call(
        _odd_chunk_kernel,
        grid=(seq // tc,),
        in_specs=[
            tok(d), tok(RWKV_W), tok(RWKV_W), tok(RWKV_W), tok(RWKV_W), tok(RWKV_W),
            pl.BlockSpec((batch, tc // RWKV_CHUNK, 1, RWKV_W), lambda j: (0, j, 0, 0)),
            tok(RWKV_W), tok(RWKV_W), tok(LRU_WIDTH), tok(LRU_WIDTH), tok(LRU_WIDTH),
            _const_spec((ODD_OUT, d)), vec, vec, _const_spec((RWKV_W, RWKV_W)), _const_spec((1, d)),
        ],
        out_specs=tok(d),
        out_shape=jax.ShapeDtypeStruct((batch, seq, d), F32),
        scratch_shapes=[
            pltpu.VMEM((batch, RWKV_HEADS // 2, LANES, LANES), F32),
            pltpu.VMEM((batch, SUBLANES, LRU_WIDTH), F32),
            pltpu.VMEM((batch * tc, RWKV_W), F32),
        ],
        compiler_params=pltpu.CompilerParams(dimension_semantics=("arbitrary",),
                                             vmem_limit_bytes=VMEM_LIMIT),
        name="odd_chunk",
    )(x2.reshape(batch, seq, d), rt, kt, bt, kh, v, wc, g, bonus, la, lb, gly, w_out.astype(BF16), _row(ln_w),
      _row(ln_b), ones_bd, _row(nw))
    return out.reshape(m, d)


def _even_layer(x2, batch, seq, nw, w_in, w_out, lora_w2, lora_b, gla_norm, conv_w, a_log, dt_bias, gdn_norm):
    prep = _even_prep(x2, seq, nw[2], w_in, lora_w2, lora_b, conv_w, a_log, dt_bias)
    return _even_chunk(x2, batch, seq, prep, w_out, gla_norm, gdn_norm, nw[3])


def _odd_layer(x2, batch, seq, nw, w_in, w_out, mu, w0, w2, a0, a2, g2, k_k, k_a, r_k, ln_w, ln_b,
               conv_w, conv_b, wa, ba, wx, bx, lam):
    grp = jnp.arange(RWKV_W) // RWKV_N
    ones_bd = (grp[:, None] == grp[None, :]).astype(BF16)
    prep = _odd_prep(x2, seq, nw[2], w_in, mu, w0, w2, a0, a2, g2, k_k, k_a, r_k, ones_bd,
                     conv_w, conv_b, wa, ba, wx, bx, lam)
    return _odd_chunk(x2, batch, seq, prep, w_out, ln_w, ln_b, ones_bd, nw[3])


def kernel(x, norm_w, ffn_w_gate, ffn_w_up, ffn_w_down, even_w_in, even_w_out, gla_lora_w2, gla_lora_b, gla_norm, gdn_conv, gdn_a_log, gdn_dt_bias, gdn_norm, odd_w_in, odd_w_out, rwkv_mu, rwkv_w0, rwkv_w2, rwkv_a0, rwkv_a2, rwkv_g2, rwkv_k_k, rwkv_k_a, rwkv_r_k, rwkv_ln_w, rwkv_ln_b, lru_conv_w, lru_conv_b, lru_wa, lru_ba, lru_wx, lru_bx, lru_lambda):
    batch, seq, d = x.shape
    depth = norm_w.shape[0]
    x2 = x.reshape(batch * seq, d)
    for i in range(depth):
        j = i // 2
        nw = norm_w[i]
        x2 = _ffn(x2, nw[0], nw[1], ffn_w_gate[i, 0], ffn_w_up[i, 0], ffn_w_down[i, 0])
        if i % 2 == 0:
            x2 = _even_layer(x2, batch, seq, nw, even_w_in[j], even_w_out[j], gla_lora_w2[j], gla_lora_b[j],
                             gla_norm[j], gdn_conv[j], gdn_a_log[j], gdn_dt_bias[j], gdn_norm[j])
        else:
            x2 = _odd_layer(x2, batch, seq, nw, odd_w_in[j], odd_w_out[j], rwkv_mu[j], rwkv_w0[j], rwkv_w2[j],
                            rwkv_a0[j], rwkv_a2[j], rwkv_g2[j], rwkv_k_k[j], rwkv_k_a[j],
                            rwkv_r_k[j].reshape(-1), rwkv_ln_w[j], rwkv_ln_b[j], lru_conv_w[j], lru_conv_b[j],
                            lru_wa[j], lru_ba[j], lru_wx[j], lru_bx[j], lru_lambda[j])
        x2 = _ffn(x2, nw[4], nw[5], ffn_w_gate[i, 1], ffn_w_up[i, 1], ffn_w_down[i, 1])
    return x2.reshape(batch, seq, d)
```

```python
import functools

import jax
import jax.numpy as jnp
from jax import lax
from jax.experimental import pallas as pl
from jax.experimental.pallas import tpu as pltpu

F32 = jnp.float32
BF16 = jnp.bfloat16

D_MODEL = 1024
D_FF = 2816
FFN_RES = 0.5
NORM_EPS = 1e-6
CONV_W = 4

GLA_HEADS = 4
GLA_DK = 64
GLA_DV = 128
GLA_LORA = 16
GLA_GATE_NORM = 16.0
GLA_QK = GLA_HEADS * GLA_DK
GLA_V = GLA_HEADS * GLA_DV
GLA_CHUNK = 64

GDN_HEADS = 4
GDN_DK = 128
GDN_DV = 128
GDN_QK = GDN_HEADS * GDN_DK
GDN_V = GDN_HEADS * GDN_DV
GDN_CONV_CH = 2 * GDN_QK + GDN_V
GDN_CHUNK = 128

RWKV_HEADS = 8
RWKV_N = 64
RWKV_W = RWKV_HEADS * RWKV_N
RWKV_W_LORA = 64
RWKV_A_LORA = 64
RWKV_G_LORA = 128
RWKV_GN_EPS = 64e-5
RWKV_SHIFT = 3 * RWKV_W + RWKV_W_LORA + RWKV_A_LORA + RWKV_G_LORA
RWKV_CHUNK = 64

LRU_WIDTH = 512
LRU_BLOCKS = 8
LRU_BW = LRU_WIDTH // LRU_BLOCKS
LRU_C = 8.0

EVEN_OUT = GLA_V + GDN_V
ODD_OUT = RWKV_W + LRU_WIDTH

LANES = 128
SUBLANES = 8
SMALL_GLR = 0
SMALL_DA = GLA_LORA
SMALL_DB = GLA_LORA + GDN_HEADS

TILE_FFN = 512
TILE_PREP = 512
TILE_CHUNK = 128
FF_CHUNK = 2816
VMEM_LIMIT = 56 * 1024 * 1024


def _mm(a, b):
    return jnp.dot(a.astype(BF16), b.astype(BF16), preferred_element_type=F32)


def _mm_nt(a, b):
    return lax.dot_general(a.astype(BF16), b.astype(BF16), (((1,), (1,)), ((), ())),
                           preferred_element_type=F32)


def _rms(x, w):
    return x * lax.rsqrt(jnp.mean(x * x, axis=-1, keepdims=True) + NORM_EPS) * w


def _sigmoid(x):
    return 0.5 * jnp.tanh(0.5 * x) + 0.5


def _silu(x):
    return x * _sigmoid(x)


def _softplus(x):
    return jnp.maximum(x, 0.0) + jnp.log(1.0 + jnp.exp(-jnp.abs(x)))


def _gelu_tanh(x):
    c = 0.7978845608028654
    return 0.5 * x * (1.0 + jnp.tanh(c * (x + 0.044715 * (x * x * x))))


def _seg_cumsum(x, seg):
    rows = lax.broadcasted_iota(jnp.int32, x.shape, 0) & (seg - 1)
    d = 1
    while d < seg:
        x = x + jnp.where(rows >= d, pltpu.roll(x, d, 0), 0.0)
        d *= 2
    return x


def _shift_rows(x, carry, s):
    sh = pltpu.roll(x, s, 0)
    c = pltpu.roll(carry, s, 0)
    rows = lax.broadcasted_iota(jnp.int32, c.shape, 0)
    head = jnp.where(rows < s, c, sh[0:SUBLANES])
    return jnp.concatenate([head, sh[SUBLANES:]], axis=0)


def _group_sum(x, ones_bd):
    hi = x.astype(BF16)
    lo = (x - hi.astype(F32)).astype(BF16)
    dot = functools.partial(jnp.dot, preferred_element_type=F32)
    return dot(hi, ones_bd) + dot(lo, ones_bd)


def _split(a):
    hi = a.astype(BF16)
    return hi, (a - hi.astype(F32)).astype(BF16)


def _mm3(a, b, nt=False):
    dims = (((1,), (1 if nt else 0,)), ((), ()))
    dot = functools.partial(lax.dot_general, dimension_numbers=dims, preferred_element_type=F32)
    ah, al = _split(a)
    bh, bl = _split(b)
    a2 = jnp.concatenate([ah, al], axis=1)
    ax = 1 if nt else 0
    return dot(a2, jnp.concatenate([bh, bh], axis=ax)) + dot(a2, jnp.concatenate([bl, bl], axis=ax))


def _inv_unit_lower(low, n, mm=_mm):
    r = lax.broadcasted_iota(jnp.int32, low.shape, 0)
    c = lax.broadcasted_iota(jnp.int32, low.shape, 1)
    neg = -low
    t = jnp.where(r == c, 1.0, 0.0) + neg
    p = neg
    k = 2
    while k < n:
        p = mm(p, p)
        t = t + mm(p, t)
        k *= 2
    return t


def _inv_unit_lower_many(lows, n, mm=_mm):
    r = lax.broadcasted_iota(jnp.int32, lows[0].shape, 0)
    c = lax.broadcasted_iota(jnp.int32, lows[0].shape, 1)
    eye = jnp.where(r == c, 1.0, 0.0)
    negs = [-low for low in lows]
    ts = [eye + p for p in negs]
    ps = [mm(p, p) for p in negs]
    m = lows[0].shape[1]
    k = 4
    while k < n:
        pq = [mm(p, jnp.concatenate([p, t], axis=1)) for p, t in zip(ps, ts)]
        ps = [z[:, 0:m] for z in pq]
        ts = [t + z[:, m:] for t, z in zip(ts, pq)]
        k *= 2
    return [t + mm(p, t) for p, t in zip(ps, ts)]


def _const_spec(shape):
    nd = len(shape)
    return pl.BlockSpec(shape, lambda *_: (0,) * nd, pipeline_mode=pl.Buffered(1))


def _row(v):
    return v.reshape(1, -1).astype(F32)


def _ffn_kernel(x_ref, nw_ref, wg_ref, wu_ref, wd_ref, o_ref):
    x = x_ref[...]
    h = _rms(x, nw_ref[0:1, :]).astype(BF16)
    acc = jnp.zeros(x.shape, F32)
    for c in range(D_FF // FF_CHUNK):
        sl = slice(c * FF_CHUNK, (c + 1) * FF_CHUNK)
        g = jnp.dot(h, wg_ref[:, sl], preferred_element_type=F32)
        u = jnp.dot(h, wu_ref[:, sl], preferred_element_type=F32)
        a = (_silu(g) * u).astype(BF16)
        acc = acc + jnp.dot(a, wd_ref[sl, :], preferred_element_type=F32)
    o_ref[...] = x + FFN_RES * _rms(acc, nw_ref[1:2, :])


def _ffn(x2, nw_pre, nw_post, wg, wu, wd):
    m, d = x2.shape
    tm = min(TILE_FFN, m)
    nw = jnp.stack([nw_pre, nw_post]).astype(F32)
    return pl.pallas_call(
        _ffn_kernel,
        grid=(m // tm,),
        in_specs=[
            pl.BlockSpec((tm, d), lambda i: (i, 0)),
            _const_spec((2, d)),
            _const_spec((d, D_FF)),
            _const_spec((d, D_FF)),
            _const_spec((D_FF, d)),
        ],
        out_specs=pl.BlockSpec((tm, d), lambda i: (i, 0)),
        out_shape=jax.ShapeDtypeStruct((m, d), F32),
        compiler_params=pltpu.CompilerParams(dimension_semantics=("arbitrary",),
                                             vmem_limit_bytes=VMEM_LIMIT),
        name="ffn",
    )(x2, nw, wg.astype(BF16), wu.astype(BF16), wd.astype(BF16))


def _even_prep_kernel(tiles_per_seq,
                      x_ref, nw_ref, wa_ref, ws_ref, wc_ref, wz_ref, lw2_ref, lb_ref, conv_ref, hp_ref,
                      qe_ref, kt_ref, kd_ref, al_ref, gv_ref, sgg_ref, dq_ref, dk_ref, dv_ref, gb_ref,
                      sdz_ref, carry_ref):
    first = (pl.program_id(0) % tiles_per_seq) == 0
    tm = x_ref.shape[0]
    h = _rms(x_ref[...], nw_ref[...]).astype(BF16)

    pa = jnp.dot(h, wa_ref[...], preferred_element_type=F32)
    ps = jnp.dot(h, ws_ref[...], preferred_element_type=F32)
    z = _mm(ps, lw2_ref[...]) + lb_ref[...]
    log_a = -_softplus(-z) * (1.0 / GLA_GATE_NORM)
    b = _seg_cumsum(log_a, GLA_CHUNK)
    gk = pa[:, GLA_QK:2 * GLA_QK]
    qe_ref[...] = pa[:, 0:GLA_QK] * (GLA_DK ** -0.5) * jnp.exp(b)
    kt_ref[...] = gk * jnp.exp(-b)
    for c in range(tm // GLA_CHUNK):
        cs = slice(c * GLA_CHUNK, (c + 1) * GLA_CHUNK)
        b_last = b[(c + 1) * GLA_CHUNK - 1:(c + 1) * GLA_CHUNK, :]
        al_ref[c] = jnp.exp(b_last)
        kd_ref[cs, :] = gk[cs] * jnp.exp(b_last - b[cs])
    gv_ref[...] = pa[:, 2 * GLA_QK:2 * GLA_QK + GLA_V]
    sgg_ref[...] = _silu(pa[:, 2 * GLA_QK + GLA_V:])

    pc = jnp.dot(h, wc_ref[...], preferred_element_type=F32)
    carry = jnp.where(first, 0.0, carry_ref[...])
    conv = pc * conv_ref[CONV_W - 1:CONV_W, :]
    for s in range(1, CONV_W):
        conv = conv + _shift_rows(pc, carry, s) * conv_ref[CONV_W - 1 - s:CONV_W - s, :]
    carry_ref[...] = pc[tm - SUBLANES:tm, :]
    c = _silu(conv)
    for hd in range(GDN_HEADS):
        lo, hi = hd * GDN_DK, (hd + 1) * GDN_DK
        cq = c[:, lo:hi]
        ck = c[:, GDN_QK + lo:GDN_QK + hi]
        dq_ref[:, lo:hi] = cq * lax.rsqrt(jnp.sum(cq * cq, -1, keepdims=True) + NORM_EPS) * (GDN_DK ** -0.5)
        dk_ref[:, lo:hi] = ck * lax.rsqrt(jnp.sum(ck * ck, -1, keepdims=True) + NORM_EPS)
    dv_ref[...] = c[:, 2 * GDN_QK:]

    g = -jnp.exp(hp_ref[0:1, :]) * _softplus(ps + hp_ref[1:2, :])
    gc = _seg_cumsum(g, GDN_CHUNK)
    lane = lax.broadcasted_iota(jnp.int32, ps.shape, 1)
    is_g = (lane >= SMALL_DA) & (lane < SMALL_DB)
    gb_ref[...] = jnp.where(is_g, gc, _sigmoid(ps))

    sdz_ref[...] = _silu(jnp.dot(h, wz_ref[...], preferred_element_type=F32))


def _even_prep(x2, seq, nw, w_in, lora_w2, lora_b, conv_w, a_log, dt_bias):
    m, d = x2.shape
    tm = min(TILE_PREP, seq)
    o = 0
    w_a = w_in[:, 0:2 * GLA_QK + 2 * GLA_V]
    o = 2 * GLA_QK + 2 * GLA_V
    w_glr = w_in[:, o:o + GLA_LORA]
    o += GLA_LORA
    w_c = w_in[:, o:o + GDN_CONV_CH]
    o += GDN_CONV_CH
    w_z = w_in[:, o:o + GDN_V]
    o += GDN_V
    w_da = w_in[:, o:o + GDN_HEADS]
    w_db = w_in[:, o + GDN_HEADS:o + 2 * GDN_HEADS]
    n_small = GLA_LORA + 2 * GDN_HEADS
    w_s = jnp.concatenate([w_glr, w_da, w_db, jnp.zeros((d, LANES - n_small), F32)], axis=1)
    lw2 = jnp.concatenate([lora_w2, jnp.zeros((LANES - GLA_LORA, GLA_QK), F32)], axis=0)
    pad_l = jnp.zeros((SMALL_DA,), F32)
    pad_r = jnp.zeros((LANES - SMALL_DB,), F32)
    hp = jnp.stack([jnp.concatenate([pad_l, a_log.astype(F32), pad_r]),
                    jnp.concatenate([pad_l, dt_bias.astype(F32), pad_r])])

    def tok(n):
        return pl.BlockSpec((tm, n), lambda i: (i, 0))

    outs = [
        (GLA_QK, tok(GLA_QK)), (GLA_QK, tok(GLA_QK)), (GLA_QK, tok(GLA_QK)),
        (None, pl.BlockSpec((tm // GLA_CHUNK, 1, GLA_QK), lambda i: (i, 0, 0))),
        (GLA_V, tok(GLA_V)), (GLA_V, tok(GLA_V)),
        (GDN_QK, tok(GDN_QK)), (GDN_QK, tok(GDN_QK)), (GDN_V, tok(GDN_V)),
        (LANES, tok(LANES)), (GDN_V, tok(GDN_V)),
    ]
    out_shape = [jax.ShapeDtypeStruct((m, n), F32) if n is not None
                 else jax.ShapeDtypeStruct((m // GLA_CHUNK, 1, GLA_QK), F32) for n, _ in outs]
    return pl.pallas_call(
        functools.partial(_even_prep_kernel, seq // tm),
        grid=(m // tm,),
        in_specs=[
            tok(d), _const_spec((1, d)),
            _const_spec(w_a.shape), _const_spec(w_s.shape), _const_spec(w_c.shape), _const_spec(w_z.shape),
            _const_spec(lw2.shape), _const_spec((1, GLA_QK)), _const_spec(conv_w.shape), _const_spec(hp.shape),
        ],
        out_specs=[s for _, s in outs],
        out_shape=out_shape,
        scratch_shapes=[pltpu.VMEM((SUBLANES, GDN_CONV_CH), F32)],
        compiler_params=pltpu.CompilerParams(dimension_semantics=("arbitrary",),
                                             vmem_limit_bytes=VMEM_LIMIT),
        name="even_prep",
    )(x2, _row(nw), w_a.astype(BF16), w_s.astype(BF16), w_c.astype(BF16), w_z.astype(BF16),
      lw2.astype(BF16), _row(lora_b), conv_w.astype(F32), hp)


def _even_chunk_kernel(x_ref, qe_ref, kt_ref, kd_ref, al_ref, gv_ref, sgg_ref, dq_ref, dk_ref, dv_ref,
                       gb_ref, sdz_ref, wo_ref, gn_ref, dn_ref, nw_ref,
                       o_ref, s_gla, s_gdn, o_scr):
    nb, tc, d = x_ref.shape

    @pl.when(pl.program_id(0) == 0)
    def _():
        s_gla[...] = jnp.zeros(s_gla.shape, F32)
        s_gdn[...] = jnp.zeros(s_gdn.shape, F32)

    n = GDN_CHUNK
    r_i = lax.broadcasted_iota(jnp.int32, (n, n), 0)
    c_i = lax.broadcasted_iota(jnp.int32, (n, n), 1)
    tril = c_i <= r_i
    strict = c_i < r_i
    bd_incl = ((c_i & (GLA_CHUNK - 1)) <= (r_i & (GLA_CHUNK - 1))) & ((c_i >= GLA_CHUNK) == (r_i >= GLA_CHUNK))
    lane = lax.broadcasted_iota(jnp.int32, (1, LANES), 1)
    m0 = jnp.where(lane < GLA_DK, 1.0, 0.0)
    m1 = 1.0 - m0

    def body(s, carry):
        r0 = pl.multiple_of(s * n, n)
        rows = pl.ds(r0, n)
        probs = [(b, hd) for b in range(nb) for hd in range(GDN_HEADS)]
        np_ = len(probs)
        hls = [slice(hd * GDN_DK, (hd + 1) * GDN_DK) for _, hd in probs]
        ld = lambda ref: [ref[b, rows, hls[i]] for i, (b, _) in enumerate(probs)]
        q, k, v = ld(dq_ref), ld(dk_ref), ld(dv_ref)
        gbt = [gb_ref[b, rows, :] for b in range(nb)]
        gc = [jnp.broadcast_to(gbt[b][:, SMALL_DA + hd:SMALL_DA + hd + 1], (n, n)) for b, hd in probs]
        beta = [jnp.broadcast_to(gbt[b][:, SMALL_DB + hd:SMALL_DB + hd + 1], (n, n)) for b, hd in probs]
        decay = [jnp.where(tril, jnp.exp(jnp.where(tril, g - g.T, 0.0)), 0.0) for g in gc]
        kb = [k[i] * beta[i] for i in range(np_)]
        low = [jnp.where(strict, _mm3(kb[i], k[i], nt=True) * decay[i], 0.0) for i in range(np_)]
        att = [_mm_nt(q[i], k[i]) * decay[i] for i in range(np_)]
        t = _inv_unit_lower_many(low, n, mm=_mm3)
        eg = [jnp.exp(g) for g in gc]
        sol = [_mm3(t[i], jnp.concatenate([v[i] * beta[i], kb[i] * eg[i]], axis=1)) for i in range(np_)]
        g_last = [g[n - 1:n, :] for g in gc]
        k_dec_t = [(k[i] * jnp.exp(g_last[i] - gc[i])).T for i in range(np_)]
        st = [s_gdn[b, hd] for b, hd in probs]
        v_new = [sol[i][:, 0:GDN_DV] - _mm3(sol[i][:, GDN_DV:], st[i]) for i in range(np_)]
        o = [_mm(q[i] * eg[i], st[i]) + _mm(att[i], v_new[i]) for i in range(np_)]
        upd = [_mm3(k_dec_t[i], v_new[i]) for i in range(np_)]
        for i, (b, hd) in enumerate(probs):
            o_scr[pl.ds(b * tc + r0, n), GLA_V + hd * GDN_DV:GLA_V + (hd + 1) * GDN_DV] = o[i]
            s_gdn[b, hd] = st[i] * jnp.exp(g_last[i]) + upd[i]

        gprobs = [(b, p) for b in range(nb) for p in range(GLA_HEADS // 2)]
        ng = len(gprobs)
        pls = [slice(p * LANES, (p + 1) * LANES) for _, p in gprobs]
        for cc in range(n // GLA_CHUNK):
            crow = pl.ds(r0 + cc * GLA_CHUNK, GLA_CHUNK)
            ldg = lambda ref: [ref[b, crow, pls[i]] for i, (b, _) in enumerate(gprobs)]
            qe, kt, kd = ldg(qe_ref), ldg(kt_ref), ldg(kd_ref)
            vs = [jnp.concatenate([gv_ref[b, crow, (2 * p) * GLA_DV:(2 * p + 1) * GLA_DV],
                                   gv_ref[b, crow, (2 * p + 1) * GLA_DV:(2 * p + 2) * GLA_DV]], axis=0)
                  for b, p in gprobs]
            qs = [jnp.concatenate([z * m0, z * m1], axis=0) for z in qe]
            ks = [jnp.concatenate([z * m0, z * m1], axis=0) for z in kd]
            att = [jnp.where(bd_incl, _mm_nt(qs[i], jnp.concatenate([kt[i], kt[i]], axis=0)), 0.0)
                   for i in range(ng)]
            sg = [s_gla[b, p] for b, p in gprobs]
            og = [_mm(att[i], vs[i]) + _mm_nt(qs[i], sg[i]) for i in range(ng)]
            ug = [_mm(vs[i].T, ks[i]) for i in range(ng)]
            for i, (b, p) in enumerate(gprobs):
                ocrow = pl.ds(b * tc + r0 + cc * GLA_CHUNK, GLA_CHUNK)
                al = al_ref[b, s * (n // GLA_CHUNK) + cc][:, pls[i]]
                s_gla[b, p] = sg[i] * al + ug[i]
                o_scr[ocrow, (2 * p) * GLA_DV:(2 * p + 1) * GLA_DV] = og[i][0:GLA_CHUNK]
                o_scr[ocrow, (2 * p + 1) * GLA_DV:(2 * p + 2) * GLA_DV] = og[i][GLA_CHUNK:]
        return carry

    lax.fori_loop(0, tc // n, body, 0)

    sgg = sgg_ref[...].reshape(nb * tc, GLA_V)
    sdz = sdz_ref[...].reshape(nb * tc, GDN_V)
    for hd in range(GLA_HEADS + GDN_HEADS):
        hl = slice(hd * LANES, (hd + 1) * LANES)
        oh = o_scr[:, hl]
        nw = gn_ref[...] if hd < GLA_HEADS else dn_ref[...]
        gate = sgg[:, hl] if hd < GLA_HEADS else sdz[:, (hd - GLA_HEADS) * LANES:(hd - GLA_HEADS + 1) * LANES]
        o_scr[:, hl] = oh * lax.rsqrt(jnp.mean(oh * oh, -1, keepdims=True) + NORM_EPS) * nw * gate
    y = jnp.dot(o_scr[...].astype(BF16), wo_ref[...], preferred_element_type=F32)
    o_ref[...] = x_ref[...] + _rms(y, nw_ref[...]).reshape(nb, tc, d)


def _even_chunk(x2, batch, seq, prep, w_out, gla_norm, gdn_norm, nw):
    m, d = x2.shape
    tc = min(TILE_CHUNK, seq)
    qe, kt, kd, al, gv, sgg, dq, dk, dv, gb, sdz = [a.reshape((batch, a.shape[0] // batch) + a.shape[1:])
                                                     for a in prep]

    def tok(n):
        return pl.BlockSpec((batch, tc, n), lambda j: (0, j, 0))

    out = pl.pallas_call(
        _even_chunk_kernel,
        grid=(seq // tc,),
        in_specs=[
            tok(d), tok(GLA_QK), tok(GLA_QK), tok(GLA_QK),
            pl.BlockSpec((batch, tc // GLA_CHUNK, 1, GLA_QK), lambda j: (0, j, 0, 0)),
            tok(GLA_V), tok(GLA_V), tok(GDN_QK), tok(GDN_QK), tok(GDN_V), tok(LANES), tok(GDN_V),
            _const_spec((EVEN_OUT, d)), _const_spec((1, GLA_DV)), _const_spec((1, GDN_DV)), _const_spec((1, d)),
        ],
        out_specs=tok(d),
        out_shape=jax.ShapeDtypeStruct((batch, seq, d), F32),
        scratch_shapes=[
            pltpu.VMEM((batch, GLA_HEADS // 2, GLA_DV, LANES), F32),
            pltpu.VMEM((batch, GDN_HEADS, GDN_DK, GDN_DV), F32),
            pltpu.VMEM((batch * tc, EVEN_OUT), F32),
        ],
        compiler_params=pltpu.CompilerParams(dimension_semantics=("arbitrary",),
                                             vmem_limit_bytes=VMEM_LIMIT),
        name="even_chunk",
    )(x2.reshape(batch, seq, d), qe, kt, kd, al, gv, sgg, dq, dk, dv, gb, sdz, w_out.astype(BF16),
      _row(gla_norm), _row(gdn_norm), _row(nw))
    return out.reshape(m, d)


def _odd_prep_kernel(tiles_per_seq,
                     x_ref, nw_ref, wp_ref, wl_ref, mu_ref, w0_ref, w2_ref, a0_ref, a2_ref, g2_ref,
                     kk_ref, ka_ref, rk_ref, ones_ref, cw_ref, cb_ref, wa_ref, ba_ref, wx_ref, bx_ref, lam_ref,
                     rt_ref, kt_ref, bt_ref, kh_ref, v_ref, wc_ref, g_ref, bonus_ref, la_ref, lb_ref, gly_ref,
                     carry_p, carry_l):
    first = (pl.program_id(0) % tiles_per_seq) == 0
    tm = x_ref.shape[0]
    h = _rms(x_ref[...], nw_ref[...]).astype(BF16)

    ps = jnp.dot(h, wp_ref[...], preferred_element_type=F32)
    carry = jnp.where(first, 0.0, carry_p[...])
    prev = _shift_rows(ps, carry, 1)
    carry_p[...] = ps[tm - SUBLANES:tm, :]
    ps = ps + (prev - ps) * mu_ref[...]
    r = ps[:, 0:RWKV_W]
    k = ps[:, RWKV_W:2 * RWKV_W]
    v = ps[:, 2 * RWKV_W:3 * RWKV_W]
    lora = ps[:, 3 * RWKV_W:3 * RWKV_W + LANES]
    gl = ps[:, 3 * RWKV_W + LANES:]
    w = -_softplus(-(w0_ref[...] + _mm(jnp.tanh(lora), w2_ref[...]))) - 0.5
    logw = -jnp.exp(w)
    a = _sigmoid(a0_ref[...] + _mm(lora, a2_ref[...]))
    g_ref[...] = _mm(_sigmoid(gl), g2_ref[...])
    ones_bd = ones_ref[...]
    kk = k * kk_ref[...]
    kk = kk * lax.rsqrt(_group_sum(kk * kk, ones_bd) + NORM_EPS)
    k = k * (1.0 + (a - 1.0) * ka_ref[...])
    bonus_ref[...] = _group_sum(r * k * rk_ref[...], ones_bd) * v
    gcum = _seg_cumsum(logw, RWKV_CHUNK)
    for c in range(tm // RWKV_CHUNK):
        wc_ref[c] = jnp.exp(gcum[(c + 1) * RWKV_CHUNK - 1:(c + 1) * RWKV_CHUNK, :])
    e_neg = jnp.exp(-gcum)
    rt_ref[...] = r * jnp.exp(gcum)
    kt_ref[...] = k * e_neg
    bt_ref[...] = kk * a * e_neg
    kh_ref[...] = kk * jnp.exp(gcum - logw)
    v_ref[...] = v

    pq = jnp.dot(h, wl_ref[...], preferred_element_type=F32)
    lx = pq[:, 0:LRU_WIDTH]
    carry2 = jnp.where(first, 0.0, carry_l[...])
    xb = lx * cw_ref[CONV_W - 1:CONV_W, :] + cb_ref[...]
    for s in range(1, CONV_W):
        xb = xb + _shift_rows(lx, carry2, s) * cw_ref[CONV_W - 1 - s:CONV_W - s, :]
    carry_l[...] = lx[tm - SUBLANES:tm, :]
    gate_r = _sigmoid(_mm(xb, wa_ref[...]) + ba_ref[...])
    gate_i = _sigmoid(_mm(xb, wx_ref[...]) + bx_ref[...])
    log_a = -LRU_C * gate_r * _softplus(-lam_ref[...])
    mult = jnp.sqrt(jnp.maximum(-jnp.tanh(log_a) * (jnp.exp(2.0 * log_a) + 1.0), 0.0))
    la_ref[...] = jnp.exp(log_a)
    lb_ref[...] = mult * gate_i * xb
    gly_ref[...] = _gelu_tanh(pq[:, LRU_WIDTH:])


def _block_diag(w):
    nb, n, _ = w.shape
    eye = jnp.eye(nb, dtype=w.dtype)
    return (eye[:, None, :, None] * w[:, :, None, :]).reshape(nb * n, nb * n)


def _odd_prep(x2, seq, nw, w_in, mu, w0, w2, a0, a2, g2, k_k, k_a, r_k, ones_bd,
              conv_w, conv_b, wa, ba, wx, bx, lam):
    m, d = x2.shape
    tm = min(TILE_PREP, seq)
    w_p = w_in[:, 0:RWKV_SHIFT]
    w_l = w_in[:, RWKV_SHIFT:]
    w2p = jnp.concatenate([w2, jnp.zeros((RWKV_A_LORA, RWKV_W), F32)], axis=0)
    a2p = jnp.concatenate([jnp.zeros((RWKV_W_LORA, RWKV_W), F32), a2], axis=0)

    def tok(n):
        return pl.BlockSpec((tm, n), lambda i: (i, 0))

    wspec = pl.BlockSpec((tm // RWKV_CHUNK, 1, RWKV_W), lambda i: (i, 0, 0))
    out_specs = [tok(RWKV_W)] * 5 + [wspec] + [tok(RWKV_W)] * 2 + [tok(LRU_WIDTH)] * 3
    out_shape = ([jax.ShapeDtypeStruct((m, RWKV_W), F32)] * 5
                 + [jax.ShapeDtypeStruct((m // RWKV_CHUNK, 1, RWKV_W), F32)]
                 + [jax.ShapeDtypeStruct((m, RWKV_W), F32)] * 2
                 + [jax.ShapeDtypeStruct((m, LRU_WIDTH), F32)] * 3)
    vec = _const_spec((1, RWKV_W))
    sq = _const_spec((RWKV_W, RWKV_W))
    return pl.pallas_call(
        functools.partial(_odd_prep_kernel, seq // tm),
        grid=(m // tm,),
        in_specs=[
            tok(d), _const_spec((1, d)), _const_spec(w_p.shape), _const_spec(w_l.shape),
            _const_spec((1, RWKV_SHIFT)), vec, _const_spec(w2p.shape), vec, _const_spec(a2p.shape),
            _const_spec(g2.shape), vec, vec, vec, sq,
            _const_spec(conv_w.shape), vec, sq, vec, sq, vec, vec,
        ],
        out_specs=out_specs,
        out_shape=out_shape,
        scratch_shapes=[pltpu.VMEM((SUBLANES, RWKV_SHIFT), F32), pltpu.VMEM((SUBLANES, LRU_WIDTH), F32)],
        compiler_params=pltpu.CompilerParams(dimension_semantics=("arbitrary",),
                                             vmem_limit_bytes=VMEM_LIMIT),
        name="odd_prep",
    )(x2, _row(nw), w_p.astype(BF16), w_l.astype(BF16), _row(mu), _row(w0), w2p.astype(BF16), _row(a0),
      a2p.astype(BF16), g2.astype(BF16), _row(k_k), _row(k_a), _row(r_k), ones_bd,
      conv_w.astype(F32), _row(conv_b), _block_diag(wa).astype(BF16), _row(ba),
      _block_diag(wx).astype(BF16), _row(bx), _row(lam))


def _odd_chunk_kernel(x_ref, rt_ref, kt_ref, bt_ref, kh_ref, v_ref, wc_ref, g_ref, bonus_ref, la_ref, lb_ref,
                      gly_ref, wo_ref, lnw_ref, lnb_ref, ones_ref, nw_ref,
                      o_ref, s_rwkv, h_lru, y_scr):
    nb, tc, d = x_ref.shape

    @pl.when(pl.program_id(0) == 0)
    def _():
        s_rwkv[...] = jnp.zeros(s_rwkv.shape, F32)
        h_lru[...] = jnp.zeros(h_lru.shape, F32)

    n = 2 * RWKV_CHUNK
    r_i = lax.broadcasted_iota(jnp.int32, (n, n), 0)
    c_i = lax.broadcasted_iota(jnp.int32, (n, n), 1)
    same = (c_i >= RWKV_CHUNK) == (r_i >= RWKV_CHUNK)
    bd_incl = ((c_i & (RWKV_CHUNK - 1)) <= (r_i & (RWKV_CHUNK - 1))) & same
    bd_strict = ((c_i & (RWKV_CHUNK - 1)) < (r_i & (RWKV_CHUNK - 1))) & same
    lane = lax.broadcasted_iota(jnp.int32, (1, LANES), 1)
    m0 = jnp.where(lane < RWKV_N, 1.0, 0.0)
    m1 = 1.0 - m0

    def stack(z):
        return jnp.concatenate([z * m0, z * m1], axis=0)

    def body(c, carry):
        r0 = pl.multiple_of(c * RWKV_CHUNK, RWKV_CHUNK)
        rows = pl.ds(r0, RWKV_CHUNK)
        probs = [(b, p) for b in range(nb) for p in range(RWKV_HEADS // 2)]
        pls = [slice(p * LANES, (p + 1) * LANES) for _, p in probs]
        np_ = len(probs)
        ld = lambda ref: [ref[b, rows, pls[i]] for i, (b, _) in enumerate(probs)]
        rt, kt, bt, kh, vv = ld(rt_ref), ld(kt_ref), ld(bt_ref), ld(kh_ref), ld(v_ref)
        lhs = [jnp.concatenate([stack(kh[i]), stack(rt[i])], axis=0) for i in range(np_)]
        mb = [_mm_nt(lhs[i], jnp.concatenate([bt[i], bt[i]], axis=0)) for i in range(np_)]
        mk = [_mm_nt(lhs[i], jnp.concatenate([kt[i], kt[i]], axis=0)) for i in range(np_)]
        m_rb = [jnp.where(bd_incl, mb[i][n:], 0.0) for i in range(np_)]
        m_bk = [jnp.where(bd_strict, mk[i][0:n], 0.0) for i in range(np_)]
        m_rk = [jnp.where(bd_incl, mk[i][n:], 0.0) for i in range(np_)]
        t = _inv_unit_lower_many([jnp.where(bd_strict, mb[i][0:n], 0.0) for i in range(np_)], RWKV_CHUNK)
        vs = [stack(z) for z in vv]
        ks = [stack(z) for z in kt]
        bs = [stack(z) for z in bt]
        x = [_mm(m_bk[i], vs[i]) for i in range(np_)]
        sol = [_mm(t[i], jnp.concatenate([lhs[i][0:n], x[i]], axis=1)) for i in range(np_)]
        corr = [_mm(m_rb[i], sol[i]) for i in range(np_)]
        y_loc = [_mm(m_rk[i], vs[i]) - corr[i][:, LANES:] for i in range(np_)]
        r_eff = [lhs[i][n:] - corr[i][:, 0:LANES] for i in range(np_)]
        vtk = [_mm(vs[i].T, ks[i]) for i in range(np_)]
        kkt_t = [sol[i][:, 0:LANES].T for i in range(np_)]
        uv_t = [sol[i][:, LANES:].T for i in range(np_)]
        st = [s_rwkv[b, p] for b, p in probs]
        ut = [_mm(st[i], kkt_t[i]) + uv_t[i] for i in range(np_)]
        ys = [_mm_nt(r_eff[i], st[i]) + y_loc[i] for i in range(np_)]
        upd = [_mm(ut[i], bs[i]) for i in range(np_)]
        for i, (b, p) in enumerate(probs):
            y_scr[pl.ds(b * tc + r0, RWKV_CHUNK), pls[i]] = ys[i][0:RWKV_CHUNK] + ys[i][RWKV_CHUNK:]
            s_rwkv[b, p] = (st[i] + vtk[i] - upd[i]) * wc_ref[b, c][:, pls[i]]
        return carry

    lax.fori_loop(0, tc // RWKV_CHUNK, body, 0)

    hls = []
    for b in range(nb):
        a = la_ref[b]
        bb = lb_ref[b]
        rows_i = lax.broadcasted_iota(jnp.int32, a.shape, 0)
        dd = 1
        while dd < tc:
            keep = rows_i >= dd
            bb = jnp.where(keep, a * pltpu.roll(bb, dd, 0) + bb, bb)
            a = jnp.where(keep, a * pltpu.roll(a, dd, 0), a)
            dd *= 2
        hl = bb + a * h_lru[b, SUBLANES - 1:SUBLANES, :]
        h_lru[b] = hl[tc - SUBLANES:tc, :]
        hls.append(hl)
    y_lru = jnp.concatenate(hls, axis=0) * gly_ref[...].reshape(nb * tc, LRU_WIDTH)

    ones_bd = ones_ref[...]
    y = y_scr[...]
    mu = _group_sum(y, ones_bd) * (1.0 / RWKV_N)
    yc = y - mu
    var = _group_sum(yc * yc, ones_bd) * (1.0 / RWKV_N)
    yn = yc * lax.rsqrt(var + RWKV_GN_EPS) * lnw_ref[...] + lnb_ref[...]
    y_rwkv = (yn + bonus_ref[...].reshape(nb * tc, RWKV_W)) * g_ref[...].reshape(nb * tc, RWKV_W)
    out = (jnp.dot(y_rwkv.astype(BF16), wo_ref[0:RWKV_W, :], preferred_element_type=F32)
           + jnp.dot(y_lru.astype(BF16), wo_ref[RWKV_W:, :], preferred_element_type=F32))
    o_ref[...] = x_ref[...] + _rms(out, nw_ref[...]).reshape(nb, tc, d)


def _odd_chunk(x2, batch, seq, prep, w_out, ln_w, ln_b, ones_bd, nw):
    m, d = x2.shape
    tc = min(TILE_CHUNK, seq)
    rt, kt, bt, kh, v, wc, g, bonus, la, lb, gly = [a.reshape((batch, a.shape[0] // batch) + a.shape[1:])
                                                     for a in prep]

    def tok(n):
        return pl.BlockSpec((batch, tc, n), lambda j: (0, j, 0))

    vec = _const_spec((1, RWKV_W))
    out = pl.pallas_call(
        _odd_chunk_kernel,
        grid=(seq // tc,),
        in_specs=[
            tok(d), tok(RWKV_W), tok(RWKV_W), tok(RWKV_W), tok(RWKV_W), tok(RWKV_W),
            pl.BlockSpec((batch, tc // RWKV_CHUNK, 1, RWKV_W), lambda j: (0, j, 0, 0)),
            tok(RWKV_W), tok(RWKV_W), tok(LRU_WIDTH), tok(LRU_WIDTH), tok(LRU_WIDTH),
            _const_spec((ODD_OUT, d)), vec, vec, _const_spec((RWKV_W, RWKV_W)), _const_spec((1, d)),
        ],
        out_specs=tok(d),
        out_shape=jax.ShapeDtypeStruct((batch, seq, d), F32),
        scratch_shapes=[
            pltpu.VMEM((batch, RWKV_HEADS // 2, LANES, LANES), F32),
            pltpu.VMEM((batch, SUBLANES, LRU_WIDTH), F32),
            pltpu.VMEM((batch * tc, RWKV_W), F32),
        ],
        compiler_params=pltpu.CompilerParams(dimension_semantics=("arbitrary",),
                                             vmem_limit_bytes=VMEM_LIMIT),
        name="odd_chunk",
    )(x2.reshape(batch, seq, d), rt, kt, bt, kh, v, wc, g, bonus, la, lb, gly, w_out.astype(BF16), _row(ln_w),
      _row(ln_b), ones_bd, _row(nw))
    return out.reshape(m, d)


def _even_layer(x2, batch, seq, nw, w_in, w_out, lora_w2, lora_b, gla_norm, conv_w, a_log, dt_bias, gdn_norm):
    prep = _even_prep(x2, seq, nw[2], w_in, lora_w2, lora_b, conv_w, a_log, dt_bias)
    return _even_chunk(x2, batch, seq, prep, w_out, gla_norm, gdn_norm, nw[3])


def _odd_layer(x2, batch, seq, nw, w_in, w_out, mu, w0, w2, a0, a2, g2, k_k, k_a, r_k, ln_w, ln_b,
               conv_w, conv_b, wa, ba, wx, bx, lam):
    grp = jnp.arange(RWKV_W) // RWKV_N
    ones_bd = (grp[:, None] == grp[None, :]).astype(BF16)
    prep = _odd_prep(x2, seq, nw[2], w_in, mu, w0, w2, a0, a2, g2, k_k, k_a, r_k, ones_bd,
                     conv_w, conv_b, wa, ba, wx, bx, lam)
    return _odd_chunk(x2, batch, seq, prep, w_out, ln_w, ln_b, ones_bd, nw[3])


def kernel(x, norm_w, ffn_w_gate, ffn_w_up, ffn_w_down, even_w_in, even_w_out, gla_lora_w2, gla_lora_b, gla_norm, gdn_conv, gdn_a_log, gdn_dt_bias, gdn_norm, odd_w_in, odd_w_out, rwkv_mu, rwkv_w0, rwkv_w2, rwkv_a0, rwkv_a2, rwkv_g2, rwkv_k_k, rwkv_k_a, rwkv_r_k, rwkv_ln_w, rwkv_ln_b, lru_conv_w, lru_conv_b, lru_wa, lru_ba, lru_wx, lru_bx, lru_lambda):
    batch, seq, d = x.shape
    depth = norm_w.shape[0]
    x2 = x.reshape(batch * seq, d)
    for i in range(depth):
        j = i // 2
        nw = norm_w[i]
        x2 = _ffn(x2, nw[0], nw[1], ffn_w_gate[i, 0], ffn_w_up[i, 0], ffn_w_down[i, 0])
        if i % 2 == 0:
            x2 = _even_layer(x2, batch, seq, nw, even_w_in[j], even_w_out[j], gla_lora_w2[j], gla_lora_b[j],
                             gla_norm[j], gdn_conv[j], gdn_a_log[j], gdn_dt_bias[j], gdn_norm[j])
        else:
            x2 = _odd_layer(x2, batch, seq, nw, odd_w_in[j], odd_w_out[j], rwkv_mu[j], rwkv_w0[j], rwkv_w2[j],
                            rwkv_a0[j], rwkv_a2[j], rwkv_g2[j], rwkv_k_k[j], rwkv_k_a[j],
                            rwkv_r_k[j].reshape(-1), rwkv_ln_w[j], rwkv_ln_b[j], lru_conv_w[j], lru_conv_b[j],
                            lru_wa[j], lru_ba[j], lru_wx[j], lru_bx[j], lru_lambda[j])
        x2 = _ffn(x2, nw[4], nw[5], ffn_w_gate[i, 1], ffn_w_up[i, 1], ffn_w_down[i, 1])
    return x2.reshape(batch, seq, d)
```

```python
import functools

import jax
import jax.numpy as jnp
from jax import lax
from jax.experimental import pallas as pl
from jax.experimental.pallas import tpu as pltpu

F32 = jnp.float32
BF16 = jnp.bfloat16

D_MODEL = 1024
D_FF = 2816
FFN_RES = 0.5
NORM_EPS = 1e-6
CONV_W = 4

GLA_HEADS = 4
GLA_DK = 64
GLA_DV = 128
GLA_LORA = 16
GLA_GATE_NORM = 16.0
GLA_QK = GLA_HEADS * GLA_DK
GLA_V = GLA_HEADS * GLA_DV
GLA_CHUNK = 64

GDN_HEADS = 4
GDN_DK = 128
GDN_DV = 128
GDN_QK = GDN_HEADS * GDN_DK
GDN_V = GDN_HEADS * GDN_DV
GDN_CONV_CH = 2 * GDN_QK + GDN_V
GDN_CHUNK = 128

RWKV_HEADS = 8
RWKV_N = 64
RWKV_W = RWKV_HEADS * RWKV_N
RWKV_W_LORA = 64
RWKV_A_LORA = 64
RWKV_G_LORA = 128
RWKV_GN_EPS = 64e-5
RWKV_SHIFT = 3 * RWKV_W + RWKV_W_LORA + RWKV_A_LORA + RWKV_G_LORA
RWKV_CHUNK = 64
RWKV_DECAY_SCALE = 0.6065306597126334

LRU_WIDTH = 512
LRU_BLOCKS = 8
LRU_BW = LRU_WIDTH // LRU_BLOCKS
LRU_C = 8.0

EVEN_OUT = GLA_V + GDN_V
ODD_OUT = RWKV_W + LRU_WIDTH

LANES = 128
SUBLANES = 8
SMALL_GLR = 0
SMALL_DA = GLA_LORA
SMALL_DB = GLA_LORA + GDN_HEADS

TILE_FFN = 512
TILE_PREP = 512
TILE_CHUNK = 128
FF_CHUNK = 2816
VMEM_LIMIT = 56 * 1024 * 1024


def _mm(a, b):
    return jnp.dot(a.astype(BF16), b.astype(BF16), preferred_element_type=F32)


def _mm_nt(a, b):
    return lax.dot_general(a.astype(BF16), b.astype(BF16), (((1,), (1,)), ((), ())),
                           preferred_element_type=F32)


def _rms(x, w):
    return x * lax.rsqrt(jnp.mean(x * x, axis=-1, keepdims=True) + NORM_EPS) * w


def _sigmoid(x):
    return 0.5 * jnp.tanh(0.5 * x) + 0.5


def _silu(x):
    return x * _sigmoid(x)


def _softplus(x):
    return jnp.maximum(x, 0.0) + jnp.log(1.0 + jnp.exp(-jnp.abs(x)))


def _gelu_tanh(x):
    c = 0.7978845608028654
    return 0.5 * x * (1.0 + jnp.tanh(c * (x + 0.044715 * (x * x * x))))


def _seg_cumsum(x, seg):
    rows = lax.broadcasted_iota(jnp.int32, x.shape, 0) & (seg - 1)
    d = 1
    while d < seg:
        x = x + jnp.where(rows >= d, pltpu.roll(x, d, 0), 0.0)
        d *= 2
    return x


def _shift_rows(x, carry, s):
    sh = pltpu.roll(x, s, 0)
    c = pltpu.roll(carry, s, 0)
    rows = lax.broadcasted_iota(jnp.int32, c.shape, 0)
    head = jnp.where(rows < s, c, sh[0:SUBLANES])
    return jnp.concatenate([head, sh[SUBLANES:]], axis=0)


def _group_sum(x, ones_bd):
    hi = x.astype(BF16)
    lo = (x - hi.astype(F32)).astype(BF16)
    dot = functools.partial(jnp.dot, preferred_element_type=F32)
    return dot(hi, ones_bd) + dot(lo, ones_bd)


def _split(a):
    hi = a.astype(BF16)
    return hi, (a - hi.astype(F32)).astype(BF16)


def _mm3(a, b, nt=False):
    dims = (((1,), (1 if nt else 0,)), ((), ()))
    ah, al = _split(a)
    bh, bl = _split(b)
    a2 = jnp.concatenate([ah, al], axis=1)
    bcat = jnp.concatenate([bh, bl], axis=0 if nt else 1)
    out = lax.dot_general(a2, jnp.concatenate([bcat, bcat], axis=1 if nt else 0), dims,
                          preferred_element_type=F32)
    n = out.shape[1] // 2
    return out[:, 0:n] + out[:, n:]


def _head_masks(half):
    lane = lax.broadcasted_iota(jnp.int32, (1, 2 * half), 1)
    m0 = jnp.where(lane < half, 1.0, 0.0)
    return m0, 1.0 - m0


def _unfold(z, masks):
    return jnp.concatenate([z * masks[0], z * masks[1]], axis=0)


def _fold(z):
    c = z.shape[0] // 2
    return z[0:c] + z[c:]


def _inv_unit_lower_folded(lows_f, masks, mm=_mm):
    c = lows_f[0].shape[0]
    r = lax.broadcasted_iota(jnp.int32, lows_f[0].shape, 0)
    col = lax.broadcasted_iota(jnp.int32, lows_f[0].shape, 1) & (c - 1)
    eye = jnp.where(r == col, 1.0, 0.0)
    negs = [-low for low in lows_f]
    ts = [eye + p for p in negs]
    ps = [mm(p, _unfold(p, masks)) for p in negs]
    k = 4
    while k < c:
        pq = [mm(p, jnp.concatenate([_unfold(p, masks), _unfold(t, masks)], axis=1)) for p, t in zip(ps, ts)]
        ps = [z[:, 0:2 * c] for z in pq]
        ts = [t + z[:, 2 * c:] for t, z in zip(ts, pq)]
        k *= 2
    return [t + mm(p, _unfold(t, masks)) for p, t in zip(ps, ts)]


def _const_spec(shape):
    nd = len(shape)
    return pl.BlockSpec(shape, lambda *_: (0,) * nd, pipeline_mode=pl.Buffered(1))


def _row(v):
    return v.reshape(1, -1).astype(F32)


def _ffn_kernel(x_ref, nw_ref, wg_ref, wu_ref, wd_ref, o_ref):
    x = x_ref[...]
    h = _rms(x, nw_ref[0:1, :]).astype(BF16)
    acc = jnp.zeros(x.shape, F32)
    for c in range(D_FF // FF_CHUNK):
        sl = slice(c * FF_CHUNK, (c + 1) * FF_CHUNK)
        g = jnp.dot(h, wg_ref[:, sl], preferred_element_type=F32)
        u = jnp.dot(h, wu_ref[:, sl], preferred_element_type=F32)
        a = (_silu(g) * u).astype(BF16)
        acc = acc + jnp.dot(a, wd_ref[sl, :], preferred_element_type=F32)
    o_ref[...] = x + FFN_RES * _rms(acc, nw_ref[1:2, :])


def _ffn(x2, nw_pre, nw_post, wg, wu, wd):
    m, d = x2.shape
    tm = min(TILE_FFN, m)
    nw = jnp.stack([nw_pre, nw_post]).astype(F32)
    return pl.pallas_call(
        _ffn_kernel,
        grid=(m // tm,),
        in_specs=[
            pl.BlockSpec((tm, d), lambda i: (i, 0)),
            _const_spec((2, d)),
            _const_spec((d, D_FF)),
            _const_spec((d, D_FF)),
            _const_spec((D_FF, d)),
        ],
        out_specs=pl.BlockSpec((tm, d), lambda i: (i, 0)),
        out_shape=jax.ShapeDtypeStruct((m, d), F32),
        compiler_params=pltpu.CompilerParams(dimension_semantics=("arbitrary",),
                                             vmem_limit_bytes=VMEM_LIMIT),
        name="ffn",
    )(x2, nw, wg.astype(BF16), wu.astype(BF16), wd.astype(BF16))


def _even_prep_kernel(tiles_per_seq,
                      x_ref, nw_ref, wa_ref, ws_ref, wc_ref, wz_ref, lw2_ref, lb_ref, conv_ref, hp_ref,
                      qe_ref, kt_ref, kd_ref, al_ref, gv_ref, sgg_ref, dq_ref, dk_ref, dv_ref, gb_ref,
                      sdz_ref, carry_ref):
    first = (pl.program_id(0) % tiles_per_seq) == 0
    tm = x_ref.shape[0]
    h = _rms(x_ref[...], nw_ref[...]).astype(BF16)

    pa = jnp.dot(h, wa_ref[...], preferred_element_type=F32)
    ps = jnp.dot(h, ws_ref[...], preferred_element_type=F32)
    pc = jnp.dot(h, wc_ref[...], preferred_element_type=F32)
    pz = jnp.dot(h, wz_ref[...], preferred_element_type=F32)
    z = _mm(ps, lw2_ref[...]) + lb_ref[...]
    log_a = -_softplus(-z) * (1.0 / GLA_GATE_NORM)
    b = _seg_cumsum(log_a, GLA_CHUNK)
    gk = pa[:, GLA_QK:2 * GLA_QK]
    qe_ref[...] = pa[:, 0:GLA_QK] * (GLA_DK ** -0.5) * jnp.exp(b)
    kt_ref[...] = gk * jnp.exp(-b)
    for c in range(tm // GLA_CHUNK):
        cs = slice(c * GLA_CHUNK, (c + 1) * GLA_CHUNK)
        b_last = b[(c + 1) * GLA_CHUNK - 1:(c + 1) * GLA_CHUNK, :]
        al_ref[c] = jnp.exp(b_last)
        kd_ref[cs, :] = gk[cs] * jnp.exp(b_last - b[cs])
    gv_ref[...] = pa[:, 2 * GLA_QK:2 * GLA_QK + GLA_V]
    sgg_ref[...] = _silu(pa[:, 2 * GLA_QK + GLA_V:])

    carry = jnp.where(first, 0.0, carry_ref[...])
    conv = pc * conv_ref[CONV_W - 1:CONV_W, :]
    for s in range(1, CONV_W):
        conv = conv + _shift_rows(pc, carry, s) * conv_ref[CONV_W - 1 - s:CONV_W - s, :]
    carry_ref[...] = pc[tm - SUBLANES:tm, :]
    c = _silu(conv)
    for hd in range(GDN_HEADS):
        lo, hi = hd * GDN_DK, (hd + 1) * GDN_DK
        cq = c[:, lo:hi]
        ck = c[:, GDN_QK + lo:GDN_QK + hi]
        dq_ref[:, lo:hi] = cq * lax.rsqrt(jnp.sum(cq * cq, -1, keepdims=True) + NORM_EPS) * (GDN_DK ** -0.5)
        dk_ref[:, lo:hi] = ck * lax.rsqrt(jnp.sum(ck * ck, -1, keepdims=True) + NORM_EPS)
    dv_ref[...] = c[:, 2 * GDN_QK:]

    g = -jnp.exp(hp_ref[0:1, :]) * _softplus(ps + hp_ref[1:2, :])
    gc = _seg_cumsum(g, GDN_CHUNK)
    lane = lax.broadcasted_iota(jnp.int32, ps.shape, 1)
    is_g = (lane >= SMALL_DA) & (lane < SMALL_DB)
    gb_ref[...] = jnp.where(is_g, gc, _sigmoid(ps))

    sdz_ref[...] = _silu(pz)


def _even_prep(x2, seq, nw, w_in, lora_w2, lora_b, conv_w, a_log, dt_bias):
    m, d = x2.shape
    tm = min(TILE_PREP, seq)
    o = 0
    w_a = w_in[:, 0:2 * GLA_QK + 2 * GLA_V]
    o = 2 * GLA_QK + 2 * GLA_V
    w_glr = w_in[:, o:o + GLA_LORA]
    o += GLA_LORA
    w_c = w_in[:, o:o + GDN_CONV_CH]
    o += GDN_CONV_CH
    w_z = w_in[:, o:o + GDN_V]
    o += GDN_V
    w_da = w_in[:, o:o + GDN_HEADS]
    w_db = w_in[:, o + GDN_HEADS:o + 2 * GDN_HEADS]
    n_small = GLA_LORA + 2 * GDN_HEADS
    w_s = jnp.concatenate([w_glr, w_da, w_db, jnp.zeros((d, LANES - n_small), F32)], axis=1)
    lw2 = jnp.concatenate([lora_w2, jnp.zeros((LANES - GLA_LORA, GLA_QK), F32)], axis=0)
    pad_l = jnp.zeros((SMALL_DA,), F32)
    pad_r = jnp.zeros((LANES - SMALL_DB,), F32)
    hp = jnp.stack([jnp.concatenate([pad_l, a_log.astype(F32), pad_r]),
                    jnp.concatenate([pad_l, dt_bias.astype(F32), pad_r])])

    def tok(n):
        return pl.BlockSpec((tm, n), lambda i: (i, 0))

    outs = [
        (GLA_QK, tok(GLA_QK)), (GLA_QK, tok(GLA_QK)), (GLA_QK, tok(GLA_QK)),
        (None, pl.BlockSpec((tm // GLA_CHUNK, 1, GLA_QK), lambda i: (i, 0, 0))),
        (GLA_V, tok(GLA_V)), (GLA_V, tok(GLA_V)),
        (GDN_QK, tok(GDN_QK)), (GDN_QK, tok(GDN_QK)), (GDN_V, tok(GDN_V)),
        (LANES, tok(LANES)), (GDN_V, tok(GDN_V)),
    ]
    out_shape = [jax.ShapeDtypeStruct((m, n), F32) if n is not None
                 else jax.ShapeDtypeStruct((m // GLA_CHUNK, 1, GLA_QK), F32) for n, _ in outs]
    return pl.pallas_call(
        functools.partial(_even_prep_kernel, seq // tm),
        grid=(m // tm,),
        in_specs=[
            tok(d), _const_spec((1, d)),
            _const_spec(w_a.shape), _const_spec(w_s.shape), _const_spec(w_c.shape), _const_spec(w_z.shape),
            _const_spec(lw2.shape), _const_spec((1, GLA_QK)), _const_spec(conv_w.shape), _const_spec(hp.shape),
        ],
        out_specs=[s for _, s in outs],
        out_shape=out_shape,
        scratch_shapes=[pltpu.VMEM((SUBLANES, GDN_CONV_CH), F32)],
        compiler_params=pltpu.CompilerParams(dimension_semantics=("arbitrary",),
                                             vmem_limit_bytes=VMEM_LIMIT),
        name="even_prep",
    )(x2, _row(nw), w_a.astype(BF16), w_s.astype(BF16), w_c.astype(BF16), w_z.astype(BF16),
      lw2.astype(BF16), _row(lora_b), conv_w.astype(F32), hp)


def _even_chunk_kernel(x_ref, qe_ref, kt_ref, kd_ref, al_ref, gv_ref, sgg_ref, dq_ref, dk_ref, dv_ref,
                       gb_ref, sdz_ref, wo_ref, gn_ref, dn_ref, nw_ref,
                       o_ref, s_gla, s_gdn, o_scr):
    nb, tc, d = x_ref.shape

    @pl.when(pl.program_id(0) == 0)
    def _():
        s_gla[...] = jnp.zeros(s_gla.shape, F32)
        s_gdn[...] = jnp.zeros(s_gdn.shape, F32)

    n = GDN_CHUNK
    r_i = lax.broadcasted_iota(jnp.int32, (n, n), 0)
    c_i = lax.broadcasted_iota(jnp.int32, (n, n), 1)
    tril = c_i <= r_i
    strict = c_i < r_i
    same_half = (c_i >= n // 2) == (r_i >= n // 2)
    hmasks = _head_masks(n // 2)
    bd_incl = ((c_i & (GLA_CHUNK - 1)) <= (r_i & (GLA_CHUNK - 1))) & ((c_i >= GLA_CHUNK) == (r_i >= GLA_CHUNK))
    lane = lax.broadcasted_iota(jnp.int32, (1, LANES), 1)
    m0 = jnp.where(lane < GLA_DK, 1.0, 0.0)
    m1 = 1.0 - m0

    def body(s, carry):
        r0 = pl.multiple_of(s * n, n)
        rows = pl.ds(r0, n)
        probs = [(b, hd) for b in range(nb) for hd in range(GDN_HEADS)]
        np_ = len(probs)
        hls = [slice(hd * GDN_DK, (hd + 1) * GDN_DK) for _, hd in probs]
        ld = lambda ref: [ref[b, rows, hls[i]] for i, (b, _) in enumerate(probs)]
        q, k, v = ld(dq_ref), ld(dk_ref), ld(dv_ref)
        gbt = [gb_ref[b, rows, :] for b in range(nb)]
        gc = [jnp.broadcast_to(gbt[b][:, SMALL_DA + hd:SMALL_DA + hd + 1], (n, n)) for b, hd in probs]
        beta = [jnp.broadcast_to(gbt[b][:, SMALL_DB + hd:SMALL_DB + hd + 1], (n, n)) for b, hd in probs]
        decay = [jnp.where(tril, jnp.exp(jnp.where(tril, g - g.T, 0.0)), 0.0) for g in gc]
        kb = [k[i] * beta[i] for i in range(np_)]
        low = [jnp.where(strict, _mm3(kb[i], k[i], nt=True) * decay[i], 0.0) for i in range(np_)]
        att = [_mm_nt(q[i], k[i]) * decay[i] for i in range(np_)]
        low_bd = [jnp.where(same_half, z, 0.0) for z in low]
        t_bd = [_unfold(z, hmasks) for z in _inv_unit_lower_folded([_fold(z) for z in low_bd], hmasks, mm=_mm3)]
        w_off = [_mm3(t_bd[i][n // 2:], low[i] - low_bd[i]) for i in range(np_)]
        w_off = [_mm3(w_off[i], t_bd[i]) for i in range(np_)]
        t = [jnp.concatenate([t_bd[i][0:n // 2], t_bd[i][n // 2:] - w_off[i]], axis=0) for i in range(np_)]
        eg = [jnp.exp(g) for g in gc]
        sol = [_mm3(t[i], jnp.concatenate([v[i] * beta[i], kb[i] * eg[i]], axis=1)) for i in range(np_)]
        g_last = [g[n - 1:n, :] for g in gc]
        k_dec_t = [(k[i] * jnp.exp(g_last[i] - gc[i])).T for i in range(np_)]
        st = [s_gdn[b, hd] for b, hd in probs]
        v_new = [sol[i][:, 0:GDN_DV] - _mm3(sol[i][:, GDN_DV:], st[i]) for i in range(np_)]
        o = [_mm(q[i] * eg[i], st[i]) + _mm(att[i], v_new[i]) for i in range(np_)]
        upd = [_mm3(k_dec_t[i], v_new[i]) for i in range(np_)]
        for i, (b, hd) in enumerate(probs):
            o_scr[pl.ds(b * tc + r0, n), GLA_V + hd * GDN_DV:GLA_V + (hd + 1) * GDN_DV] = o[i]
            s_gdn[b, hd] = st[i] * jnp.exp(g_last[i]) + upd[i]

        gprobs = [(b, p) for b in range(nb) for p in range(GLA_HEADS // 2)]
        ng = len(gprobs)
        pls = [slice(p * LANES, (p + 1) * LANES) for _, p in gprobs]
        for cc in range(n // GLA_CHUNK):
            crow = pl.ds(r0 + cc * GLA_CHUNK, GLA_CHUNK)
            ldg = lambda ref: [ref[b, crow, pls[i]] for i, (b, _) in enumerate(gprobs)]
            qe, kt, kd = ldg(qe_ref), ldg(kt_ref), ldg(kd_ref)
            vs = [jnp.concatenate([gv_ref[b, crow, (2 * p) * GLA_DV:(2 * p + 1) * GLA_DV],
                                   gv_ref[b, crow, (2 * p + 1) * GLA_DV:(2 * p + 2) * GLA_DV]], axis=0)
                  for b, p in gprobs]
            qs = [jnp.concatenate([z * m0, z * m1], axis=0) for z in qe]
            ks = [jnp.concatenate([z * m0, z * m1], axis=0) for z in kd]
            att = [jnp.where(bd_incl, _mm_nt(qs[i], jnp.concatenate([kt[i], kt[i]], axis=0)), 0.0)
                   for i in range(ng)]
            sg = [s_gla[b, p] for b, p in gprobs]
            og = [_mm(att[i], vs[i]) + _mm_nt(qs[i], sg[i]) for i in range(ng)]
            ug = [_mm(vs[i].T, ks[i]) for i in range(ng)]
            for i, (b, p) in enumerate(gprobs):
                ocrow = pl.ds(b * tc + r0 + cc * GLA_CHUNK, GLA_CHUNK)
                al = al_ref[b, s * (n // GLA_CHUNK) + cc][:, pls[i]]
                s_gla[b, p] = sg[i] * al + ug[i]
                o_scr[ocrow, (2 * p) * GLA_DV:(2 * p + 1) * GLA_DV] = og[i][0:GLA_CHUNK]
                o_scr[ocrow, (2 * p + 1) * GLA_DV:(2 * p + 2) * GLA_DV] = og[i][GLA_CHUNK:]
        return carry

    lax.fori_loop(0, tc // n, body, 0)

    sgg = sgg_ref[...].reshape(nb * tc, GLA_V)
    sdz = sdz_ref[...].reshape(nb * tc, GDN_V)
    for hd in range(GLA_HEADS + GDN_HEADS):
        hl = slice(hd * LANES, (hd + 1) * LANES)
        oh = o_scr[:, hl]
        nw = gn_ref[...] if hd < GLA_HEADS else dn_ref[...]
        gate = sgg[:, hl] if hd < GLA_HEADS else sdz[:, (hd - GLA_HEADS) * LANES:(hd - GLA_HEADS + 1) * LANES]
        o_scr[:, hl] = oh * lax.rsqrt(jnp.mean(oh * oh, -1, keepdims=True) + NORM_EPS) * nw * gate
    y = jnp.dot(o_scr[...].astype(BF16), wo_ref[...], preferred_element_type=F32)
    o_ref[...] = x_ref[...] + _rms(y, nw_ref[...]).reshape(nb, tc, d)


def _even_chunk(x2, batch, seq, prep, w_out, gla_norm, gdn_norm, nw):
    m, d = x2.shape
    tc = min(TILE_CHUNK, seq)
    qe, kt, kd, al, gv, sgg, dq, dk, dv, gb, sdz = [a.reshape((batch, a.shape[0] // batch) + a.shape[1:])
                                                     for a in prep]

    def tok(n):
        return pl.BlockSpec((batch, tc, n), lambda j: (0, j, 0))

    out = pl.pallas_call(
        _even_chunk_kernel,
        grid=(seq // tc,),
        in_specs=[
            tok(d), tok(GLA_QK), tok(GLA_QK), tok(GLA_QK),
            pl.BlockSpec((batch, tc // GLA_CHUNK, 1, GLA_QK), lambda j: (0, j, 0, 0)),
            tok(GLA_V), tok(GLA_V), tok(GDN_QK), tok(GDN_QK), tok(GDN_V), tok(LANES), tok(GDN_V),
            _const_spec((EVEN_OUT, d)), _const_spec((1, GLA_DV)), _const_spec((1, GDN_DV)), _const_spec((1, d)),
        ],
        out_specs=tok(d),
        out_shape=jax.ShapeDtypeStruct((batch, seq, d), F32),
        scratch_shapes=[
            pltpu.VMEM((batch, GLA_HEADS // 2, GLA_DV, LANES), F32),
            pltpu.VMEM((batch, GDN_HEADS, GDN_DK, GDN_DV), F32),
            pltpu.VMEM((batch * tc, EVEN_OUT), F32),
        ],
        compiler_params=pltpu.CompilerParams(dimension_semantics=("arbitrary",),
                                             vmem_limit_bytes=VMEM_LIMIT),
        name="even_chunk",
    )(x2.reshape(batch, seq, d), qe, kt, kd, al, gv, sgg, dq, dk, dv, gb, sdz, w_out.astype(BF16),
      _row(gla_norm), _row(gdn_norm), _row(nw))
    return out.reshape(m, d)


def _odd_prep_kernel(tiles_per_seq,
                     x_ref, nw_ref, wp_ref, wl_ref, mu_ref, w0_ref, w2_ref, a0_ref, a2_ref, g2_ref,
                     kk_ref, ka_ref, rk_ref, ones_ref, cw_ref, cb_ref, wa_ref, ba_ref, wx_ref, bx_ref, lam_ref,
                     rt_ref, kt_ref, bt_ref, kh_ref, v_ref, wc_ref, g_ref, bonus_ref, la_ref, lb_ref, gly_ref,
                     carry_p, carry_l):
    first = (pl.program_id(0) % tiles_per_seq) == 0
    tm = x_ref.shape[0]
    h = _rms(x_ref[...], nw_ref[...]).astype(BF16)

    ps = jnp.dot(h, wp_ref[...], preferred_element_type=F32)
    carry = jnp.where(first, 0.0, carry_p[...])
    prev = _shift_rows(ps, carry, 1)
    carry_p[...] = ps[tm - SUBLANES:tm, :]
    ps = ps + (prev - ps) * mu_ref[...]
    r = ps[:, 0:RWKV_W]
    k = ps[:, RWKV_W:2 * RWKV_W]
    v = ps[:, 2 * RWKV_W:3 * RWKV_W]
    lora = ps[:, 3 * RWKV_W:3 * RWKV_W + LANES]
    gl = ps[:, 3 * RWKV_W + LANES:]
    logw = -(RWKV_DECAY_SCALE * _sigmoid(w0_ref[...] + _mm(jnp.tanh(lora), w2_ref[...])))
    a = _sigmoid(a0_ref[...] + _mm(lora, a2_ref[...]))
    g_ref[...] = _mm(_sigmoid(gl), g2_ref[...])
    ones_bd = ones_ref[...]
    kk = k * kk_ref[...]
    kk = kk * lax.rsqrt(_group_sum(kk * kk, ones_bd) + NORM_EPS)
    k = k * (1.0 + (a - 1.0) * ka_ref[...])
    bonus_ref[...] = _group_sum(r * k * rk_ref[...], ones_bd) * v
    gcum = _seg_cumsum(logw, RWKV_CHUNK)
    for c in range(tm // RWKV_CHUNK):
        wc_ref[c] = jnp.exp(gcum[(c + 1) * RWKV_CHUNK - 1:(c + 1) * RWKV_CHUNK, :])
    e_neg = jnp.exp(-gcum)
    rt_ref[...] = r * jnp.exp(gcum)
    kt_ref[...] = k * e_neg
    bt_ref[...] = kk * a * e_neg
    kh_ref[...] = kk * jnp.exp(gcum - logw)
    v_ref[...] = v

    pq = jnp.dot(h, wl_ref[...], preferred_element_type=F32)
    lx = pq[:, 0:LRU_WIDTH]
    carry2 = jnp.where(first, 0.0, carry_l[...])
    xb = lx * cw_ref[CONV_W - 1:CONV_W, :] + cb_ref[...]
    for s in range(1, CONV_W):
        xb = xb + _shift_rows(lx, carry2, s) * cw_ref[CONV_W - 1 - s:CONV_W - s, :]
    carry_l[...] = lx[tm - SUBLANES:tm, :]
    gate_r = _sigmoid(_mm(xb, wa_ref[...]) + ba_ref[...])
    gate_i = _sigmoid(_mm(xb, wx_ref[...]) + bx_ref[...])
    log_a = -LRU_C * gate_r * _softplus(-lam_ref[...])
    mult = jnp.sqrt(jnp.maximum(-jnp.tanh(log_a) * (jnp.exp(2.0 * log_a) + 1.0), 0.0))
    la_ref[...] = jnp.exp(log_a)
    lb_ref[...] = mult * gate_i * xb
    gly_ref[...] = _gelu_tanh(pq[:, LRU_WIDTH:])


def _block_diag(w):
    nb, n, _ = w.shape
    eye = jnp.eye(nb, dtype=w.dtype)
    return (eye[:, None, :, None] * w[:, :, None, :]).reshape(nb * n, nb * n)


def _odd_prep(x2, seq, nw, w_in, mu, w0, w2, a0, a2, g2, k_k, k_a, r_k, ones_bd,
              conv_w, conv_b, wa, ba, wx, bx, lam):
    m, d = x2.shape
    tm = min(TILE_PREP, seq)
    w_p = w_in[:, 0:RWKV_SHIFT]
    w_l = w_in[:, RWKV_SHIFT:]
    w2p = jnp.concatenate([w2, jnp.zeros((RWKV_A_LORA, RWKV_W), F32)], axis=0)
    a2p = jnp.concatenate([jnp.zeros((RWKV_W_LORA, RWKV_W), F32), a2], axis=0)

    def tok(n):
        return pl.BlockSpec((tm, n), lambda i: (i, 0))

    wspec = pl.BlockSpec((tm // RWKV_CHUNK, 1, RWKV_W), lambda i: (i, 0, 0))
    out_specs = [tok(RWKV_W)] * 5 + [wspec] + [tok(RWKV_W)] * 2 + [tok(LRU_WIDTH)] * 3
    out_shape = ([jax.ShapeDtypeStruct((m, RWKV_W), F32)] * 5
                 + [jax.ShapeDtypeStruct((m // RWKV_CHUNK, 1, RWKV_W), F32)]
                 + [jax.ShapeDtypeStruct((m, RWKV_W), F32)] * 2
                 + [jax.ShapeDtypeStruct((m, LRU_WIDTH), F32)] * 3)
    vec = _const_spec((1, RWKV_W))
    sq = _const_spec((RWKV_W, RWKV_W))
    return pl.pallas_call(
        functools.partial(_odd_prep_kernel, seq // tm),
        grid=(m // tm,),
        in_specs=[
            tok(d), _const_spec((1, d)), _const_spec(w_p.shape), _const_spec(w_l.shape),
            _const_spec((1, RWKV_SHIFT)), vec, _const_spec(w2p.shape), vec, _const_spec(a2p.shape),
            _const_spec(g2.shape), vec, vec, vec, sq,
            _const_spec(conv_w.shape), vec, sq, vec, sq, vec, vec,
        ],
        out_specs=out_specs,
        out_shape=out_shape,
        scratch_shapes=[pltpu.VMEM((SUBLANES, RWKV_SHIFT), F32), pltpu.VMEM((SUBLANES, LRU_WIDTH), F32)],
        compiler_params=pltpu.CompilerParams(dimension_semantics=("arbitrary",),
                                             vmem_limit_bytes=VMEM_LIMIT),
        name="odd_prep",
    )(x2, _row(nw), w_p.astype(BF16), w_l.astype(BF16), _row(mu), _row(w0), w2p.astype(BF16), _row(a0),
      a2p.astype(BF16), g2.astype(BF16), _row(k_k), _row(k_a), _row(r_k), ones_bd,
      conv_w.astype(F32), _row(conv_b), _block_diag(wa).astype(BF16), _row(ba),
      _block_diag(wx).astype(BF16), _row(bx), _row(lam))


def _odd_chunk_kernel(x_ref, rt_ref, kt_ref, bt_ref, kh_ref, v_ref, wc_ref, g_ref, bonus_ref, la_ref, lb_ref,
                      gly_ref, wo_ref, lnw_ref, lnb_ref, ones_ref, nw_ref,
                      o_ref, s_rwkv, h_lru, y_scr):
    nb, tc, d = x_ref.shape

    @pl.when(pl.program_id(0) == 0)
    def _():
        s_rwkv[...] = jnp.zeros(s_rwkv.shape, F32)
        h_lru[...] = jnp.zeros(h_lru.shape, F32)

    cs = RWKV_CHUNK
    r_i = lax.broadcasted_iota(jnp.int32, (cs, 2 * cs), 0)
    c_i = lax.broadcasted_iota(jnp.int32, (cs, 2 * cs), 1) & (cs - 1)
    f_incl = c_i <= r_i
    f_strict = c_i < r_i
    masks = _head_masks(RWKV_N)
    stack = lambda z: _unfold(z, masks)

    def body(c, carry):
        r0 = pl.multiple_of(c * cs, cs)
        rows = pl.ds(r0, cs)
        probs = [(b, p) for b in range(nb) for p in range(RWKV_HEADS // 2)]
        pls = [slice(p * LANES, (p + 1) * LANES) for _, p in probs]
        np_ = len(probs)
        ld = lambda ref: [ref[b, rows, pls[i]] for i, (b, _) in enumerate(probs)]
        rt, kt, bt, kh, vv = ld(rt_ref), ld(kt_ref), ld(bt_ref), ld(kh_ref), ld(v_ref)
        vs = [stack(z) for z in vv]
        ks = [stack(z) for z in kt]
        bs = [stack(z) for z in bt]
        mm = [_mm_nt(jnp.concatenate([kh[i], rt[i]], axis=0), jnp.concatenate([bs[i], ks[i]], axis=0))
              for i in range(np_)]
        m_rb = [jnp.where(f_incl, mm[i][cs:, 0:LANES], 0.0) for i in range(np_)]
        m_bk = [jnp.where(f_strict, mm[i][0:cs, LANES:], 0.0) for i in range(np_)]
        m_rk = [jnp.where(f_incl, mm[i][cs:, LANES:], 0.0) for i in range(np_)]
        t = _inv_unit_lower_folded([jnp.where(f_strict, mm[i][0:cs, 0:LANES], 0.0) for i in range(np_)], masks)
        x = [_mm(m_bk[i], vs[i]) for i in range(np_)]
        sol = [_mm(t[i], jnp.concatenate([stack(kh[i]), stack(x[i])], axis=1)) for i in range(np_)]
        sol_s = [jnp.concatenate([stack(z[:, 0:LANES]), stack(z[:, LANES:])], axis=1) for z in sol]
        corr = [_mm(m_rb[i], sol_s[i]) for i in range(np_)]
        y_loc = [_mm(m_rk[i], vs[i]) - corr[i][:, LANES:] for i in range(np_)]
        r_eff = [rt[i] - corr[i][:, 0:LANES] for i in range(np_)]
        vtk = [_mm(_fold(vs[i].T), ks[i]) for i in range(np_)]
        uv_t = [_fold(sol_s[i][:, LANES:].T) for i in range(np_)]
        st = [s_rwkv[b, p] for b, p in probs]
        ut = [_mm_nt(st[i], sol_s[i][:, 0:LANES]) + uv_t[i] for i in range(np_)]
        ys = [_mm_nt(r_eff[i], stack(st[i])) + y_loc[i] for i in range(np_)]
        upd = [_mm(ut[i], bs[i]) for i in range(np_)]
        for i, (b, p) in enumerate(probs):
            y_scr[pl.ds(b * tc + r0, cs), pls[i]] = ys[i]
            s_rwkv[b, p] = (st[i] + vtk[i] - upd[i]) * wc_ref[b, c][:, pls[i]]
        return carry

    lax.fori_loop(0, tc // RWKV_CHUNK, body, 0)

    hls = []
    for b in range(nb):
        a = la_ref[b]
        bb = lb_ref[b]
        rows_i = lax.broadcasted_iota(jnp.int32, a.shape, 0)
        dd = 1
        while dd < tc:
            keep = rows_i >= dd
            bb = jnp.where(keep, a * pltpu.roll(bb, dd, 0) + bb, bb)
            a = jnp.where(keep, a * pltpu.roll(a, dd, 0), a)
            dd *= 2
        hl = bb + a * h_lru[b, SUBLANES - 1:SUBLANES, :]
        h_lru[b] = hl[tc - SUBLANES:tc, :]
        hls.append(hl)
    y_lru = jnp.concatenate(hls, axis=0) * gly_ref[...].reshape(nb * tc, LRU_WIDTH)

    ones_bd = ones_ref[...]
    y = y_scr[...]
    mu = _group_sum(y, ones_bd) * (1.0 / RWKV_N)
    yc = y - mu
    var = _group_sum(yc * yc, ones_bd) * (1.0 / RWKV_N)
    yn = yc * lax.rsqrt(var + RWKV_GN_EPS) * lnw_ref[...] + lnb_ref[...]
    y_rwkv = (yn + bonus_ref[...].reshape(nb * tc, RWKV_W)) * g_ref[...].reshape(nb * tc, RWKV_W)
    out = (jnp.dot(y_rwkv.astype(BF16), wo_ref[0:RWKV_W, :], preferred_element_type=F32)
           + jnp.dot(y_lru.astype(BF16), wo_ref[RWKV_W:, :], preferred_element_type=F32))
    o_ref[...] = x_ref[...] + _rms(out, nw_ref[...]).reshape(nb, tc, d)


def _odd_chunk(x2, batch, seq, prep, w_out, ln_w, ln_b, ones_bd, nw):
    m, d = x2.shape
    tc = min(TILE_CHUNK, seq)
    rt, kt, bt, kh, v, wc, g, bonus, la, lb, gly = [a.reshape((batch, a.shape[0] // batch) + a.shape[1:])
                                                     for a in prep]

    def tok(n):
        return pl.BlockSpec((batch, tc, n), lambda j: (0, j, 0))

    vec = _const_spec((1, RWKV_W))
    out = pl.pallas_call(
        _odd_chunk_kernel,
        grid=(seq // tc,),
        in_specs=[
            tok(d), tok(RWKV_W), tok(RWKV_W), tok(RWKV_W), tok(RWKV_W), tok(RWKV_W),
            pl.BlockSpec((batch, tc // RWKV_CHUNK, 1, RWKV_W), lambda j: (0, j, 0, 0)),
            tok(RWKV_W), tok(RWKV_W), tok(LRU_WIDTH), tok(LRU_WIDTH), tok(LRU_WIDTH),
            _const_spec((ODD_OUT, d)), vec, vec, _const_spec((RWKV_W, RWKV_W)), _const_spec((1, d)),
        ],
        out_specs=tok(d),
        out_shape=jax.ShapeDtypeStruct((batch, seq, d), F32),
        scratch_shapes=[
            pltpu.VMEM((batch, RWKV_HEADS // 2, RWKV_N, LANES), F32),
            pltpu.VMEM((batch, SUBLANES, LRU_WIDTH), F32),
            pltpu.VMEM((batch * tc, RWKV_W), F32),
        ],
        compiler_params=pltpu.CompilerParams(dimension_semantics=("arbitrary",),
                                             vmem_limit_bytes=VMEM_LIMIT),
        name="odd_chunk",
    )(x2.reshape(batch, seq, d), rt, kt, bt, kh, v, wc, g, bonus, la, lb, gly, w_out.astype(BF16), _row(ln_w),
      _row(ln_b), ones_bd, _row(nw))
    return out.reshape(m, d)


def _even_layer(x2, batch, seq, nw, w_in, w_out, lora_w2, lora_b, gla_norm, conv_w, a_log, dt_bias, gdn_norm):
    prep = _even_prep(x2, seq, nw[2], w_in, lora_w2, lora_b, conv_w, a_log, dt_bias)
    return _even_chunk(x2, batch, seq, prep, w_out, gla_norm, gdn_norm, nw[3])


def _odd_layer(x2, batch, seq, nw, w_in, w_out, mu, w0, w2, a0, a2, g2, k_k, k_a, r_k, ln_w, ln_b,
               conv_w, conv_b, wa, ba, wx, bx, lam):
    grp = jnp.arange(RWKV_W) // RWKV_N
    ones_bd = (grp[:, None] == grp[None, :]).astype(BF16)
    prep = _odd_prep(x2, seq, nw[2], w_in, mu, w0, w2, a0, a2, g2, k_k, k_a, r_k, ones_bd,
                     conv_w, conv_b, wa, ba, wx, bx, lam)
    return _odd_chunk(x2, batch, seq, prep, w_out, ln_w, ln_b, ones_bd, nw[3])


def kernel(x, norm_w, ffn_w_gate, ffn_w_up, ffn_w_down, even_w_in, even_w_out, gla_lora_w2, gla_lora_b, gla_norm, gdn_conv, gdn_a_log, gdn_dt_bias, gdn_norm, odd_w_in, odd_w_out, rwkv_mu, rwkv_w0, rwkv_w2, rwkv_a0, rwkv_a2, rwkv_g2, rwkv_k_k, rwkv_k_a, rwkv_r_k, rwkv_ln_w, rwkv_ln_b, lru_conv_w, lru_conv_b, lru_wa, lru_ba, lru_wx, lru_bx, lru_lambda):
    batch, seq, d = x.shape
    depth = norm_w.shape[0]
    x2 = x.reshape(batch * seq, d)
    for i in range(depth):
        j = i // 2
        nw = norm_w[i]
        x2 = _ffn(x2, nw[0], nw[1], ffn_w_gate[i, 0], ffn_w_up[i, 0], ffn_w_down[i, 0])
        if i % 2 == 0:
            x2 = _even_layer(x2, batch, seq, nw, even_w_in[j], even_w_out[j], gla_lora_w2[j], gla_lora_b[j],
                             gla_norm[j], gdn_conv[j], gdn_a_log[j], gdn_dt_bias[j], gdn_norm[j])
        else:
            x2 = _odd_layer(x2, batch, seq, nw, odd_w_in[j], odd_w_out[j], rwkv_mu[j], rwkv_w0[j], rwkv_w2[j],
                            rwkv_a0[j], rwkv_a2[j], rwkv_g2[j], rwkv_k_k[j], rwkv_k_a[j],
                            rwkv_r_k[j].reshape(-1), rwkv_ln_w[j], rwkv_ln_b[j], lru_conv_w[j], lru_conv_b[j],
                            lru_wa[j], lru_ba[j], lru_wx[j], lru_bx[j], lru_lambda[j])
        x2 = _ffn(x2, nw[4], nw[5], ffn_w_gate[i, 1], ffn_w_up[i, 1], ffn_w_down[i, 1])
    return x2.reshape(batch, seq, d)
```

```python
import functools

import jax
import jax.numpy as jnp
from jax import lax
from jax.experimental import pallas as pl
from jax.experimental.pallas import tpu as pltpu

F32 = jnp.float32
BF16 = jnp.bfloat16

D_MODEL = 1024
D_FF = 2816
FFN_RES = 0.5
NORM_EPS = 1e-6
CONV_W = 4

GLA_HEADS = 4
GLA_DK = 64
GLA_DV = 128
GLA_LORA = 16
GLA_GATE_NORM = 16.0
GLA_QK = GLA_HEADS * GLA_DK
GLA_V = GLA_HEADS * GLA_DV
GLA_CHUNK = 64

GDN_HEADS = 4
GDN_DK = 128
GDN_DV = 128
GDN_QK = GDN_HEADS * GDN_DK
GDN_V = GDN_HEADS * GDN_DV
GDN_CONV_CH = 2 * GDN_QK + GDN_V
GDN_CHUNK = 128

RWKV_HEADS = 8
RWKV_N = 64
RWKV_W = RWKV_HEADS * RWKV_N
RWKV_W_LORA = 64
RWKV_A_LORA = 64
RWKV_G_LORA = 128
RWKV_GN_EPS = 64e-5
RWKV_SHIFT = 3 * RWKV_W + RWKV_W_LORA + RWKV_A_LORA + RWKV_G_LORA
RWKV_CHUNK = 64
RWKV_DECAY_SCALE = 0.6065306597126334

LRU_WIDTH = 512
LRU_BLOCKS = 8
LRU_BW = LRU_WIDTH // LRU_BLOCKS
LRU_C = 8.0

EVEN_OUT = GLA_V + GDN_V
ODD_OUT = RWKV_W + LRU_WIDTH

LANES = 128
SUBLANES = 8
SMALL_GLR = 0
SMALL_DA = GLA_LORA
SMALL_DB = GLA_LORA + GDN_HEADS

TILE_FFN = 512
TILE_PREP = 512
TILE_CHUNK = 128
FF_CHUNK = 2816
VMEM_LIMIT = 56 * 1024 * 1024


def _mm(a, b):
    return jnp.dot(a.astype(BF16), b.astype(BF16), preferred_element_type=F32)


def _mm_nt(a, b):
    return lax.dot_general(a.astype(BF16), b.astype(BF16), (((1,), (1,)), ((), ())),
                           preferred_element_type=F32)


def _rms(x, w):
    return x * lax.rsqrt(jnp.mean(x * x, axis=-1, keepdims=True) + NORM_EPS) * w


def _sigmoid(x):
    return 0.5 * jnp.tanh(0.5 * x) + 0.5


def _silu(x):
    h = 0.5 * x
    return h + h * jnp.tanh(h)


def _softplus(x):
    return jnp.maximum(x, 0.0) + jnp.log(1.0 + jnp.exp(-jnp.abs(x)))


def _gelu_tanh(x):
    c = 0.7978845608028654
    return 0.5 * x * (1.0 + jnp.tanh(c * (x + 0.044715 * (x * x * x))))


def _seg_cumsum(x, seg):
    rows = lax.broadcasted_iota(jnp.int32, x.shape, 0) & (seg - 1)
    d = 1
    while d < seg:
        x = x + jnp.where(rows >= d, pltpu.roll(x, d, 0), 0.0)
        d *= 2
    return x


def _shift_rows(x, carry, s):
    sh = pltpu.roll(x, s, 0)
    c = pltpu.roll(carry, s, 0)
    rows = lax.broadcasted_iota(jnp.int32, c.shape, 0)
    head = jnp.where(rows < s, c, sh[0:SUBLANES])
    return jnp.concatenate([head, sh[SUBLANES:]], axis=0)


def _group_sum(x, ones_bd):
    hi = x.astype(BF16)
    lo = (x - hi.astype(F32)).astype(BF16)
    dot = functools.partial(jnp.dot, preferred_element_type=F32)
    return dot(hi, ones_bd) + dot(lo, ones_bd)


def _split(a):
    hi = a.astype(BF16)
    return hi, (a - hi.astype(F32)).astype(BF16)


def _parts1(a):
    return (a.astype(BF16),)


def _parts2(a):
    return _split(a)


def _mm_parts(ap, bp, nt=False):
    dims = (((1,), (1 if nt else 0,)), ((), ()))
    a2 = ap[0] if len(ap) == 1 else jnp.concatenate(ap, axis=1)
    bcat = bp[0] if len(bp) == 1 else jnp.concatenate(bp, axis=0 if nt else 1)
    rhs = bcat if len(ap) == 1 else jnp.concatenate([bcat] * len(ap), axis=1 if nt else 0)
    out = lax.dot_general(a2, rhs, dims, preferred_element_type=F32)
    if len(bp) == 1:
        return out
    n = out.shape[1] // 2
    return out[:, 0:n] + out[:, n:]


def _mm3(a, b, nt=False):
    return _mm_parts(_parts2(a), _parts2(b), nt)


def _head_masks(half):
    lane = lax.broadcasted_iota(jnp.int32, (1, 2 * half), 1)
    m0 = jnp.where(lane < half, 1.0, 0.0)
    return m0, 1.0 - m0


def _unfold(z, masks):
    return jnp.concatenate([z * masks[0], z * masks[1]], axis=0)


def _fold(z):
    c = z.shape[0] // 2
    return z[0:c] + z[c:]


def _inv_unit_lower_folded(lows_f, parts):
    c = lows_f[0].shape[0]
    r = lax.broadcasted_iota(jnp.int32, lows_f[0].shape, 0)
    col = lax.broadcasted_iota(jnp.int32, lows_f[0].shape, 1)
    eye = jnp.where(r == (col & (c - 1)), 1.0, 0.0)
    mb = (jnp.where(col < c, 1.0, 0.0).astype(BF16), jnp.where(col < c, 0.0, 1.0).astype(BF16))
    unf = lambda terms: tuple(jnp.concatenate([z * mb[0], z * mb[1]], axis=0) for z in terms)
    negs = [-low for low in lows_f]
    ts = [eye + p for p in negs]
    sp = [parts(p) for p in negs]
    ps = [_mm_parts(x, unf(x)) for x in sp]
    k = 4
    while k < c:
        sq = [parts(p) for p in ps]
        st = [parts(t) for t in ts]
        rhs = [tuple(jnp.concatenate([uq, ut], axis=1) for uq, ut in zip(unf(x), unf(y))) for x, y in zip(sq, st)]
        pq = [_mm_parts(x, y) for x, y in zip(sq, rhs)]
        ps = [z[:, 0:2 * c] for z in pq]
        ts = [t + z[:, 2 * c:] for t, z in zip(ts, pq)]
        k *= 2
    return [t + _mm_parts(parts(p), unf(parts(t))) for p, t in zip(ps, ts)]


def _const_spec(shape):
    nd = len(shape)
    return pl.BlockSpec(shape, lambda *_: (0,) * nd, pipeline_mode=pl.Buffered(1))


def _row(v):
    return v.reshape(1, -1).astype(F32)


def _ffn_kernel(x_ref, nw_ref, wg_ref, wu_ref, wd_ref, o_ref):
    x = x_ref[...]
    h = _rms(x, nw_ref[0:1, :]).astype(BF16)
    acc = jnp.zeros(x.shape, F32)
    for c in range(D_FF // FF_CHUNK):
        sl = slice(c * FF_CHUNK, (c + 1) * FF_CHUNK)
        g = jnp.dot(h, wg_ref[:, sl], preferred_element_type=F32)
        u = jnp.dot(h, wu_ref[:, sl], preferred_element_type=F32)
        a = (_silu(g) * u).astype(BF16)
        acc = acc + jnp.dot(a, wd_ref[sl, :], preferred_element_type=F32)
    o_ref[...] = x + FFN_RES * _rms(acc, nw_ref[1:2, :])


def _ffn(x2, nw_pre, nw_post, wg, wu, wd, layer, slot):
    m, d = x2.shape
    tm = min(TILE_FFN, m)
    nw = jnp.stack([nw_pre, nw_post]).astype(F32)

    def pick(rows, cols):
        return pl.BlockSpec((None, None, rows, cols), lambda i: (layer, slot, 0, 0), pipeline_mode=pl.Buffered(1))

    return pl.pallas_call(
        _ffn_kernel,
        grid=(m // tm,),
        in_specs=[
            pl.BlockSpec((tm, d), lambda i: (i, 0)),
            _const_spec((2, d)),
            pick(d, D_FF),
            pick(d, D_FF),
            pick(D_FF, d),
        ],
        out_specs=pl.BlockSpec((tm, d), lambda i: (i, 0)),
        out_shape=jax.ShapeDtypeStruct((m, d), F32),
        compiler_params=pltpu.CompilerParams(dimension_semantics=("arbitrary",),
                                             vmem_limit_bytes=VMEM_LIMIT),
        name="ffn",
    )(x2, nw, wg, wu, wd)


def _even_prep_kernel(tiles_per_seq,
                      x_ref, nw_ref, wa_ref, ws_ref, wc_ref, wz_ref, lw2_ref, lb_ref, conv_ref, hp_ref,
                      qe_ref, kt_ref, kd_ref, al_ref, gv_ref, sgg_ref, dq_ref, dk_ref, dv_ref, gb_ref,
                      sdz_ref, carry_ref):
    first = (pl.program_id(0) % tiles_per_seq) == 0
    tm = x_ref.shape[0]
    h = _rms(x_ref[...], nw_ref[...]).astype(BF16)

    pa = jnp.dot(h, wa_ref[...], preferred_element_type=F32)
    ps = jnp.dot(h, ws_ref[...], preferred_element_type=F32)
    pc = jnp.dot(h, wc_ref[...], preferred_element_type=F32)
    pz = jnp.dot(h, wz_ref[...], preferred_element_type=F32)
    z = _mm(ps, lw2_ref[...]) + lb_ref[...]
    log_a = -_softplus(-z) * (1.0 / GLA_GATE_NORM)
    b = _seg_cumsum(log_a, GLA_CHUNK)
    gk = pa[:, GLA_QK:2 * GLA_QK]
    qe_ref[...] = pa[:, 0:GLA_QK] * (GLA_DK ** -0.5) * jnp.exp(b)
    kt_ref[...] = gk * jnp.exp(-b)
    for c in range(tm // GLA_CHUNK):
        cs = slice(c * GLA_CHUNK, (c + 1) * GLA_CHUNK)
        b_last = b[(c + 1) * GLA_CHUNK - 1:(c + 1) * GLA_CHUNK, :]
        al_ref[c] = jnp.exp(b_last)
        kd_ref[cs, :] = gk[cs] * jnp.exp(b_last - b[cs])
    gv_ref[...] = pa[:, 2 * GLA_QK:2 * GLA_QK + GLA_V]
    sgg_ref[...] = _silu(pa[:, 2 * GLA_QK + GLA_V:])

    carry = jnp.where(first, 0.0, carry_ref[...])
    conv = pc * conv_ref[CONV_W - 1:CONV_W, :]
    for s in range(1, CONV_W):
        conv = conv + _shift_rows(pc, carry, s) * conv_ref[CONV_W - 1 - s:CONV_W - s, :]
    carry_ref[...] = pc[tm - SUBLANES:tm, :]
    c = _silu(conv)
    for hd in range(GDN_HEADS):
        lo, hi = hd * GDN_DK, (hd + 1) * GDN_DK
        cq = c[:, lo:hi]
        ck = c[:, GDN_QK + lo:GDN_QK + hi]
        dq_ref[:, lo:hi] = cq * lax.rsqrt(jnp.sum(cq * cq, -1, keepdims=True) + NORM_EPS) * (GDN_DK ** -0.5)
        dk_ref[:, lo:hi] = ck * lax.rsqrt(jnp.sum(ck * ck, -1, keepdims=True) + NORM_EPS)
    dv_ref[...] = c[:, 2 * GDN_QK:]

    g = -jnp.exp(hp_ref[0:1, :]) * _softplus(ps + hp_ref[1:2, :])
    gc = _seg_cumsum(g, GDN_CHUNK)
    lane = lax.broadcasted_iota(jnp.int32, ps.shape, 1)
    is_g = (lane >= SMALL_DA) & (lane < SMALL_DB)
    gb_ref[...] = jnp.where(is_g, gc, _sigmoid(ps))

    sdz_ref[...] = _silu(pz)


def _even_prep(x2, seq, nw, w_in, lora_w2, lora_b, conv_w, a_log, dt_bias):
    m, d = x2.shape
    tm = min(TILE_PREP, seq)
    o = 0
    w_a = w_in[:, 0:2 * GLA_QK + 2 * GLA_V]
    o = 2 * GLA_QK + 2 * GLA_V
    w_glr = w_in[:, o:o + GLA_LORA]
    o += GLA_LORA
    w_c = w_in[:, o:o + GDN_CONV_CH]
    o += GDN_CONV_CH
    w_z = w_in[:, o:o + GDN_V]
    o += GDN_V
    w_da = w_in[:, o:o + GDN_HEADS]
    w_db = w_in[:, o + GDN_HEADS:o + 2 * GDN_HEADS]
    n_small = GLA_LORA + 2 * GDN_HEADS
    w_s = jnp.concatenate([w_glr, w_da, w_db, jnp.zeros((d, LANES - n_small), F32)], axis=1)
    lw2 = jnp.concatenate([lora_w2, jnp.zeros((LANES - GLA_LORA, GLA_QK), F32)], axis=0)
    pad_l = jnp.zeros((SMALL_DA,), F32)
    pad_r = jnp.zeros((LANES - SMALL_DB,), F32)
    hp = jnp.stack([jnp.concatenate([pad_l, a_log.astype(F32), pad_r]),
                    jnp.concatenate([pad_l, dt_bias.astype(F32), pad_r])])

    def tok(n):
        return pl.BlockSpec((tm, n), lambda i: (i, 0))

    outs = [
        (GLA_QK, tok(GLA_QK)), (GLA_QK, tok(GLA_QK)), (GLA_QK, tok(GLA_QK)),
        (None, pl.BlockSpec((tm // GLA_CHUNK, 1, GLA_QK), lambda i: (i, 0, 0))),
        (GLA_V, tok(GLA_V)), (GLA_V, tok(GLA_V)),
        (GDN_QK, tok(GDN_QK)), (GDN_QK, tok(GDN_QK)), (GDN_V, tok(GDN_V)),
        (LANES, tok(LANES)), (GDN_V, tok(GDN_V)),
    ]
    out_shape = [jax.ShapeDtypeStruct((m, n), F32) if n is not None
                 else jax.ShapeDtypeStruct((m // GLA_CHUNK, 1, GLA_QK), F32) for n, _ in outs]
    return pl.pallas_call(
        functools.partial(_even_prep_kernel, seq // tm),
        grid=(m // tm,),
        in_specs=[
            tok(d), _const_spec((1, d)),
            _const_spec(w_a.shape), _const_spec(w_s.shape), _const_spec(w_c.shape), _const_spec(w_z.shape),
            _const_spec(lw2.shape), _const_spec((1, GLA_QK)), _const_spec(conv_w.shape), _const_spec(hp.shape),
        ],
        out_specs=[s for _, s in outs],
        out_shape=out_shape,
        scratch_shapes=[pltpu.VMEM((SUBLANES, GDN_CONV_CH), F32)],
        compiler_params=pltpu.CompilerParams(dimension_semantics=("arbitrary",),
                                             vmem_limit_bytes=VMEM_LIMIT),
        name="even_prep",
    )(x2, _row(nw), w_a.astype(BF16), w_s.astype(BF16), w_c.astype(BF16), w_z.astype(BF16),
      lw2.astype(BF16), _row(lora_b), conv_w.astype(F32), hp)


def _even_chunk_kernel(x_ref, qe_ref, kt_ref, kd_ref, al_ref, gv_ref, sgg_ref, dq_ref, dk_ref, dv_ref,
                       gb_ref, sdz_ref, wo_ref, gn_ref, dn_ref, nw_ref,
                       o_ref, s_gla, s_gdn, o_scr):
    nb, tc, d = x_ref.shape

    @pl.when(pl.program_id(0) == 0)
    def _():
        s_gla[...] = jnp.zeros(s_gla.shape, F32)
        s_gdn[...] = jnp.zeros(s_gdn.shape, F32)

    n = GDN_CHUNK
    r_i = lax.broadcasted_iota(jnp.int32, (n, n), 0)
    c_i = lax.broadcasted_iota(jnp.int32, (n, n), 1)
    tril = c_i <= r_i
    strict = c_i < r_i
    same_half = (c_i >= n // 2) == (r_i >= n // 2)
    hmasks = _head_masks(n // 2)
    bd_incl = ((c_i & (GLA_CHUNK - 1)) <= (r_i & (GLA_CHUNK - 1))) & ((c_i >= GLA_CHUNK) == (r_i >= GLA_CHUNK))
    lane = lax.broadcasted_iota(jnp.int32, (1, LANES), 1)
    m0 = jnp.where(lane < GLA_DK, 1.0, 0.0)
    m1 = 1.0 - m0

    def body(s, carry):
        r0 = pl.multiple_of(s * n, n)
        rows = pl.ds(r0, n)
        probs = [(b, hd) for b in range(nb) for hd in range(GDN_HEADS)]
        np_ = len(probs)
        hls = [slice(hd * GDN_DK, (hd + 1) * GDN_DK) for _, hd in probs]
        ld = lambda ref: [ref[b, rows, hls[i]] for i, (b, _) in enumerate(probs)]
        q, k, v = ld(dq_ref), ld(dk_ref), ld(dv_ref)
        gbt = [gb_ref[b, rows, :] for b in range(nb)]
        gc = [jnp.broadcast_to(gbt[b][:, SMALL_DA + hd:SMALL_DA + hd + 1], (n, n)) for b, hd in probs]
        beta = [jnp.broadcast_to(gbt[b][:, SMALL_DB + hd:SMALL_DB + hd + 1], (n, n)) for b, hd in probs]
        decay = [jnp.where(tril, jnp.exp(jnp.where(tril, g - g.T, 0.0)), 0.0) for g in gc]
        kb = [k[i] * beta[i] for i in range(np_)]
        low = [jnp.where(strict, _mm3(kb[i], k[i], nt=True) * decay[i], 0.0) for i in range(np_)]
        att = [_mm_nt(q[i], k[i]) * decay[i] for i in range(np_)]
        low_bd = [jnp.where(same_half, z, 0.0) for z in low]
        t_bd = [_unfold(z, hmasks) for z in _inv_unit_lower_folded([_fold(z) for z in low_bd], _parts2)]
        w_off = [_mm3(t_bd[i][n // 2:], low[i] - low_bd[i]) for i in range(np_)]
        w_off = [_mm3(w_off[i], t_bd[i]) for i in range(np_)]
        t = [jnp.concatenate([t_bd[i][0:n // 2], t_bd[i][n // 2:] - w_off[i]], axis=0) for i in range(np_)]
        eg = [jnp.exp(g) for g in gc]
        sol = [_mm3(t[i], jnp.concatenate([v[i] * beta[i], kb[i] * eg[i]], axis=1)) for i in range(np_)]
        g_last = [g[n - 1:n, :] for g in gc]
        k_dec_t = [(k[i] * jnp.exp(g_last[i] - gc[i])).T for i in range(np_)]
        st = [s_gdn[b, hd] for b, hd in probs]
        v_new = [sol[i][:, 0:GDN_DV] - _mm3(sol[i][:, GDN_DV:], st[i]) for i in range(np_)]
        o = [_mm(q[i] * eg[i], st[i]) + _mm(att[i], v_new[i]) for i in range(np_)]
        upd = [_mm3(k_dec_t[i], v_new[i]) for i in range(np_)]
        for i, (b, hd) in enumerate(probs):
            o_scr[pl.ds(b * tc + r0, n), GLA_V + hd * GDN_DV:GLA_V + (hd + 1) * GDN_DV] = o[i]
            s_gdn[b, hd] = st[i] * jnp.exp(g_last[i]) + upd[i]

        gprobs = [(b, p) for b in range(nb) for p in range(GLA_HEADS // 2)]
        ng = len(gprobs)
        pls = [slice(p * LANES, (p + 1) * LANES) for _, p in gprobs]
        for cc in range(n // GLA_CHUNK):
            crow = pl.ds(r0 + cc * GLA_CHUNK, GLA_CHUNK)
            ldg = lambda ref: [ref[b, crow, pls[i]] for i, (b, _) in enumerate(gprobs)]
            qe, kt, kd = ldg(qe_ref), ldg(kt_ref), ldg(kd_ref)
            vs = [jnp.concatenate([gv_ref[b, crow, (2 * p) * GLA_DV:(2 * p + 1) * GLA_DV],
                                   gv_ref[b, crow, (2 * p + 1) * GLA_DV:(2 * p + 2) * GLA_DV]], axis=0)
                  for b, p in gprobs]
            qs = [jnp.concatenate([z * m0, z * m1], axis=0) for z in qe]
            ks = [jnp.concatenate([z * m0, z * m1], axis=0) for z in kd]
            att = [jnp.where(bd_incl, _mm_nt(qs[i], jnp.concatenate([kt[i], kt[i]], axis=0)), 0.0)
                   for i in range(ng)]
            sg = [s_gla[b, p] for b, p in gprobs]
            og = [_mm(att[i], vs[i]) + _mm_nt(qs[i], sg[i]) for i in range(ng)]
            ug = [_mm(vs[i].T, ks[i]) for i in range(ng)]
            for i, (b, p) in enumerate(gprobs):
                ocrow = pl.ds(b * tc + r0 + cc * GLA_CHUNK, GLA_CHUNK)
                al = al_ref[b, s * (n // GLA_CHUNK) + cc][:, pls[i]]
                s_gla[b, p] = sg[i] * al + ug[i]
                o_scr[ocrow, (2 * p) * GLA_DV:(2 * p + 1) * GLA_DV] = og[i][0:GLA_CHUNK]
                o_scr[ocrow, (2 * p + 1) * GLA_DV:(2 * p + 2) * GLA_DV] = og[i][GLA_CHUNK:]
        return carry

    lax.fori_loop(0, tc // n, body, 0)

    sgg = sgg_ref[...].reshape(nb * tc, GLA_V)
    sdz = sdz_ref[...].reshape(nb * tc, GDN_V)
    for hd in range(GLA_HEADS + GDN_HEADS):
        hl = slice(hd * LANES, (hd + 1) * LANES)
        oh = o_scr[:, hl]
        nw = gn_ref[...] if hd < GLA_HEADS else dn_ref[...]
        gate = sgg[:, hl] if hd < GLA_HEADS else sdz[:, (hd - GLA_HEADS) * LANES:(hd - GLA_HEADS + 1) * LANES]
        o_scr[:, hl] = oh * lax.rsqrt(jnp.mean(oh * oh, -1, keepdims=True) + NORM_EPS) * nw * gate
    y = jnp.dot(o_scr[...].astype(BF16), wo_ref[...], preferred_element_type=F32)
    o_ref[...] = x_ref[...] + _rms(y, nw_ref[...]).reshape(nb, tc, d)


def _even_chunk(x2, batch, seq, prep, w_out, gla_norm, gdn_norm, nw):
    m, d = x2.shape
    tc = min(TILE_CHUNK, seq)
    qe, kt, kd, al, gv, sgg, dq, dk, dv, gb, sdz = [a.reshape((batch, a.shape[0] // batch) + a.shape[1:])
                                                     for a in prep]

    def tok(n):
        return pl.BlockSpec((batch, tc, n), lambda j: (0, j, 0))

    out = pl.pallas_call(
        _even_chunk_kernel,
        grid=(seq // tc,),
        in_specs=[
            tok(d), tok(GLA_QK), tok(GLA_QK), tok(GLA_QK),
            pl.BlockSpec((batch, tc // GLA_CHUNK, 1, GLA_QK), lambda j: (0, j, 0, 0)),
            tok(GLA_V), tok(GLA_V), tok(GDN_QK), tok(GDN_QK), tok(GDN_V), tok(LANES), tok(GDN_V),
            _const_spec((EVEN_OUT, d)), _const_spec((1, GLA_DV)), _const_spec((1, GDN_DV)), _const_spec((1, d)),
        ],
        out_specs=tok(d),
        out_shape=jax.ShapeDtypeStruct((batch, seq, d), F32),
        scratch_shapes=[
            pltpu.VMEM((batch, GLA_HEADS // 2, GLA_DV, LANES), F32),
            pltpu.VMEM((batch, GDN_HEADS, GDN_DK, GDN_DV), F32),
            pltpu.VMEM((batch * tc, EVEN_OUT), F32),
        ],
        compiler_params=pltpu.CompilerParams(dimension_semantics=("arbitrary",),
                                             vmem_limit_bytes=VMEM_LIMIT),
        name="even_chunk",
    )(x2.reshape(batch, seq, d), qe, kt, kd, al, gv, sgg, dq, dk, dv, gb, sdz, w_out.astype(BF16),
      _row(gla_norm), _row(gdn_norm), _row(nw))
    return out.reshape(m, d)


def _odd_prep_kernel(tiles_per_seq,
                     x_ref, nw_ref, wp_ref, wl_ref, mu_ref, w0_ref, w2_ref, a0_ref, a2_ref, g2_ref,
                     kk_ref, ka_ref, rk_ref, ones_ref, cw_ref, cb_ref, wa_ref, ba_ref, wx_ref, bx_ref, lam_ref,
                     rt_ref, kt_ref, bt_ref, kh_ref, v_ref, wc_ref, g_ref, bonus_ref, la_ref, lb_ref, gly_ref,
                     carry_p, carry_l):
    first = (pl.program_id(0) % tiles_per_seq) == 0
    tm = x_ref.shape[0]
    h = _rms(x_ref[...], nw_ref[...]).astype(BF16)

    ps = jnp.dot(h, wp_ref[...], preferred_element_type=F32)
    carry = jnp.where(first, 0.0, carry_p[...])
    prev = _shift_rows(ps, carry, 1)
    carry_p[...] = ps[tm - SUBLANES:tm, :]
    ps = ps + (prev - ps) * mu_ref[...]
    r = ps[:, 0:RWKV_W]
    k = ps[:, RWKV_W:2 * RWKV_W]
    v = ps[:, 2 * RWKV_W:3 * RWKV_W]
    lora = ps[:, 3 * RWKV_W:3 * RWKV_W + LANES]
    gl = ps[:, 3 * RWKV_W + LANES:]
    logw = -(RWKV_DECAY_SCALE * _sigmoid(w0_ref[...] + _mm(jnp.tanh(lora), w2_ref[...])))
    a = _sigmoid(a0_ref[...] + _mm(lora, a2_ref[...]))
    g_ref[...] = _mm(_sigmoid(gl), g2_ref[...])
    ones_bd = ones_ref[...]
    kk = k * kk_ref[...]
    kk = kk * lax.rsqrt(_group_sum(kk * kk, ones_bd) + NORM_EPS)
    k = k * (1.0 + (a - 1.0) * ka_ref[...])
    bonus_ref[...] = _group_sum(r * k * rk_ref[...], ones_bd) * v
    gcum = _seg_cumsum(logw, RWKV_CHUNK)
    for c in range(tm // RWKV_CHUNK):
        wc_ref[c] = jnp.exp(gcum[(c + 1) * RWKV_CHUNK - 1:(c + 1) * RWKV_CHUNK, :])
    e_neg = jnp.exp(-gcum)
    rt_ref[...] = r * jnp.exp(gcum)
    kt_ref[...] = k * e_neg
    bt_ref[...] = kk * a * e_neg
    kh_ref[...] = kk * jnp.exp(gcum - logw)
    v_ref[...] = v

    pq = jnp.dot(h, wl_ref[...], preferred_element_type=F32)
    lx = pq[:, 0:LRU_WIDTH]
    carry2 = jnp.where(first, 0.0, carry_l[...])
    xb = lx * cw_ref[CONV_W - 1:CONV_W, :] + cb_ref[...]
    for s in range(1, CONV_W):
        xb = xb + _shift_rows(lx, carry2, s) * cw_ref[CONV_W - 1 - s:CONV_W - s, :]
    carry_l[...] = lx[tm - SUBLANES:tm, :]
    gate_r = _sigmoid(_mm(xb, wa_ref[...]) + ba_ref[...])
    gate_i = _sigmoid(_mm(xb, wx_ref[...]) + bx_ref[...])
    log_a = -LRU_C * gate_r * _softplus(-lam_ref[...])
    mult = jnp.sqrt(jnp.maximum(-jnp.tanh(log_a) * (jnp.exp(2.0 * log_a) + 1.0), 0.0))
    la_ref[...] = jnp.exp(log_a)
    lb_ref[...] = mult * gate_i * xb
    gly_ref[...] = _gelu_tanh(pq[:, LRU_WIDTH:])


def _block_diag(w):
    nb, n, _ = w.shape
    eye = jnp.eye(nb, dtype=w.dtype)
    return (eye[:, None, :, None] * w[:, :, None, :]).reshape(nb * n, nb * n)


def _odd_prep(x2, seq, nw, w_in, mu, w0, w2, a0, a2, g2, k_k, k_a, r_k, ones_bd,
              conv_w, conv_b, wa, ba, wx, bx, lam):
    m, d = x2.shape
    tm = min(TILE_PREP, seq)
    w_p = w_in[:, 0:RWKV_SHIFT]
    w_l = w_in[:, RWKV_SHIFT:]
    w2p = jnp.concatenate([w2, jnp.zeros((RWKV_A_LORA, RWKV_W), F32)], axis=0)
    a2p = jnp.concatenate([jnp.zeros((RWKV_W_LORA, RWKV_W), F32), a2], axis=0)

    def tok(n):
        return pl.BlockSpec((tm, n), lambda i: (i, 0))

    wspec = pl.BlockSpec((tm // RWKV_CHUNK, 1, RWKV_W), lambda i: (i, 0, 0))
    out_specs = [tok(RWKV_W)] * 5 + [wspec] + [tok(RWKV_W)] * 2 + [tok(LRU_WIDTH)] * 3
    out_shape = ([jax.ShapeDtypeStruct((m, RWKV_W), F32)] * 5
                 + [jax.ShapeDtypeStruct((m // RWKV_CHUNK, 1, RWKV_W), F32)]
                 + [jax.ShapeDtypeStruct((m, RWKV_W), F32)] * 2
                 + [jax.ShapeDtypeStruct((m, LRU_WIDTH), F32)] * 3)
    vec = _const_spec((1, RWKV_W))
    sq = _const_spec((RWKV_W, RWKV_W))
    return pl.pallas_call(
        functools.partial(_odd_prep_kernel, seq // tm),
        grid=(m // tm,),
        in_specs=[
            tok(d), _const_spec((1, d)), _const_spec(w_p.shape), _const_spec(w_l.shape),
            _const_spec((1, RWKV_SHIFT)), vec, _const_spec(w2p.shape), vec, _const_spec(a2p.shape),
            _const_spec(g2.shape), vec, vec, vec, sq,
            _const_spec(conv_w.shape), vec, sq, vec, sq, vec, vec,
        ],
        out_specs=out_specs,
        out_shape=out_shape,
        scratch_shapes=[pltpu.VMEM((SUBLANES, RWKV_SHIFT), F32), pltpu.VMEM((SUBLANES, LRU_WIDTH), F32)],
        compiler_params=pltpu.CompilerParams(dimension_semantics=("arbitrary",),
                                             vmem_limit_bytes=VMEM_LIMIT),
        name="odd_prep",
    )(x2, _row(nw), w_p.astype(BF16), w_l.astype(BF16), _row(mu), _row(w0), w2p.astype(BF16), _row(a0),
      a2p.astype(BF16), g2.astype(BF16), _row(k_k), _row(k_a), _row(r_k), ones_bd,
      conv_w.astype(F32), _row(conv_b), _block_diag(wa).astype(BF16), _row(ba),
      _block_diag(wx).astype(BF16), _row(bx), _row(lam))


def _odd_chunk_kernel(x_ref, rt_ref, kt_ref, bt_ref, kh_ref, v_ref, wc_ref, g_ref, bonus_ref, la_ref, lb_ref,
                      gly_ref, wo_ref, lnw_ref, lnb_ref, ones_ref, nw_ref,
                      o_ref, s_rwkv, h_lru, y_scr):
    nb, tc, d = x_ref.shape

    @pl.when(pl.program_id(0) == 0)
    def _():
        s_rwkv[...] = jnp.zeros(s_rwkv.shape, F32)
        h_lru[...] = jnp.zeros(h_lru.shape, F32)

    cs = RWKV_CHUNK
    r_i = lax.broadcasted_iota(jnp.int32, (cs, 2 * cs), 0)
    c_i = lax.broadcasted_iota(jnp.int32, (cs, 2 * cs), 1) & (cs - 1)
    f_incl = c_i <= r_i
    f_strict = c_i < r_i
    masks = _head_masks(RWKV_N)
    stack = lambda z: _unfold(z, masks)

    def body(c, carry):
        r0 = pl.multiple_of(c * cs, cs)
        rows = pl.ds(r0, cs)
        probs = [(b, p) for b in range(nb) for p in range(RWKV_HEADS // 2)]
        pls = [slice(p * LANES, (p + 1) * LANES) for _, p in probs]
        np_ = len(probs)
        ld = lambda ref: [ref[b, rows, pls[i]] for i, (b, _) in enumerate(probs)]
        rt, kt, bt, kh, vv = ld(rt_ref), ld(kt_ref), ld(bt_ref), ld(kh_ref), ld(v_ref)
        vs = [stack(z) for z in vv]
        ks = [stack(z) for z in kt]
        bs = [stack(z) for z in bt]
        mm = [_mm_nt(jnp.concatenate([kh[i], rt[i]], axis=0), jnp.concatenate([bs[i], ks[i]], axis=0))
              for i in range(np_)]
        m_rb = [jnp.where(f_incl, mm[i][cs:, 0:LANES], 0.0) for i in range(np_)]
        m_bk = [jnp.where(f_strict, mm[i][0:cs, LANES:], 0.0) for i in range(np_)]
        m_rk = [jnp.where(f_incl, mm[i][cs:, LANES:], 0.0) for i in range(np_)]
        t = _inv_unit_lower_folded([jnp.where(f_strict, mm[i][0:cs, 0:LANES], 0.0) for i in range(np_)], _parts1)
        x = [_mm(m_bk[i], vs[i]) for i in range(np_)]
        sol = [_mm(t[i], jnp.concatenate([stack(kh[i]), stack(x[i])], axis=1)) for i in range(np_)]
        sol_s = [jnp.concatenate([stack(z[:, 0:LANES]), stack(z[:, LANES:])], axis=1) for z in sol]
        corr = [_mm(m_rb[i], sol_s[i]) for i in range(np_)]
        y_loc = [_mm(m_rk[i], vs[i]) - corr[i][:, LANES:] for i in range(np_)]
        r_eff = [rt[i] - corr[i][:, 0:LANES] for i in range(np_)]
        vtk = [_mm(_fold(vs[i].T), ks[i]) for i in range(np_)]
        uv_t = [_fold(sol_s[i][:, LANES:].T) for i in range(np_)]
        st = [s_rwkv[b, p] for b, p in probs]
        ut = [_mm_nt(st[i], sol_s[i][:, 0:LANES]) + uv_t[i] for i in range(np_)]
        ys = [_mm_nt(r_eff[i], stack(st[i])) + y_loc[i] for i in range(np_)]
        upd = [_mm(ut[i], bs[i]) for i in range(np_)]
        for i, (b, p) in enumerate(probs):
            y_scr[pl.ds(b * tc + r0, cs), pls[i]] = ys[i]
            s_rwkv[b, p] = (st[i] + vtk[i] - upd[i]) * wc_ref[b, c][:, pls[i]]
        return carry

    lax.fori_loop(0, tc // RWKV_CHUNK, body, 0)

    rows_i = lax.broadcasted_iota(jnp.int32, (tc, LRU_WIDTH), 0) & (SUBLANES - 1)
    a_s, b_s = [], []
    for b in range(nb):
        a = la_ref[b]
        bb = lb_ref[b]
        dd = 1
        while dd < SUBLANES:
            keep = rows_i >= dd
            bb = jnp.where(keep, a * pltpu.roll(bb, dd, 0) + bb, bb)
            a = jnp.where(keep, a * pltpu.roll(a, dd, 0), a)
            dd *= 2
        a_s.append(a)
        b_s.append(bb)
    hrow = [h_lru[b, SUBLANES - 1:SUBLANES, :] for b in range(nb)]
    hgs = [[] for _ in range(nb)]
    for g in range(tc // SUBLANES):
        gs = slice(g * SUBLANES, (g + 1) * SUBLANES)
        for b in range(nb):
            hg = b_s[b][gs] + a_s[b][gs] * hrow[b]
            hrow[b] = hg[SUBLANES - 1:SUBLANES, :]
            hgs[b].append(hg)
    hls = []
    for b in range(nb):
        h_lru[b] = hgs[b][-1]
        hls.extend(hgs[b])
    y_lru = jnp.concatenate(hls, axis=0) * gly_ref[...].reshape(nb * tc, LRU_WIDTH)

    ones_bd = ones_ref[...]
    y = y_scr[...]
    mu = _group_sum(y, ones_bd) * (1.0 / RWKV_N)
    yc = y - mu
    var = _group_sum(yc * yc, ones_bd) * (1.0 / RWKV_N)
    yn = yc * lax.rsqrt(var + RWKV_GN_EPS) * lnw_ref[...] + lnb_ref[...]
    y_rwkv = (yn + bonus_ref[...].reshape(nb * tc, RWKV_W)) * g_ref[...].reshape(nb * tc, RWKV_W)
    out = (jnp.dot(y_rwkv.astype(BF16), wo_ref[0:RWKV_W, :], preferred_element_type=F32)
           + jnp.dot(y_lru.astype(BF16), wo_ref[RWKV_W:, :], preferred_element_type=F32))
    o_ref[...] = x_ref[...] + _rms(out, nw_ref[...]).reshape(nb, tc, d)


def _odd_chunk(x2, batch, seq, prep, w_out, ln_w, ln_b, ones_bd, nw):
    m, d = x2.shape
    tc = min(TILE_CHUNK, seq)
    rt, kt, bt, kh, v, wc, g, bonus, la, lb, gly = [a.reshape((batch, a.shape[0] // batch) + a.shape[1:])
                                                     for a in prep]

    def tok(n):
        return pl.BlockSpec((batch, tc, n), lambda j: (0, j, 0))

    vec = _const_spec((1, RWKV_W))
    out = pl.pallas_call(
        _odd_chunk_kernel,
        grid=(seq // tc,),
        in_specs=[
            tok(d), tok(RWKV_W), tok(RWKV_W), tok(RWKV_W), tok(RWKV_W), tok(RWKV_W),
            pl.BlockSpec((batch, tc // RWKV_CHUNK, 1, RWKV_W), lambda j: (0, j, 0, 0)),
            tok(RWKV_W), tok(RWKV_W), tok(LRU_WIDTH), tok(LRU_WIDTH), tok(LRU_WIDTH),
            _const_spec((ODD_OUT, d)), vec, vec, _const_spec((RWKV_W, RWKV_W)), _const_spec((1, d)),
        ],
        out_specs=tok(d),
        out_shape=jax.ShapeDtypeStruct((batch, seq, d), F32),
        scratch_shapes=[
            pltpu.VMEM((batch, RWKV_HEADS // 2, RWKV_N, LANES), F32),
            pltpu.VMEM((batch, SUBLANES, LRU_WIDTH), F32),
            pltpu.VMEM((batch * tc, RWKV_W), F32),
        ],
        compiler_params=pltpu.CompilerParams(dimension_semantics=("arbitrary",),
                                             vmem_limit_bytes=VMEM_LIMIT),
        name="odd_chunk",
    )(x2.reshape(batch, seq, d), rt, kt, bt, kh, v, wc, g, bonus, la, lb, gly, w_out.astype(BF16), _row(ln_w),
      _row(ln_b), ones_bd, _row(nw))
    return out.reshape(m, d)


def _even_layer(x2, batch, seq, nw, w_in, w_out, lora_w2, lora_b, gla_norm, conv_w, a_log, dt_bias, gdn_norm):
    prep = _even_prep(x2, seq, nw[2], w_in, lora_w2, lora_b, conv_w, a_log, dt_bias)
    return _even_chunk(x2, batch, seq, prep, w_out, gla_norm, gdn_norm, nw[3])


def _odd_layer(x2, batch, seq, nw, w_in, w_out, mu, w0, w2, a0, a2, g2, k_k, k_a, r_k, ln_w, ln_b,
               conv_w, conv_b, wa, ba, wx, bx, lam):
    grp = jnp.arange(RWKV_W) // RWKV_N
    ones_bd = (grp[:, None] == grp[None, :]).astype(BF16)
    prep = _odd_prep(x2, seq, nw[2], w_in, mu, w0, w2, a0, a2, g2, k_k, k_a, r_k, ones_bd,
                     conv_w, conv_b, wa, ba, wx, bx, lam)
    return _odd_chunk(x2, batch, seq, prep, w_out, ln_w, ln_b, ones_bd, nw[3])


def kernel(x, norm_w, ffn_w_gate, ffn_w_up, ffn_w_down, even_w_in, even_w_out, gla_lora_w2, gla_lora_b, gla_norm, gdn_conv, gdn_a_log, gdn_dt_bias, gdn_norm, odd_w_in, odd_w_out, rwkv_mu, rwkv_w0, rwkv_w2, rwkv_a0, rwkv_a2, rwkv_g2, rwkv_k_k, rwkv_k_a, rwkv_r_k, rwkv_ln_w, rwkv_ln_b, lru_conv_w, lru_conv_b, lru_wa, lru_ba, lru_wx, lru_bx, lru_lambda):
    batch, seq, d = x.shape
    depth = norm_w.shape[0]
    x2 = x.reshape(batch * seq, d)
    wg, wu, wd = ffn_w_gate.astype(BF16), ffn_w_up.astype(BF16), ffn_w_down.astype(BF16)
    for i in range(depth):
        j = i // 2
        nw = norm_w[i]
        x2 = _ffn(x2, nw[0], nw[1], wg, wu, wd, i, 0)
        if i % 2 == 0:
            x2 = _even_layer(x2, batch, seq, nw, even_w_in[j], even_w_out[j], gla_lora_w2[j], gla_lora_b[j],
                             gla_norm[j], gdn_conv[j], gdn_a_log[j], gdn_dt_bias[j], gdn_norm[j])
        else:
            x2 = _odd_layer(x2, batch, seq, nw, odd_w_in[j], odd_w_out[j], rwkv_mu[j], rwkv_w0[j], rwkv_w2[j],
                            rwkv_a0[j], rwkv_a2[j], rwkv_g2[j], rwkv_k_k[j], rwkv_k_a[j],
                            rwkv_r_k[j].reshape(-1), rwkv_ln_w[j], rwkv_ln_b[j], lru_conv_w[j], lru_conv_b[j],
                            lru_wa[j], lru_ba[j], lru_wx[j], lru_bx[j], lru_lambda[j])
        x2 = _ffn(x2, nw[4], nw[5], wg, wu, wd, i, 1)
    return x2.reshape(batch, seq, d)
```

```python
import functools

import jax
import jax.numpy as jnp
from jax import lax
from jax.experimental import pallas as pl
from jax.experimental.pallas import tpu as pltpu

F32 = jnp.float32
BF16 = jnp.bfloat16

D_MODEL = 1024
D_FF = 2816
FFN_RES = 0.5
NORM_EPS = 1e-6
CONV_W = 4

GLA_HEADS = 4
GLA_DK = 64
GLA_DV = 128
GLA_LORA = 16
GLA_GATE_NORM = 16.0
GLA_QK = GLA_HEADS * GLA_DK
GLA_V = GLA_HEADS * GLA_DV
GLA_CHUNK = 64

GDN_HEADS = 4
GDN_DK = 128
GDN_DV = 128
GDN_QK = GDN_HEADS * GDN_DK
GDN_V = GDN_HEADS * GDN_DV
GDN_CONV_CH = 2 * GDN_QK + GDN_V
GDN_CHUNK = 128

RWKV_HEADS = 8
RWKV_N = 64
RWKV_W = RWKV_HEADS * RWKV_N
RWKV_W_LORA = 64
RWKV_A_LORA = 64
RWKV_G_LORA = 128
RWKV_GN_EPS = 64e-5
RWKV_SHIFT = 3 * RWKV_W + RWKV_W_LORA + RWKV_A_LORA + RWKV_G_LORA
RWKV_CHUNK = 64
RWKV_DECAY_SCALE = 0.6065306597126334

LRU_WIDTH = 512
LRU_BLOCKS = 8
LRU_BW = LRU_WIDTH // LRU_BLOCKS
LRU_C = 8.0

EVEN_OUT = GLA_V + GDN_V
ODD_OUT = RWKV_W + LRU_WIDTH

LANES = 128
SUBLANES = 8
SMALL_GLR = 0
SMALL_DA = GLA_LORA
SMALL_DB = GLA_LORA + GDN_HEADS

EV_COLS = dict(qe=(0, GLA_QK), kt=(GLA_QK, GLA_QK), kd=(2 * GLA_QK, GLA_QK), gv=(3 * GLA_QK, GLA_V),
               sgg=(3 * GLA_QK + GLA_V, GLA_V), dq=(3 * GLA_QK + 2 * GLA_V, GDN_QK),
               dk=(3 * GLA_QK + 2 * GLA_V + GDN_QK, GDN_QK), dv=(3 * GLA_QK + 2 * GLA_V + 2 * GDN_QK, GDN_V),
               gb=(3 * GLA_QK + 2 * GLA_V + 2 * GDN_QK + GDN_V, LANES),
               sdz=(3 * GLA_QK + 2 * GLA_V + 2 * GDN_QK + GDN_V + LANES, GDN_V))
EV_WIDTH = 3 * GLA_QK + 2 * GLA_V + 2 * GDN_QK + 2 * GDN_V + LANES
OD_NAMES = ("rt", "kt", "bt", "kh", "v", "g", "bonus", "la", "lb", "gly")
OD_COLS = {name: (i * RWKV_W, RWKV_W) for i, name in enumerate(OD_NAMES)}
OD_WIDTH = len(OD_NAMES) * RWKV_W

TILE_FFN = 512
TILE_PREP = 512
TILE_CHUNK = 128
FF_CHUNK = 2816
VMEM_LIMIT = 56 * 1024 * 1024


def _mm(a, b):
    return jnp.dot(a.astype(BF16), b.astype(BF16), preferred_element_type=F32)


def _mm_nt(a, b):
    return lax.dot_general(a.astype(BF16), b.astype(BF16), (((1,), (1,)), ((), ())),
                           preferred_element_type=F32)


def _rms(x, w):
    return x * lax.rsqrt(jnp.mean(x * x, axis=-1, keepdims=True) + NORM_EPS) * w


def _sigmoid(x):
    return 0.5 * jnp.tanh(0.5 * x) + 0.5


def _silu(x):
    h = 0.5 * x
    return h + h * jnp.tanh(h)


def _softplus(x):
    return jnp.maximum(x, 0.0) + jnp.log(1.0 + jnp.exp(-jnp.abs(x)))


def _gelu_tanh(x):
    c = 0.7978845608028654
    return 0.5 * x * (1.0 + jnp.tanh(c * (x + 0.044715 * (x * x * x))))


def _seg_cumsum(x, seg):
    rows = lax.broadcasted_iota(jnp.int32, x.shape, 0) & (seg - 1)
    d = 1
    while d < seg:
        x = x + jnp.where(rows >= d, pltpu.roll(x, d, 0), 0.0)
        d *= 2
    return x


def _shift_rows(x, carry, s):
    sh = pltpu.roll(x, s, 0)
    c = pltpu.roll(carry, s, 0)
    rows = lax.broadcasted_iota(jnp.int32, c.shape, 0)
    head = jnp.where(rows < s, c, sh[0:SUBLANES])
    return jnp.concatenate([head, sh[SUBLANES:]], axis=0)


def _group_sum(x, ones_bd):
    hi = x.astype(BF16)
    lo = (x - hi.astype(F32)).astype(BF16)
    dot = functools.partial(jnp.dot, preferred_element_type=F32)
    return dot(hi, ones_bd) + dot(lo, ones_bd)


def _split(a):
    hi = a.astype(BF16)
    return hi, (a - hi.astype(F32)).astype(BF16)


def _parts1(a):
    return (a.astype(BF16),)


def _parts2(a):
    return _split(a)


def _mm_parts(ap, bp, nt=False):
    dims = (((1,), (1 if nt else 0,)), ((), ()))
    a2 = ap[0] if len(ap) == 1 else jnp.concatenate(ap, axis=1)
    bcat = bp[0] if len(bp) == 1 else jnp.concatenate(bp, axis=0 if nt else 1)
    rhs = bcat if len(ap) == 1 else jnp.concatenate([bcat] * len(ap), axis=1 if nt else 0)
    out = lax.dot_general(a2, rhs, dims, preferred_element_type=F32)
    if len(bp) == 1:
        return out
    n = out.shape[1] // 2
    return out[:, 0:n] + out[:, n:]


def _mm3(a, b, nt=False):
    return _mm_parts(_parts2(a), _parts2(b), nt)


def _head_masks(half):
    lane = lax.broadcasted_iota(jnp.int32, (1, 2 * half), 1)
    m0 = jnp.where(lane < half, 1.0, 0.0)
    return m0, 1.0 - m0


def _unfold(z, masks):
    return jnp.concatenate([z * masks[0], z * masks[1]], axis=0)


def _fold(z):
    c = z.shape[0] // 2
    return z[0:c] + z[c:]


def _inv_unit_lower_folded(lows_f, parts):
    c = lows_f[0].shape[0]
    r = lax.broadcasted_iota(jnp.int32, lows_f[0].shape, 0)
    col = lax.broadcasted_iota(jnp.int32, lows_f[0].shape, 1)
    eye = jnp.where(r == (col & (c - 1)), 1.0, 0.0)
    mb = (jnp.where(col < c, 1.0, 0.0).astype(BF16), jnp.where(col < c, 0.0, 1.0).astype(BF16))
    unf = lambda terms: tuple(jnp.concatenate([z * mb[0], z * mb[1]], axis=0) for z in terms)
    negs = [-low for low in lows_f]
    ts = [eye + p for p in negs]
    sp = [parts(p) for p in negs]
    ps = [_mm_parts(x, unf(x)) for x in sp]
    k = 4
    while k < c:
        sq = [parts(p) for p in ps]
        st = [parts(t) for t in ts]
        rhs = [tuple(jnp.concatenate([uq, ut], axis=1) for uq, ut in zip(unf(x), unf(y))) for x, y in zip(sq, st)]
        pq = [_mm_parts(x, y) for x, y in zip(sq, rhs)]
        ps = [z[:, 0:2 * c] for z in pq]
        ts = [t + z[:, 2 * c:] for t, z in zip(ts, pq)]
        k *= 2
    return [t + _mm_parts(parts(p), unf(parts(t))) for p, t in zip(ps, ts)]


def _const_spec(shape):
    nd = len(shape)
    return pl.BlockSpec(shape, lambda *_: (0,) * nd, pipeline_mode=pl.Buffered(1))


def _row(v):
    return v.reshape(1, -1).astype(F32)


def _col_views(ref, cols):
    lead = (slice(None),) * (len(ref.shape) - 1)
    return {name: ref.at[lead + (slice(off, off + n),)] for name, (off, n) in cols.items()}


def _ffn_kernel(x_ref, nw_ref, wg_ref, wu_ref, wd_ref, o_ref):
    x = x_ref[...]
    h = _rms(x, nw_ref[0:1, :]).astype(BF16)
    acc = jnp.zeros(x.shape, F32)
    for c in range(D_FF // FF_CHUNK):
        sl = slice(c * FF_CHUNK, (c + 1) * FF_CHUNK)
        g = jnp.dot(h, wg_ref[:, sl], preferred_element_type=F32)
        u = jnp.dot(h, wu_ref[:, sl], preferred_element_type=F32)
        a = (_silu(g) * u).astype(BF16)
        acc = acc + jnp.dot(a, wd_ref[sl, :], preferred_element_type=F32)
    o_ref[...] = x + FFN_RES * _rms(acc, nw_ref[1:2, :])


def _ffn(x2, nw_pre, nw_post, wg, wu, wd, layer, slot):
    m, d = x2.shape
    tm = min(TILE_FFN, m)
    nw = jnp.stack([nw_pre, nw_post]).astype(F32)

    def pick(rows, cols):
        return pl.BlockSpec((None, None, rows, cols), lambda i: (layer, slot, 0, 0), pipeline_mode=pl.Buffered(1))

    return pl.pallas_call(
        _ffn_kernel,
        grid=(m // tm,),
        in_specs=[
            pl.BlockSpec((tm, d), lambda i: (i, 0)),
            _const_spec((2, d)),
            pick(d, D_FF),
            pick(d, D_FF),
            pick(D_FF, d),
        ],
        out_specs=pl.BlockSpec((tm, d), lambda i: (i, 0)),
        out_shape=jax.ShapeDtypeStruct((m, d), F32),
        compiler_params=pltpu.CompilerParams(dimension_semantics=("arbitrary",),
                                             vmem_limit_bytes=VMEM_LIMIT),
        name="ffn",
    )(x2, nw, wg, wu, wd)


def _even_prep_kernel(tiles_per_seq,
                      x_ref, nw_ref, wa_ref, ws_ref, wc_ref, wz_ref, lw2_ref, lb_ref, conv_ref, hp_ref,
                      ev_ref, al_ref, carry_ref):
    v_ = _col_views(ev_ref, EV_COLS)
    qe_ref, kt_ref, kd_ref, gv_ref, sgg_ref = v_["qe"], v_["kt"], v_["kd"], v_["gv"], v_["sgg"]
    dq_ref, dk_ref, dv_ref, gb_ref, sdz_ref = v_["dq"], v_["dk"], v_["dv"], v_["gb"], v_["sdz"]
    first = (pl.program_id(0) % tiles_per_seq) == 0
    tm = x_ref.shape[0]
    h = _rms(x_ref[...], nw_ref[...]).astype(BF16)

    pa = jnp.dot(h, wa_ref[...], preferred_element_type=F32)
    ps = jnp.dot(h, ws_ref[...], preferred_element_type=F32)
    pc = jnp.dot(h, wc_ref[...], preferred_element_type=F32)
    pz = jnp.dot(h, wz_ref[...], preferred_element_type=F32)
    z = _mm(ps, lw2_ref[...]) + lb_ref[...]
    log_a = -_softplus(-z) * (1.0 / GLA_GATE_NORM)
    b = _seg_cumsum(log_a, GLA_CHUNK)
    gk = pa[:, GLA_QK:2 * GLA_QK]
    qe_ref[...] = pa[:, 0:GLA_QK] * (GLA_DK ** -0.5) * jnp.exp(b)
    kt_ref[...] = gk * jnp.exp(-b)
    for c in range(tm // GLA_CHUNK):
        cs = slice(c * GLA_CHUNK, (c + 1) * GLA_CHUNK)
        b_last = b[(c + 1) * GLA_CHUNK - 1:(c + 1) * GLA_CHUNK, :]
        al_ref[c] = jnp.exp(b_last)
        kd_ref[cs, :] = gk[cs] * jnp.exp(b_last - b[cs])
    gv_ref[...] = pa[:, 2 * GLA_QK:2 * GLA_QK + GLA_V]
    sgg_ref[...] = _silu(pa[:, 2 * GLA_QK + GLA_V:])

    carry = jnp.where(first, 0.0, carry_ref[...])
    conv = pc * conv_ref[CONV_W - 1:CONV_W, :]
    for s in range(1, CONV_W):
        conv = conv + _shift_rows(pc, carry, s) * conv_ref[CONV_W - 1 - s:CONV_W - s, :]
    carry_ref[...] = pc[tm - SUBLANES:tm, :]
    c = _silu(conv)
    for hd in range(GDN_HEADS):
        lo, hi = hd * GDN_DK, (hd + 1) * GDN_DK
        cq = c[:, lo:hi]
        ck = c[:, GDN_QK + lo:GDN_QK + hi]
        dq_ref[:, lo:hi] = cq * lax.rsqrt(jnp.sum(cq * cq, -1, keepdims=True) + NORM_EPS) * (GDN_DK ** -0.5)
        dk_ref[:, lo:hi] = ck * lax.rsqrt(jnp.sum(ck * ck, -1, keepdims=True) + NORM_EPS)
    dv_ref[...] = c[:, 2 * GDN_QK:]

    g = -jnp.exp(hp_ref[0:1, :]) * _softplus(ps + hp_ref[1:2, :])
    gc = _seg_cumsum(g, GDN_CHUNK)
    lane = lax.broadcasted_iota(jnp.int32, ps.shape, 1)
    is_g = (lane >= SMALL_DA) & (lane < SMALL_DB)
    gb_ref[...] = jnp.where(is_g, gc, _sigmoid(ps))

    sdz_ref[...] = _silu(pz)


def _even_prep(x2, seq, nw, w_in, lora_w2, lora_b, conv_w, a_log, dt_bias):
    m, d = x2.shape
    tm = min(TILE_PREP, seq)
    o = 0
    w_a = w_in[:, 0:2 * GLA_QK + 2 * GLA_V]
    o = 2 * GLA_QK + 2 * GLA_V
    w_glr = w_in[:, o:o + GLA_LORA]
    o += GLA_LORA
    w_c = w_in[:, o:o + GDN_CONV_CH]
    o += GDN_CONV_CH
    w_z = w_in[:, o:o + GDN_V]
    o += GDN_V
    w_da = w_in[:, o:o + GDN_HEADS]
    w_db = w_in[:, o + GDN_HEADS:o + 2 * GDN_HEADS]
    n_small = GLA_LORA + 2 * GDN_HEADS
    w_s = jnp.concatenate([w_glr, w_da, w_db, jnp.zeros((d, LANES - n_small), F32)], axis=1)
    lw2 = jnp.concatenate([lora_w2, jnp.zeros((LANES - GLA_LORA, GLA_QK), F32)], axis=0)
    pad_l = jnp.zeros((SMALL_DA,), F32)
    pad_r = jnp.zeros((LANES - SMALL_DB,), F32)
    hp = jnp.stack([jnp.concatenate([pad_l, a_log.astype(F32), pad_r]),
                    jnp.concatenate([pad_l, dt_bias.astype(F32), pad_r])])

    def tok(n):
        return pl.BlockSpec((tm, n), lambda i: (i, 0))

    out_specs = [tok(EV_WIDTH), pl.BlockSpec((tm // GLA_CHUNK, 1, GLA_QK), lambda i: (i, 0, 0))]
    out_shape = [jax.ShapeDtypeStruct((m, EV_WIDTH), F32), jax.ShapeDtypeStruct((m // GLA_CHUNK, 1, GLA_QK), F32)]
    return pl.pallas_call(
        functools.partial(_even_prep_kernel, seq // tm),
        grid=(m // tm,),
        in_specs=[
            tok(d), _const_spec((1, d)),
            _const_spec(w_a.shape), _const_spec(w_s.shape), _const_spec(w_c.shape), _const_spec(w_z.shape),
            _const_spec(lw2.shape), _const_spec((1, GLA_QK)), _const_spec(conv_w.shape), _const_spec(hp.shape),
        ],
        out_specs=out_specs,
        out_shape=out_shape,
        scratch_shapes=[pltpu.VMEM((SUBLANES, GDN_CONV_CH), F32)],
        compiler_params=pltpu.CompilerParams(dimension_semantics=("arbitrary",),
                                             vmem_limit_bytes=VMEM_LIMIT),
        name="even_prep",
    )(x2, _row(nw), w_a.astype(BF16), w_s.astype(BF16), w_c.astype(BF16), w_z.astype(BF16),
      lw2.astype(BF16), _row(lora_b), conv_w.astype(F32), hp)


def _even_chunk_kernel(x_ref, ev_ref, al_ref, wo_ref, gn_ref, dn_ref, nw_ref,
                       o_ref, s_gla, s_gdn, o_scr):
    v_ = _col_views(ev_ref, EV_COLS)
    qe_ref, kt_ref, kd_ref, gv_ref, sgg_ref = v_["qe"], v_["kt"], v_["kd"], v_["gv"], v_["sgg"]
    dq_ref, dk_ref, dv_ref, gb_ref, sdz_ref = v_["dq"], v_["dk"], v_["dv"], v_["gb"], v_["sdz"]
    nb, tc, d = x_ref.shape

    @pl.when(pl.program_id(0) == 0)
    def _():
        s_gla[...] = jnp.zeros(s_gla.shape, F32)
        s_gdn[...] = jnp.zeros(s_gdn.shape, F32)

    n = GDN_CHUNK
    r_i = lax.broadcasted_iota(jnp.int32, (n, n), 0)
    c_i = lax.broadcasted_iota(jnp.int32, (n, n), 1)
    tril = c_i <= r_i
    strict = c_i < r_i
    same_half = (c_i >= n // 2) == (r_i >= n // 2)
    hmasks = _head_masks(n // 2)
    bd_incl = ((c_i & (GLA_CHUNK - 1)) <= (r_i & (GLA_CHUNK - 1))) & ((c_i >= GLA_CHUNK) == (r_i >= GLA_CHUNK))
    lane = lax.broadcasted_iota(jnp.int32, (1, LANES), 1)
    m0 = jnp.where(lane < GLA_DK, 1.0, 0.0)
    m1 = 1.0 - m0

    def body(s, carry):
        r0 = pl.multiple_of(s * n, n)
        rows = pl.ds(r0, n)
        probs = [(b, hd) for b in range(nb) for hd in range(GDN_HEADS)]
        np_ = len(probs)
        hls = [slice(hd * GDN_DK, (hd + 1) * GDN_DK) for _, hd in probs]
        ld = lambda ref: [ref[b, rows, hls[i]] for i, (b, _) in enumerate(probs)]
        q, k, v = ld(dq_ref), ld(dk_ref), ld(dv_ref)
        gbt = [gb_ref[b, rows, :] for b in range(nb)]
        gc = [jnp.broadcast_to(gbt[b][:, SMALL_DA + hd:SMALL_DA + hd + 1], (n, n)) for b, hd in probs]
        beta = [jnp.broadcast_to(gbt[b][:, SMALL_DB + hd:SMALL_DB + hd + 1], (n, n)) for b, hd in probs]
        decay = [jnp.where(tril, jnp.exp(jnp.where(tril, g - g.T, 0.0)), 0.0) for g in gc]
        kb = [k[i] * beta[i] for i in range(np_)]
        low = [jnp.where(strict, _mm3(kb[i], k[i], nt=True) * decay[i], 0.0) for i in range(np_)]
        att = [_mm_nt(q[i], k[i]) * decay[i] for i in range(np_)]
        low_bd = [jnp.where(same_half, z, 0.0) for z in low]
        t_bd = [_unfold(z, hmasks) for z in _inv_unit_lower_folded([_fold(z) for z in low_bd], _parts2)]
        w_off = [_mm3(t_bd[i][n // 2:], low[i] - low_bd[i]) for i in range(np_)]
        w_off = [_mm3(w_off[i], t_bd[i]) for i in range(np_)]
        t = [jnp.concatenate([t_bd[i][0:n // 2], t_bd[i][n // 2:] - w_off[i]], axis=0) for i in range(np_)]
        eg = [jnp.exp(g) for g in gc]
        sol = [_mm3(t[i], jnp.concatenate([v[i] * beta[i], kb[i] * eg[i]], axis=1)) for i in range(np_)]
        g_last = [g[n - 1:n, :] for g in gc]
        k_dec_t = [(k[i] * jnp.exp(g_last[i] - gc[i])).T for i in range(np_)]
        st = [s_gdn[b, hd] for b, hd in probs]
        v_new = [sol[i][:, 0:GDN_DV] - _mm3(sol[i][:, GDN_DV:], st[i]) for i in range(np_)]
        o = [_mm(q[i] * eg[i], st[i]) + _mm(att[i], v_new[i]) for i in range(np_)]
        upd = [_mm3(k_dec_t[i], v_new[i]) for i in range(np_)]
        for i, (b, hd) in enumerate(probs):
            o_scr[pl.ds(b * tc + r0, n), GLA_V + hd * GDN_DV:GLA_V + (hd + 1) * GDN_DV] = o[i]
            s_gdn[b, hd] = st[i] * jnp.exp(g_last[i]) + upd[i]

        gprobs = [(b, p) for b in range(nb) for p in range(GLA_HEADS // 2)]
        ng = len(gprobs)
        pls = [slice(p * LANES, (p + 1) * LANES) for _, p in gprobs]
        for cc in range(n // GLA_CHUNK):
            crow = pl.ds(r0 + cc * GLA_CHUNK, GLA_CHUNK)
            ldg = lambda ref: [ref[b, crow, pls[i]] for i, (b, _) in enumerate(gprobs)]
            qe, kt, kd = ldg(qe_ref), ldg(kt_ref), ldg(kd_ref)
            vs = [jnp.concatenate([gv_ref[b, crow, (2 * p) * GLA_DV:(2 * p + 1) * GLA_DV],
                                   gv_ref[b, crow, (2 * p + 1) * GLA_DV:(2 * p + 2) * GLA_DV]], axis=0)
                  for b, p in gprobs]
            qs = [jnp.concatenate([z * m0, z * m1], axis=0) for z in qe]
            ks = [jnp.concatenate([z * m0, z * m1], axis=0) for z in kd]
            att = [jnp.where(bd_incl, _mm_nt(qs[i], jnp.concatenate([kt[i], kt[i]], axis=0)), 0.0)
                   for i in range(ng)]
            sg = [s_gla[b, p] for b, p in gprobs]
            og = [_mm(att[i], vs[i]) + _mm_nt(qs[i], sg[i]) for i in range(ng)]
            ug = [_mm(vs[i].T, ks[i]) for i in range(ng)]
            for i, (b, p) in enumerate(gprobs):
                ocrow = pl.ds(b * tc + r0 + cc * GLA_CHUNK, GLA_CHUNK)
                al = al_ref[b, s * (n // GLA_CHUNK) + cc][:, pls[i]]
                s_gla[b, p] = sg[i] * al + ug[i]
                o_scr[ocrow, (2 * p) * GLA_DV:(2 * p + 1) * GLA_DV] = og[i][0:GLA_CHUNK]
                o_scr[ocrow, (2 * p + 1) * GLA_DV:(2 * p + 2) * GLA_DV] = og[i][GLA_CHUNK:]
        return carry

    lax.fori_loop(0, tc // n, body, 0)

    sgg = sgg_ref[...].reshape(nb * tc, GLA_V)
    sdz = sdz_ref[...].reshape(nb * tc, GDN_V)
    for hd in range(GLA_HEADS + GDN_HEADS):
        hl = slice(hd * LANES, (hd + 1) * LANES)
        oh = o_scr[:, hl]
        nw = gn_ref[...] if hd < GLA_HEADS else dn_ref[...]
        gate = sgg[:, hl] if hd < GLA_HEADS else sdz[:, (hd - GLA_HEADS) * LANES:(hd - GLA_HEADS + 1) * LANES]
        o_scr[:, hl] = oh * lax.rsqrt(jnp.mean(oh * oh, -1, keepdims=True) + NORM_EPS) * nw * gate
    y = jnp.dot(o_scr[...].astype(BF16), wo_ref[...], preferred_element_type=F32)
    o_ref[...] = x_ref[...] + _rms(y, nw_ref[...]).reshape(nb, tc, d)


def _even_chunk(x2, batch, seq, prep, w_out, gla_norm, gdn_norm, nw):
    m, d = x2.shape
    tc = min(TILE_CHUNK, seq)
    ev, al = [a.reshape((batch, a.shape[0] // batch) + a.shape[1:]) for a in prep]

    def tok(n):
        return pl.BlockSpec((batch, tc, n), lambda j: (0, j, 0))

    out = pl.pallas_call(
        _even_chunk_kernel,
        grid=(seq // tc,),
        in_specs=[
            tok(d), tok(EV_WIDTH), pl.BlockSpec((batch, tc // GLA_CHUNK, 1, GLA_QK), lambda j: (0, j, 0, 0)),
            _const_spec((EVEN_OUT, d)), _const_spec((1, GLA_DV)), _const_spec((1, GDN_DV)), _const_spec((1, d)),
        ],
        out_specs=tok(d),
        out_shape=jax.ShapeDtypeStruct((batch, seq, d), F32),
        scratch_shapes=[
            pltpu.VMEM((batch, GLA_HEADS // 2, GLA_DV, LANES), F32),
            pltpu.VMEM((batch, GDN_HEADS, GDN_DK, GDN_DV), F32),
            pltpu.VMEM((batch * tc, EVEN_OUT), F32),
        ],
        compiler_params=pltpu.CompilerParams(dimension_semantics=("arbitrary",),
                                             vmem_limit_bytes=VMEM_LIMIT),
        name="even_chunk",
    )(x2.reshape(batch, seq, d), ev, al, w_out.astype(BF16), _row(gla_norm), _row(gdn_norm), _row(nw))
    return out.reshape(m, d)


def _odd_prep_kernel(tiles_per_seq,
                     x_ref, nw_ref, wp_ref, wl_ref, mu_ref, w0_ref, w2_ref, a0_ref, a2_ref, g2_ref,
                     kk_ref, ka_ref, rk_ref, ones_ref, cw_ref, cb_ref, wa_ref, ba_ref, wx_ref, bx_ref, lam_ref,
                     od_ref, wc_ref, carry_p, carry_l):
    v_ = _col_views(od_ref, OD_COLS)
    rt_ref, kt_ref, bt_ref, kh_ref, v_ref = v_["rt"], v_["kt"], v_["bt"], v_["kh"], v_["v"]
    g_ref, bonus_ref, la_ref, lb_ref, gly_ref = v_["g"], v_["bonus"], v_["la"], v_["lb"], v_["gly"]
    first = (pl.program_id(0) % tiles_per_seq) == 0
    tm = x_ref.shape[0]
    h = _rms(x_ref[...], nw_ref[...]).astype(BF16)

    ps = jnp.dot(h, wp_ref[...], preferred_element_type=F32)
    carry = jnp.where(first, 0.0, carry_p[...])
    prev = _shift_rows(ps, carry, 1)
    carry_p[...] = ps[tm - SUBLANES:tm, :]
    ps = ps + (prev - ps) * mu_ref[...]
    r = ps[:, 0:RWKV_W]
    k = ps[:, RWKV_W:2 * RWKV_W]
    v = ps[:, 2 * RWKV_W:3 * RWKV_W]
    lora = ps[:, 3 * RWKV_W:3 * RWKV_W + LANES]
    gl = ps[:, 3 * RWKV_W + LANES:]
    logw = -(RWKV_DECAY_SCALE * _sigmoid(w0_ref[...] + _mm(jnp.tanh(lora), w2_ref[...])))
    a = _sigmoid(a0_ref[...] + _mm(lora, a2_ref[...]))
    g_ref[...] = _mm(_sigmoid(gl), g2_ref[...])
    ones_bd = ones_ref[...]
    kk = k * kk_ref[...]
    kk = kk * lax.rsqrt(_group_sum(kk * kk, ones_bd) + NORM_EPS)
    k = k * (1.0 + (a - 1.0) * ka_ref[...])
    bonus_ref[...] = _group_sum(r * k * rk_ref[...], ones_bd) * v
    gcum = _seg_cumsum(logw, RWKV_CHUNK)
    for c in range(tm // RWKV_CHUNK):
        wc_ref[c] = jnp.exp(gcum[(c + 1) * RWKV_CHUNK - 1:(c + 1) * RWKV_CHUNK, :])
    e_neg = jnp.exp(-gcum)
    rt_ref[...] = r * jnp.exp(gcum)
    kt_ref[...] = k * e_neg
    bt_ref[...] = kk * a * e_neg
    kh_ref[...] = kk * jnp.exp(gcum - logw)
    v_ref[...] = v

    pq = jnp.dot(h, wl_ref[...], preferred_element_type=F32)
    lx = pq[:, 0:LRU_WIDTH]
    carry2 = jnp.where(first, 0.0, carry_l[...])
    xb = lx * cw_ref[CONV_W - 1:CONV_W, :] + cb_ref[...]
    for s in range(1, CONV_W):
        xb = xb + _shift_rows(lx, carry2, s) * cw_ref[CONV_W - 1 - s:CONV_W - s, :]
    carry_l[...] = lx[tm - SUBLANES:tm, :]
    gate_r = _sigmoid(_mm(xb, wa_ref[...]) + ba_ref[...])
    gate_i = _sigmoid(_mm(xb, wx_ref[...]) + bx_ref[...])
    log_a = -LRU_C * gate_r * _softplus(-lam_ref[...])
    mult = jnp.sqrt(jnp.maximum(-jnp.tanh(log_a) * (jnp.exp(2.0 * log_a) + 1.0), 0.0))
    la_ref[...] = jnp.exp(log_a)
    lb_ref[...] = mult * gate_i * xb
    gly_ref[...] = _gelu_tanh(pq[:, LRU_WIDTH:])


def _block_diag(w):
    nb, n, _ = w.shape
    eye = jnp.eye(nb, dtype=w.dtype)
    return (eye[:, None, :, None] * w[:, :, None, :]).reshape(nb * n, nb * n)


def _odd_prep(x2, seq, nw, w_in, mu, w0, w2, a0, a2, g2, k_k, k_a, r_k, ones_bd,
              conv_w, conv_b, wa, ba, wx, bx, lam):
    m, d = x2.shape
    tm = min(TILE_PREP, seq)
    w_p = w_in[:, 0:RWKV_SHIFT]
    w_l = w_in[:, RWKV_SHIFT:]
    w2p = jnp.concatenate([w2, jnp.zeros((RWKV_A_LORA, RWKV_W), F32)], axis=0)
    a2p = jnp.concatenate([jnp.zeros((RWKV_W_LORA, RWKV_W), F32), a2], axis=0)

    def tok(n):
        return pl.BlockSpec((tm, n), lambda i: (i, 0))

    wspec = pl.BlockSpec((tm // RWKV_CHUNK, 1, RWKV_W), lambda i: (i, 0, 0))
    out_specs = [tok(OD_WIDTH), wspec]
    out_shape = [jax.ShapeDtypeStruct((m, OD_WIDTH), F32), jax.ShapeDtypeStruct((m // RWKV_CHUNK, 1, RWKV_W), F32)]
    vec = _const_spec((1, RWKV_W))
    sq = _const_spec((RWKV_W, RWKV_W))
    return pl.pallas_call(
        functools.partial(_odd_prep_kernel, seq // tm),
        grid=(m // tm,),
        in_specs=[
            tok(d), _const_spec((1, d)), _const_spec(w_p.shape), _const_spec(w_l.shape),
            _const_spec((1, RWKV_SHIFT)), vec, _const_spec(w2p.shape), vec, _const_spec(a2p.shape),
            _const_spec(g2.shape), vec, vec, vec, sq,
            _const_spec(conv_w.shape), vec, sq, vec, sq, vec, vec,
        ],
        out_specs=out_specs,
        out_shape=out_shape,
        scratch_shapes=[pltpu.VMEM((SUBLANES, RWKV_SHIFT), F32), pltpu.VMEM((SUBLANES, LRU_WIDTH), F32)],
        compiler_params=pltpu.CompilerParams(dimension_semantics=("arbitrary",),
                                             vmem_limit_bytes=VMEM_LIMIT),
        name="odd_prep",
    )(x2, _row(nw), w_p.astype(BF16), w_l.astype(BF16), _row(mu), _row(w0), w2p.astype(BF16), _row(a0),
      a2p.astype(BF16), g2.astype(BF16), _row(k_k), _row(k_a), _row(r_k), ones_bd,
      conv_w.astype(F32), _row(conv_b), _block_diag(wa).astype(BF16), _row(ba),
      _block_diag(wx).astype(BF16), _row(bx), _row(lam))


def _odd_chunk_kernel(x_ref, od_ref, wc_ref, wo_ref, lnw_ref, lnb_ref, ones_ref, nw_ref,
                      o_ref, s_rwkv, h_lru, y_scr):
    v_ = _col_views(od_ref, OD_COLS)
    rt_ref, kt_ref, bt_ref, kh_ref, v_ref = v_["rt"], v_["kt"], v_["bt"], v_["kh"], v_["v"]
    g_ref, bonus_ref, la_ref, lb_ref, gly_ref = v_["g"], v_["bonus"], v_["la"], v_["lb"], v_["gly"]
    nb, tc, d = x_ref.shape

    @pl.when(pl.program_id(0) == 0)
    def _():
        s_rwkv[...] = jnp.zeros(s_rwkv.shape, F32)
        h_lru[...] = jnp.zeros(h_lru.shape, F32)

    cs = RWKV_CHUNK
    r_i = lax.broadcasted_iota(jnp.int32, (cs, 2 * cs), 0)
    c_i = lax.broadcasted_iota(jnp.int32, (cs, 2 * cs), 1) & (cs - 1)
    f_incl = c_i <= r_i
    f_strict = c_i < r_i
    masks = _head_masks(RWKV_N)
    stack = lambda z: _unfold(z, masks)

    def body(c, carry):
        r0 = pl.multiple_of(c * cs, cs)
        rows = pl.ds(r0, cs)
        probs = [(b, p) for b in range(nb) for p in range(RWKV_HEADS // 2)]
        pls = [slice(p * LANES, (p + 1) * LANES) for _, p in probs]
        np_ = len(probs)
        ld = lambda ref: [ref[b, rows, pls[i]] for i, (b, _) in enumerate(probs)]
        rt, kt, bt, kh, vv = ld(rt_ref), ld(kt_ref), ld(bt_ref), ld(kh_ref), ld(v_ref)
        vs = [stack(z) for z in vv]
        ks = [stack(z) for z in kt]
        bs = [stack(z) for z in bt]
        mm = [_mm_nt(jnp.concatenate([kh[i], rt[i]], axis=0), jnp.concatenate([bs[i], ks[i]], axis=0))
              for i in range(np_)]
        m_rb = [jnp.where(f_incl, mm[i][cs:, 0:LANES], 0.0) for i in range(np_)]
        m_bk = [jnp.where(f_strict, mm[i][0:cs, LANES:], 0.0) for i in range(np_)]
        m_rk = [jnp.where(f_incl, mm[i][cs:, LANES:], 0.0) for i in range(np_)]
        t = _inv_unit_lower_folded([jnp.where(f_strict, mm[i][0:cs, 0:LANES], 0.0) for i in range(np_)], _parts1)
        x = [_mm(m_bk[i], vs[i]) for i in range(np_)]
        sol = [_mm(t[i], jnp.concatenate([stack(kh[i]), stack(x[i])], axis=1)) for i in range(np_)]
        sol_s = [jnp.concatenate([stack(z[:, 0:LANES]), stack(z[:, LANES:])], axis=1) for z in sol]
        corr = [_mm(m_rb[i], sol_s[i]) for i in range(np_)]
        y_loc = [_mm(m_rk[i], vs[i]) - corr[i][:, LANES:] for i in range(np_)]
        r_eff = [rt[i] - corr[i][:, 0:LANES] for i in range(np_)]
        vtk = [_mm(_fold(vs[i].T), ks[i]) for i in range(np_)]
        uv_t = [_fold(sol_s[i][:, LANES:].T) for i in range(np_)]
        st = [s_rwkv[b, p] for b, p in probs]
        ut = [_mm_nt(st[i], sol_s[i][:, 0:LANES]) + uv_t[i] for i in range(np_)]
        ys = [_mm_nt(r_eff[i], stack(st[i])) + y_loc[i] for i in range(np_)]
        upd = [_mm(ut[i], bs[i]) for i in range(np_)]
        for i, (b, p) in enumerate(probs):
            y_scr[pl.ds(b * tc + r0, cs), pls[i]] = ys[i]
            s_rwkv[b, p] = (st[i] + vtk[i] - upd[i]) * wc_ref[b, c][:, pls[i]]
        return carry

    lax.fori_loop(0, tc // RWKV_CHUNK, body, 0)

    rows_i = lax.broadcasted_iota(jnp.int32, (tc, LRU_WIDTH), 0) & (SUBLANES - 1)
    a_s, b_s = [], []
    for b in range(nb):
        a = la_ref[b]
        bb = lb_ref[b]
        dd = 1
        while dd < SUBLANES:
            keep = rows_i >= dd
            bb = jnp.where(keep, a * pltpu.roll(bb, dd, 0) + bb, bb)
            a = jnp.where(keep, a * pltpu.roll(a, dd, 0), a)
            dd *= 2
        a_s.append(a)
        b_s.append(bb)
    hrow = [h_lru[b, SUBLANES - 1:SUBLANES, :] for b in range(nb)]
    hgs = [[] for _ in range(nb)]
    for g in range(tc // SUBLANES):
        gs = slice(g * SUBLANES, (g + 1) * SUBLANES)
        for b in range(nb):
            hg = b_s[b][gs] + a_s[b][gs] * hrow[b]
            hrow[b] = hg[SUBLANES - 1:SUBLANES, :]
            hgs[b].append(hg)
    hls = []
    for b in range(nb):
        h_lru[b] = hgs[b][-1]
        hls.extend(hgs[b])
    y_lru = jnp.concatenate(hls, axis=0) * gly_ref[...].reshape(nb * tc, LRU_WIDTH)

    ones_bd = ones_ref[...]
    y = y_scr[...]
    mu = _group_sum(y, ones_bd) * (1.0 / RWKV_N)
    yc = y - mu
    var = _group_sum(yc * yc, ones_bd) * (1.0 / RWKV_N)
    yn = yc * lax.rsqrt(var + RWKV_GN_EPS) * lnw_ref[...] + lnb_ref[...]
    y_rwkv = (yn + bonus_ref[...].reshape(nb * tc, RWKV_W)) * g_ref[...].reshape(nb * tc, RWKV_W)
    out = (jnp.dot(y_rwkv.astype(BF16), wo_ref[0:RWKV_W, :], preferred_element_type=F32)
           + jnp.dot(y_lru.astype(BF16), wo_ref[RWKV_W:, :], preferred_element_type=F32))
    o_ref[...] = x_ref[...] + _rms(out, nw_ref[...]).reshape(nb, tc, d)


def _odd_chunk(x2, batch, seq, prep, w_out, ln_w, ln_b, ones_bd, nw):
    m, d = x2.shape
    tc = min(TILE_CHUNK, seq)
    od, wc = [a.reshape((batch, a.shape[0] // batch) + a.shape[1:]) for a in prep]

    def tok(n):
        return pl.BlockSpec((batch, tc, n), lambda j: (0, j, 0))

    vec = _const_spec((1, RWKV_W))
    out = pl.pallas_call(
        _odd_chunk_kernel,
        grid=(seq // tc,),
        in_specs=[
            tok(d), tok(OD_WIDTH), pl.BlockSpec((batch, tc // RWKV_CHUNK, 1, RWKV_W), lambda j: (0, j, 0, 0)),
            _const_spec((ODD_OUT, d)), vec, vec, _const_spec((RWKV_W, RWKV_W)), _const_spec((1, d)),
        ],
        out_specs=tok(d),
        out_shape=jax.ShapeDtypeStruct((batch, seq, d), F32),
        scratch_shapes=[
            pltpu.VMEM((batch, RWKV_HEADS // 2, RWKV_N, LANES), F32),
            pltpu.VMEM((batch, SUBLANES, LRU_WIDTH), F32),
            pltpu.VMEM((batch * tc, RWKV_W), F32),
        ],
        compiler_params=pltpu.CompilerParams(dimension_semantics=("arbitrary",),
                                             vmem_limit_bytes=VMEM_LIMIT),
        name="odd_chunk",
    )(x2.reshape(batch, seq, d), od, wc, w_out.astype(BF16), _row(ln_w), _row(ln_b), ones_bd, _row(nw))
    return out.reshape(m, d)


def _even_layer(x2, batch, seq, nw, w_in, w_out, lora_w2, lora_b, gla_norm, conv_w, a_log, dt_bias, gdn_norm):
    prep = _even_prep(x2, seq, nw[2], w_in, lora_w2, lora_b, conv_w, a_log, dt_bias)
    return _even_chunk(x2, batch, seq, prep, w_out, gla_norm, gdn_norm, nw[3])


def _odd_layer(x2, batch, seq, nw, w_in, w_out, mu, w0, w2, a0, a2, g2, k_k, k_a, r_k, ln_w, ln_b,
               conv_w, conv_b, wa, ba, wx, bx, lam):
    grp = jnp.arange(RWKV_W) // RWKV_N
    ones_bd = (grp[:, None] == grp[None, :]).astype(BF16)
    prep = _odd_prep(x2, seq, nw[2], w_in, mu, w0, w2, a0, a2, g2, k_k, k_a, r_k, ones_bd,
                     conv_w, conv_b, wa, ba, wx, bx, lam)
    return _odd_chunk(x2, batch, seq, prep, w_out, ln_w, ln_b, ones_bd, nw[3])


def kernel(x, norm_w, ffn_w_gate, ffn_w_up, ffn_w_down, even_w_in, even_w_out, gla_lora_w2, gla_lora_b, gla_norm, gdn_conv, gdn_a_log, gdn_dt_bias, gdn_norm, odd_w_in, odd_w_out, rwkv_mu, rwkv_w0, rwkv_w2, rwkv_a0, rwkv_a2, rwkv_g2, rwkv_k_k, rwkv_k_a, rwkv_r_k, rwkv_ln_w, rwkv_ln_b, lru_conv_w, lru_conv_b, lru_wa, lru_ba, lru_wx, lru_bx, lru_lambda):
    batch, seq, d = x.shape
    depth = norm_w.shape[0]
    x2 = x.reshape(batch * seq, d)
    wg, wu, wd = ffn_w_gate.astype(BF16), ffn_w_up.astype(BF16), ffn_w_down.astype(BF16)
    for i in range(depth):
        j = i // 2
        nw = norm_w[i]
        x2 = _ffn(x2, nw[0], nw[1], wg, wu, wd, i, 0)
        if i % 2 == 0:
            x2 = _even_layer(x2, batch, seq, nw, even_w_in[j], even_w_out[j], gla_lora_w2[j], gla_lora_b[j],
                             gla_norm[j], gdn_conv[j], gdn_a_log[j], gdn_dt_bias[j], gdn_norm[j])
        else:
            x2 = _odd_layer(x2, batch, seq, nw, odd_w_in[j], odd_w_out[j], rwkv_mu[j], rwkv_w0[j], rwkv_w2[j],
                            rwkv_a0[j], rwkv_a2[j], rwkv_g2[j], rwkv_k_k[j], rwkv_k_a[j],
                            rwkv_r_k[j].reshape(-1), rwkv_ln_w[j], rwkv_ln_b[j], lru_conv_w[j], lru_conv_b[j],
                            lru_wa[j], lru_ba[j], lru_wx[j], lru_bx[j], lru_lambda[j])
        x2 = _ffn(x2, nw[4], nw[5], wg, wu, wd, i, 1)
    return x2.reshape(batch, seq, d)
```

```python
import functools

import jax
import jax.numpy as jnp
from jax import lax
from jax.experimental import pallas as pl
from jax.experimental.pallas import tpu as pltpu

F32 = jnp.float32
BF16 = jnp.bfloat16

D_MODEL = 1024
D_FF = 2816
FFN_RES = 0.5
NORM_EPS = 1e-6
CONV_W = 4

GLA_HEADS = 4
GLA_DK = 64
GLA_DV = 128
GLA_LORA = 16
GLA_GATE_NORM = 16.0
GLA_QK = GLA_HEADS * GLA_DK
GLA_V = GLA_HEADS * GLA_DV
GLA_CHUNK = 64

GDN_HEADS = 4
GDN_DK = 128
GDN_DV = 128
GDN_QK = GDN_HEADS * GDN_DK
GDN_V = GDN_HEADS * GDN_DV
GDN_CONV_CH = 2 * GDN_QK + GDN_V
GDN_CHUNK = 128

RWKV_HEADS = 8
RWKV_N = 64
RWKV_W = RWKV_HEADS * RWKV_N
RWKV_W_LORA = 64
RWKV_A_LORA = 64
RWKV_G_LORA = 128
RWKV_GN_EPS = 64e-5
RWKV_SHIFT = 3 * RWKV_W + RWKV_W_LORA + RWKV_A_LORA + RWKV_G_LORA
RWKV_CHUNK = 64
RWKV_DECAY_SCALE = 0.6065306597126334

LRU_WIDTH = 512
LRU_BLOCKS = 8
LRU_BW = LRU_WIDTH // LRU_BLOCKS
LRU_C = 8.0

EVEN_OUT = GLA_V + GDN_V
ODD_OUT = RWKV_W + LRU_WIDTH

LANES = 128
SUBLANES = 8
SMALL_GLR = 0
SMALL_DA = GLA_LORA
SMALL_DB = GLA_LORA + GDN_HEADS

EV_COLS = dict(qe=(0, GLA_QK), kt=(GLA_QK, GLA_QK), kd=(2 * GLA_QK, GLA_QK), gv=(3 * GLA_QK, GLA_V),
               sgg=(3 * GLA_QK + GLA_V, GLA_V), dq=(3 * GLA_QK + 2 * GLA_V, GDN_QK),
               dk=(3 * GLA_QK + 2 * GLA_V + GDN_QK, GDN_QK), dv=(3 * GLA_QK + 2 * GLA_V + 2 * GDN_QK, GDN_V),
               gb=(3 * GLA_QK + 2 * GLA_V + 2 * GDN_QK + GDN_V, LANES),
               sdz=(3 * GLA_QK + 2 * GLA_V + 2 * GDN_QK + GDN_V + LANES, GDN_V))
EV_WIDTH = 3 * GLA_QK + 2 * GLA_V + 2 * GDN_QK + 2 * GDN_V + LANES
OD_NAMES = ("rt", "kt", "bt", "kh", "v", "g", "bonus", "la", "lb", "gly")
OD_COLS = {name: (i * RWKV_W, RWKV_W) for i, name in enumerate(OD_NAMES)}
OD_WIDTH = len(OD_NAMES) * RWKV_W

TILE_FFN = 1024
FFN_PARTS = 4
TILE_PREP = 512
TILE_CHUNK = 128
VMEM_LIMIT = 56 * 1024 * 1024


def _mm(a, b):
    return jnp.dot(a.astype(BF16), b.astype(BF16), preferred_element_type=F32)


def _mm_nt(a, b):
    return lax.dot_general(a.astype(BF16), b.astype(BF16), (((1,), (1,)), ((), ())),
                           preferred_element_type=F32)


def _rms(x, w):
    return x * lax.rsqrt(jnp.mean(x * x, axis=-1, keepdims=True) + NORM_EPS) * w


def _sigmoid(x):
    return 0.5 * jnp.tanh(0.5 * x) + 0.5


def _silu(x):
    h = 0.5 * x
    return h + h * jnp.tanh(h)


def _softplus(x):
    return jnp.maximum(x, 0.0) + jnp.log(1.0 + jnp.exp(-jnp.abs(x)))


def _gelu_tanh(x):
    c = 0.7978845608028654
    return 0.5 * x * (1.0 + jnp.tanh(c * (x + 0.044715 * (x * x * x))))


def _seg_cumsum(x, seg):
    rows = lax.broadcasted_iota(jnp.int32, x.shape, 0) & (seg - 1)
    d = 1
    while d < seg:
        x = x + jnp.where(rows >= d, pltpu.roll(x, d, 0), 0.0)
        d *= 2
    return x


def _shift_rows(x, carry, s):
    sh = pltpu.roll(x, s, 0)
    c = pltpu.roll(carry, s, 0)
    rows = lax.broadcasted_iota(jnp.int32, c.shape, 0)
    head = jnp.where(rows < s, c, sh[0:SUBLANES])
    return jnp.concatenate([head, sh[SUBLANES:]], axis=0)


def _group_sum(x, ones_bd):
    hi = x.astype(BF16)
    lo = (x - hi.astype(F32)).astype(BF16)
    dot = functools.partial(jnp.dot, preferred_element_type=F32)
    return dot(hi, ones_bd) + dot(lo, ones_bd)


def _split(a):
    hi = a.astype(BF16)
    return hi, (a - hi.astype(F32)).astype(BF16)


def _parts1(a):
    return (a.astype(BF16),)


def _parts2(a):
    return _split(a)


def _mm_parts(ap, bp, nt=False):
    dims = (((1,), (1 if nt else 0,)), ((), ()))
    a2 = ap[0] if len(ap) == 1 else jnp.concatenate(ap, axis=1)
    bcat = bp[0] if len(bp) == 1 else jnp.concatenate(bp, axis=0 if nt else 1)
    rhs = bcat if len(ap) == 1 else jnp.concatenate([bcat] * len(ap), axis=1 if nt else 0)
    out = lax.dot_general(a2, rhs, dims, preferred_element_type=F32)
    if len(bp) == 1:
        return out
    n = out.shape[1] // 2
    return out[:, 0:n] + out[:, n:]


def _mm3(a, b, nt=False):
    return _mm_parts(_parts2(a), _parts2(b), nt)


def _head_masks(half):
    lane = lax.broadcasted_iota(jnp.int32, (1, 2 * half), 1)
    m0 = jnp.where(lane < half, 1.0, 0.0)
    return m0, 1.0 - m0


def _unfold(z, masks):
    return jnp.concatenate([z * masks[0], z * masks[1]], axis=0)


def _fold(z):
    c = z.shape[0] // 2
    return z[0:c] + z[c:]


def _inv_unit_lower_folded(lows_f, parts):
    c = lows_f[0].shape[0]
    r = lax.broadcasted_iota(jnp.int32, lows_f[0].shape, 0)
    col = lax.broadcasted_iota(jnp.int32, lows_f[0].shape, 1)
    eye = jnp.where(r == (col & (c - 1)), 1.0, 0.0)
    mb = (jnp.where(col < c, 1.0, 0.0).astype(BF16), jnp.where(col < c, 0.0, 1.0).astype(BF16))
    unf = lambda terms: tuple(jnp.concatenate([z * mb[0], z * mb[1]], axis=0) for z in terms)
    negs = [-low for low in lows_f]
    ts = [eye + p for p in negs]
    sp = [parts(p) for p in negs]
    ps = [_mm_parts(x, unf(x)) for x in sp]
    k = 4
    while k < c:
        sq = [parts(p) for p in ps]
        st = [parts(t) for t in ts]
        rhs = [tuple(jnp.concatenate([uq, ut], axis=1) for uq, ut in zip(unf(x), unf(y))) for x, y in zip(sq, st)]
        pq = [_mm_parts(x, y) for x, y in zip(sq, rhs)]
        ps = [z[:, 0:2 * c] for z in pq]
        ts = [t + z[:, 2 * c:] for t, z in zip(ts, pq)]
        k *= 2
    return [t + _mm_parts(parts(p), unf(parts(t))) for p, t in zip(ps, ts)]


def _const_spec(shape):
    nd = len(shape)
    return pl.BlockSpec(shape, lambda *_: (0,) * nd, pipeline_mode=pl.Buffered(1))


def _row(v):
    return v.reshape(1, -1).astype(F32)


def _col_views(ref, cols):
    lead = (slice(None),) * (len(ref.shape) - 1)
    return {name: ref.at[lead + (slice(off, off + n),)] for name, (off, n) in cols.items()}


def _ffn_kernel(x_ref, nw_ref, wg_ref, wu_ref, wd_ref, o_ref):
    tm = x_ref.shape[0]
    rows = tm // FFN_PARTS
    dot = functools.partial(jnp.dot, preferred_element_type=F32)
    sl = [slice(i * rows, (i + 1) * rows) for i in range(FFN_PARTS)]
    x, h, g, u, a, y = ({} for _ in range(6))
    for t in range(FFN_PARTS + 2):
        if t < FFN_PARTS:
            x[t] = x_ref[sl[t], :]
            h[t] = _rms(x[t], nw_ref[0:1, :]).astype(BF16)
            g[t] = dot(h[t], wg_ref[...])
            u[t] = dot(h[t], wu_ref[...])
        if 1 <= t <= FFN_PARTS:
            a[t - 1] = (_silu(g[t - 1]) * u[t - 1]).astype(BF16)
            y[t - 1] = dot(a[t - 1], wd_ref[...])
        if t >= 2:
            o_ref[sl[t - 2], :] = x[t - 2] + FFN_RES * _rms(y[t - 2], nw_ref[1:2, :])


def _ffn(x2, nw_pre, nw_post, wg, wu, wd, layer, slot):
    m, d = x2.shape
    tm = min(TILE_FFN, m)
    nw = jnp.stack([nw_pre, nw_post]).astype(F32)

    def pick(rows, cols):
        return pl.BlockSpec((None, None, rows, cols), lambda i: (layer, slot, 0, 0), pipeline_mode=pl.Buffered(1))

    return pl.pallas_call(
        _ffn_kernel,
        grid=(m // tm,),
        in_specs=[
            pl.BlockSpec((tm, d), lambda i: (i, 0)),
            _const_spec((2, d)),
            pick(d, D_FF),
            pick(d, D_FF),
            pick(D_FF, d),
        ],
        out_specs=pl.BlockSpec((tm, d), lambda i: (i, 0)),
        out_shape=jax.ShapeDtypeStruct((m, d), F32),
        compiler_params=pltpu.CompilerParams(dimension_semantics=("arbitrary",),
                                             vmem_limit_bytes=VMEM_LIMIT),
        name="ffn",
    )(x2, nw, wg, wu, wd)


def _even_prep_kernel(tiles_per_seq,
                      x_ref, nw_ref, wa_ref, ws_ref, wc_ref, wz_ref, lw2_ref, lb_ref, conv_ref, hp_ref,
                      ev_ref, al_ref, carry_ref):
    v_ = _col_views(ev_ref, EV_COLS)
    qe_ref, kt_ref, kd_ref, gv_ref, sgg_ref = v_["qe"], v_["kt"], v_["kd"], v_["gv"], v_["sgg"]
    dq_ref, dk_ref, dv_ref, gb_ref, sdz_ref = v_["dq"], v_["dk"], v_["dv"], v_["gb"], v_["sdz"]
    first = (pl.program_id(0) % tiles_per_seq) == 0
    tm = x_ref.shape[0]
    h = _rms(x_ref[...], nw_ref[...]).astype(BF16)

    pa = jnp.dot(h, wa_ref[...], preferred_element_type=F32)
    ps = jnp.dot(h, ws_ref[...], preferred_element_type=F32)
    pc = jnp.dot(h, wc_ref[...], preferred_element_type=F32)
    pz = jnp.dot(h, wz_ref[...], preferred_element_type=F32)
    z = _mm(ps, lw2_ref[...]) + lb_ref[...]
    log_a = -_softplus(-z) * (1.0 / GLA_GATE_NORM)
    b = _seg_cumsum(log_a, GLA_CHUNK)
    gk = pa[:, GLA_QK:2 * GLA_QK]
    qe_ref[...] = pa[:, 0:GLA_QK] * (GLA_DK ** -0.5) * jnp.exp(b)
    kt_ref[...] = gk * jnp.exp(-b)
    for c in range(tm // GLA_CHUNK):
        cs = slice(c * GLA_CHUNK, (c + 1) * GLA_CHUNK)
        b_last = b[(c + 1) * GLA_CHUNK - 1:(c + 1) * GLA_CHUNK, :]
        al_ref[c] = jnp.exp(b_last)
        kd_ref[cs, :] = gk[cs] * jnp.exp(b_last - b[cs])
    gv_ref[...] = pa[:, 2 * GLA_QK:2 * GLA_QK + GLA_V]
    sgg_ref[...] = _silu(pa[:, 2 * GLA_QK + GLA_V:])

    carry = jnp.where(first, 0.0, carry_ref[...])
    conv = pc * conv_ref[CONV_W - 1:CONV_W, :]
    for s in range(1, CONV_W):
        conv = conv + _shift_rows(pc, carry, s) * conv_ref[CONV_W - 1 - s:CONV_W - s, :]
    carry_ref[...] = pc[tm - SUBLANES:tm, :]
    c = _silu(conv)
    for hd in range(GDN_HEADS):
        lo, hi = hd * GDN_DK, (hd + 1) * GDN_DK
        cq = c[:, lo:hi]
        ck = c[:, GDN_QK + lo:GDN_QK + hi]
        dq_ref[:, lo:hi] = cq * lax.rsqrt(jnp.sum(cq * cq, -1, keepdims=True) + NORM_EPS) * (GDN_DK ** -0.5)
        dk_ref[:, lo:hi] = ck * lax.rsqrt(jnp.sum(ck * ck, -1, keepdims=True) + NORM_EPS)
    dv_ref[...] = c[:, 2 * GDN_QK:]

    g = -jnp.exp(hp_ref[0:1, :]) * _softplus(ps + hp_ref[1:2, :])
    gc = _seg_cumsum(g, GDN_CHUNK)
    lane = lax.broadcasted_iota(jnp.int32, ps.shape, 1)
    is_g = (lane >= SMALL_DA) & (lane < SMALL_DB)
    gb_ref[...] = jnp.where(is_g, gc, _sigmoid(ps))

    sdz_ref[...] = _silu(pz)


def _even_prep(x2, seq, nw, w_in, lora_w2, lora_b, conv_w, a_log, dt_bias):
    m, d = x2.shape
    tm = min(TILE_PREP, seq)
    o = 0
    w_a = w_in[:, 0:2 * GLA_QK + 2 * GLA_V]
    o = 2 * GLA_QK + 2 * GLA_V
    w_glr = w_in[:, o:o + GLA_LORA]
    o += GLA_LORA
    w_c = w_in[:, o:o + GDN_CONV_CH]
    o += GDN_CONV_CH
    w_z = w_in[:, o:o + GDN_V]
    o += GDN_V
    w_da = w_in[:, o:o + GDN_HEADS]
    w_db = w_in[:, o + GDN_HEADS:o + 2 * GDN_HEADS]
    n_small = GLA_LORA + 2 * GDN_HEADS
    w_s = jnp.concatenate([w_glr, w_da, w_db, jnp.zeros((d, LANES - n_small), F32)], axis=1)
    lw2 = jnp.concatenate([lora_w2, jnp.zeros((LANES - GLA_LORA, GLA_QK), F32)], axis=0)
    pad_l = jnp.zeros((SMALL_DA,), F32)
    pad_r = jnp.zeros((LANES - SMALL_DB,), F32)
    hp = jnp.stack([jnp.concatenate([pad_l, a_log.astype(F32), pad_r]),
                    jnp.concatenate([pad_l, dt_bias.astype(F32), pad_r])])

    def tok(n):
        return pl.BlockSpec((tm, n), lambda i: (i, 0))

    out_specs = [tok(EV_WIDTH), pl.BlockSpec((tm // GLA_CHUNK, 1, GLA_QK), lambda i: (i, 0, 0))]
    out_shape = [jax.ShapeDtypeStruct((m, EV_WIDTH), F32), jax.ShapeDtypeStruct((m // GLA_CHUNK, 1, GLA_QK), F32)]
    return pl.pallas_call(
        functools.partial(_even_prep_kernel, seq // tm),
        grid=(m // tm,),
        in_specs=[
            tok(d), _const_spec((1, d)),
            _const_spec(w_a.shape), _const_spec(w_s.shape), _const_spec(w_c.shape), _const_spec(w_z.shape),
            _const_spec(lw2.shape), _const_spec((1, GLA_QK)), _const_spec(conv_w.shape), _const_spec(hp.shape),
        ],
        out_specs=out_specs,
        out_shape=out_shape,
        scratch_shapes=[pltpu.VMEM((SUBLANES, GDN_CONV_CH), F32)],
        compiler_params=pltpu.CompilerParams(dimension_semantics=("arbitrary",),
                                             vmem_limit_bytes=VMEM_LIMIT),
        name="even_prep",
    )(x2, _row(nw), w_a.astype(BF16), w_s.astype(BF16), w_c.astype(BF16), w_z.astype(BF16),
      lw2.astype(BF16), _row(lora_b), conv_w.astype(F32), hp)


def _even_chunk_kernel(x_ref, ev_ref, al_ref, wo_ref, gn_ref, dn_ref, nw_ref,
                       o_ref, s_gla, s_gdn, o_scr):
    v_ = _col_views(ev_ref, EV_COLS)
    qe_ref, kt_ref, kd_ref, gv_ref, sgg_ref = v_["qe"], v_["kt"], v_["kd"], v_["gv"], v_["sgg"]
    dq_ref, dk_ref, dv_ref, gb_ref, sdz_ref = v_["dq"], v_["dk"], v_["dv"], v_["gb"], v_["sdz"]
    nb, tc, d = x_ref.shape

    @pl.when(pl.program_id(0) == 0)
    def _():
        s_gla[...] = jnp.zeros(s_gla.shape, F32)
        s_gdn[...] = jnp.zeros(s_gdn.shape, F32)

    n = GDN_CHUNK
    r_i = lax.broadcasted_iota(jnp.int32, (n, n), 0)
    c_i = lax.broadcasted_iota(jnp.int32, (n, n), 1)
    tril = c_i <= r_i
    strict = c_i < r_i
    same_half = (c_i >= n // 2) == (r_i >= n // 2)
    hmasks = _head_masks(n // 2)
    bd_incl = ((c_i & (GLA_CHUNK - 1)) <= (r_i & (GLA_CHUNK - 1))) & ((c_i >= GLA_CHUNK) == (r_i >= GLA_CHUNK))
    lane = lax.broadcasted_iota(jnp.int32, (1, LANES), 1)
    m0 = jnp.where(lane < GLA_DK, 1.0, 0.0)
    m1 = 1.0 - m0

    def body(s, carry):
        r0 = pl.multiple_of(s * n, n)
        rows = pl.ds(r0, n)
        probs = [(b, hd) for b in range(nb) for hd in range(GDN_HEADS)]
        np_ = len(probs)
        hls = [slice(hd * GDN_DK, (hd + 1) * GDN_DK) for _, hd in probs]
        ld = lambda ref: [ref[b, rows, hls[i]] for i, (b, _) in enumerate(probs)]
        q, k, v = ld(dq_ref), ld(dk_ref), ld(dv_ref)
        gbt = [gb_ref[b, rows, :] for b in range(nb)]
        gc = [jnp.broadcast_to(gbt[b][:, SMALL_DA + hd:SMALL_DA + hd + 1], (n, n)) for b, hd in probs]
        beta = [jnp.broadcast_to(gbt[b][:, SMALL_DB + hd:SMALL_DB + hd + 1], (n, n)) for b, hd in probs]
        decay = [jnp.where(tril, jnp.exp(jnp.where(tril, g - g.T, 0.0)), 0.0) for g in gc]
        kb = [k[i] * beta[i] for i in range(np_)]
        low = [jnp.where(strict, _mm3(kb[i], k[i], nt=True) * decay[i], 0.0) for i in range(np_)]
        att = [_mm_nt(q[i], k[i]) * decay[i] for i in range(np_)]
        low_bd = [jnp.where(same_half, z, 0.0) for z in low]
        t_bd = [_unfold(z, hmasks) for z in _inv_unit_lower_folded([_fold(z) for z in low_bd], _parts2)]
        w_off = [_mm3(t_bd[i][n // 2:], low[i] - low_bd[i]) for i in range(np_)]
        w_off = [_mm3(w_off[i], t_bd[i]) for i in range(np_)]
        t = [jnp.concatenate([t_bd[i][0:n // 2], t_bd[i][n // 2:] - w_off[i]], axis=0) for i in range(np_)]
        eg = [jnp.exp(g) for g in gc]
        sol = [_mm3(t[i], jnp.concatenate([v[i] * beta[i], kb[i] * eg[i]], axis=1)) for i in range(np_)]
        g_last = [g[n - 1:n, :] for g in gc]
        k_dec_t = [(k[i] * jnp.exp(g_last[i] - gc[i])).T for i in range(np_)]
        st = [s_gdn[b, hd] for b, hd in probs]
        v_new = [sol[i][:, 0:GDN_DV] - _mm3(sol[i][:, GDN_DV:], st[i]) for i in range(np_)]
        o = [_mm(q[i] * eg[i], st[i]) + _mm(att[i], v_new[i]) for i in range(np_)]
        upd = [_mm3(k_dec_t[i], v_new[i]) for i in range(np_)]
        for i, (b, hd) in enumerate(probs):
            o_scr[pl.ds(b * tc + r0, n), GLA_V + hd * GDN_DV:GLA_V + (hd + 1) * GDN_DV] = o[i]
            s_gdn[b, hd] = st[i] * jnp.exp(g_last[i]) + upd[i]

        gprobs = [(b, p) for b in range(nb) for p in range(GLA_HEADS // 2)]
        ng = len(gprobs)
        pls = [slice(p * LANES, (p + 1) * LANES) for _, p in gprobs]
        for cc in range(n // GLA_CHUNK):
            crow = pl.ds(r0 + cc * GLA_CHUNK, GLA_CHUNK)
            ldg = lambda ref: [ref[b, crow, pls[i]] for i, (b, _) in enumerate(gprobs)]
            qe, kt, kd = ldg(qe_ref), ldg(kt_ref), ldg(kd_ref)
            vs = [jnp.concatenate([gv_ref[b, crow, (2 * p) * GLA_DV:(2 * p + 1) * GLA_DV],
                                   gv_ref[b, crow, (2 * p + 1) * GLA_DV:(2 * p + 2) * GLA_DV]], axis=0)
                  for b, p in gprobs]
            qs = [jnp.concatenate([z * m0, z * m1], axis=0) for z in qe]
            ks = [jnp.concatenate([z * m0, z * m1], axis=0) for z in kd]
            att = [jnp.where(bd_incl, _mm_nt(qs[i], jnp.concatenate([kt[i], kt[i]], axis=0)), 0.0)
                   for i in range(ng)]
            sg = [s_gla[b, p] for b, p in gprobs]
            og = [_mm(att[i], vs[i]) + _mm_nt(qs[i], sg[i]) for i in range(ng)]
            ug = [_mm(vs[i].T, ks[i]) for i in range(ng)]
            for i, (b, p) in enumerate(gprobs):
                ocrow = pl.ds(b * tc + r0 + cc * GLA_CHUNK, GLA_CHUNK)
                al = al_ref[b, s * (n // GLA_CHUNK) + cc][:, pls[i]]
                s_gla[b, p] = sg[i] * al + ug[i]
                o_scr[ocrow, (2 * p) * GLA_DV:(2 * p + 1) * GLA_DV] = og[i][0:GLA_CHUNK]
                o_scr[ocrow, (2 * p + 1) * GLA_DV:(2 * p + 2) * GLA_DV] = og[i][GLA_CHUNK:]
        return carry

    lax.fori_loop(0, tc // n, body, 0)

    sgg = sgg_ref[...].reshape(nb * tc, GLA_V)
    sdz = sdz_ref[...].reshape(nb * tc, GDN_V)
    for hd in range(GLA_HEADS + GDN_HEADS):
        hl = slice(hd * LANES, (hd + 1) * LANES)
        oh = o_scr[:, hl]
        nw = gn_ref[...] if hd < GLA_HEADS else dn_ref[...]
        gate = sgg[:, hl] if hd < GLA_HEADS else sdz[:, (hd - GLA_HEADS) * LANES:(hd - GLA_HEADS + 1) * LANES]
        o_scr[:, hl] = oh * lax.rsqrt(jnp.mean(oh * oh, -1, keepdims=True) + NORM_EPS) * nw * gate
    y = jnp.dot(o_scr[...].astype(BF16), wo_ref[...], preferred_element_type=F32)
    o_ref[...] = x_ref[...] + _rms(y, nw_ref[...]).reshape(nb, tc, d)


def _even_chunk(x2, batch, seq, prep, w_out, gla_norm, gdn_norm, nw):
    m, d = x2.shape
    tc = min(TILE_CHUNK, seq)
    ev, al = [a.reshape((batch, a.shape[0] // batch) + a.shape[1:]) for a in prep]

    def tok(n):
        return pl.BlockSpec((batch, tc, n), lambda j: (0, j, 0))

    out = pl.pallas_call(
        _even_chunk_kernel,
        grid=(seq // tc,),
        in_specs=[
            tok(d), tok(EV_WIDTH), pl.BlockSpec((batch, tc // GLA_CHUNK, 1, GLA_QK), lambda j: (0, j, 0, 0)),
            _const_spec((EVEN_OUT, d)), _const_spec((1, GLA_DV)), _const_spec((1, GDN_DV)), _const_spec((1, d)),
        ],
        out_specs=tok(d),
        out_shape=jax.ShapeDtypeStruct((batch, seq, d), F32),
        scratch_shapes=[
            pltpu.VMEM((batch, GLA_HEADS // 2, GLA_DV, LANES), F32),
            pltpu.VMEM((batch, GDN_HEADS, GDN_DK, GDN_DV), F32),
            pltpu.VMEM((batch * tc, EVEN_OUT), F32),
        ],
        compiler_params=pltpu.CompilerParams(dimension_semantics=("arbitrary",),
                                             vmem_limit_bytes=VMEM_LIMIT),
        name="even_chunk",
    )(x2.reshape(batch, seq, d), ev, al, w_out.astype(BF16), _row(gla_norm), _row(gdn_norm), _row(nw))
    return out.reshape(m, d)


def _odd_prep_kernel(tiles_per_seq,
                     x_ref, nw_ref, wp_ref, wl_ref, mu_ref, w0_ref, w2_ref, a0_ref, a2_ref, g2_ref,
                     kk_ref, ka_ref, rk_ref, ones_ref, cw_ref, cb_ref, wa_ref, ba_ref, wx_ref, bx_ref, lam_ref,
                     od_ref, wc_ref, carry_p, carry_l):
    v_ = _col_views(od_ref, OD_COLS)
    rt_ref, kt_ref, bt_ref, kh_ref, v_ref = v_["rt"], v_["kt"], v_["bt"], v_["kh"], v_["v"]
    g_ref, bonus_ref, la_ref, lb_ref, gly_ref = v_["g"], v_["bonus"], v_["la"], v_["lb"], v_["gly"]
    first = (pl.program_id(0) % tiles_per_seq) == 0
    tm = x_ref.shape[0]
    h = _rms(x_ref[...], nw_ref[...]).astype(BF16)

    ps = jnp.dot(h, wp_ref[...], preferred_element_type=F32)
    carry = jnp.where(first, 0.0, carry_p[...])
    prev = _shift_rows(ps, carry, 1)
    carry_p[...] = ps[tm - SUBLANES:tm, :]
    ps = ps + (prev - ps) * mu_ref[...]
    r = ps[:, 0:RWKV_W]
    k = ps[:, RWKV_W:2 * RWKV_W]
    v = ps[:, 2 * RWKV_W:3 * RWKV_W]
    lora = ps[:, 3 * RWKV_W:3 * RWKV_W + LANES]
    gl = ps[:, 3 * RWKV_W + LANES:]
    logw = -(RWKV_DECAY_SCALE * _sigmoid(w0_ref[...] + _mm(jnp.tanh(lora), w2_ref[...])))
    a = _sigmoid(a0_ref[...] + _mm(lora, a2_ref[...]))
    g_ref[...] = _mm(_sigmoid(gl), g2_ref[...])
    ones_bd = ones_ref[...]
    kk = k * kk_ref[...]
    kk = kk * lax.rsqrt(_group_sum(kk * kk, ones_bd) + NORM_EPS)
    k = k * (1.0 + (a - 1.0) * ka_ref[...])
    bonus_ref[...] = _group_sum(r * k * rk_ref[...], ones_bd) * v
    gcum = _seg_cumsum(logw, RWKV_CHUNK)
    for c in range(tm // RWKV_CHUNK):
        wc_ref[c] = jnp.exp(gcum[(c + 1) * RWKV_CHUNK - 1:(c + 1) * RWKV_CHUNK, :])
    e_neg = jnp.exp(-gcum)
    rt_ref[...] = r * jnp.exp(gcum)
    kt_ref[...] = k * e_neg
    bt_ref[...] = kk * a * e_neg
    kh_ref[...] = kk * jnp.exp(gcum - logw)
    v_ref[...] = v

    pq = jnp.dot(h, wl_ref[...], preferred_element_type=F32)
    lx = pq[:, 0:LRU_WIDTH]
    carry2 = jnp.where(first, 0.0, carry_l[...])
    xb = lx * cw_ref[CONV_W - 1:CONV_W, :] + cb_ref[...]
    for s in range(1, CONV_W):
        xb = xb + _shift_rows(lx, carry2, s) * cw_ref[CONV_W - 1 - s:CONV_W - s, :]
    carry_l[...] = lx[tm - SUBLANES:tm, :]
    gate_r = _sigmoid(_mm(xb, wa_ref[...]) + ba_ref[...])
    gate_i = _sigmoid(_mm(xb, wx_ref[...]) + bx_ref[...])
    log_a = -LRU_C * gate_r * _softplus(-lam_ref[...])
    mult = jnp.sqrt(jnp.maximum(-jnp.tanh(log_a) * (jnp.exp(2.0 * log_a) + 1.0), 0.0))
    la_ref[...] = jnp.exp(log_a)
    lb_ref[...] = mult * gate_i * xb
    gly_ref[...] = _gelu_tanh(pq[:, LRU_WIDTH:])


def _block_diag(w):
    nb, n, _ = w.shape
    eye = jnp.eye(nb, dtype=w.dtype)
    return (eye[:, None, :, None] * w[:, :, None, :]).reshape(nb * n, nb * n)


def _odd_prep(x2, seq, nw, w_in, mu, w0, w2, a0, a2, g2, k_k, k_a, r_k, ones_bd,
              conv_w, conv_b, wa, ba, wx, bx, lam):
    m, d = x2.shape
    tm = min(TILE_PREP, seq)
    w_p = w_in[:, 0:RWKV_SHIFT]
    w_l = w_in[:, RWKV_SHIFT:]
    w2p = jnp.concatenate([w2, jnp.zeros((RWKV_A_LORA, RWKV_W), F32)], axis=0)
    a2p = jnp.concatenate([jnp.zeros((RWKV_W_LORA, RWKV_W), F32), a2], axis=0)

    def tok(n):
        return pl.BlockSpec((tm, n), lambda i: (i, 0))

    wspec = pl.BlockSpec((tm // RWKV_CHUNK, 1, RWKV_W), lambda i: (i, 0, 0))
    out_specs = [tok(OD_WIDTH), wspec]
    out_shape = [jax.ShapeDtypeStruct((m, OD_WIDTH), F32), jax.ShapeDtypeStruct((m // RWKV_CHUNK, 1, RWKV_W), F32)]
    vec = _const_spec((1, RWKV_W))
    sq = _const_spec((RWKV_W, RWKV_W))
    return pl.pallas_call(
        functools.partial(_odd_prep_kernel, seq // tm),
        grid=(m // tm,),
        in_specs=[
            tok(d), _const_spec((1, d)), _const_spec(w_p.shape), _const_spec(w_l.shape),
            _const_spec((1, RWKV_SHIFT)), vec, _const_spec(w2p.shape), vec, _const_spec(a2p.shape),
            _const_spec(g2.shape), vec, vec, vec, sq,
            _const_spec(conv_w.shape), vec, sq, vec, sq, vec, vec,
        ],
        out_specs=out_specs,
        out_shape=out_shape,
        scratch_shapes=[pltpu.VMEM((SUBLANES, RWKV_SHIFT), F32), pltpu.VMEM((SUBLANES, LRU_WIDTH), F32)],
        compiler_params=pltpu.CompilerParams(dimension_semantics=("arbitrary",),
                                             vmem_limit_bytes=VMEM_LIMIT),
        name="odd_prep",
    )(x2, _row(nw), w_p.astype(BF16), w_l.astype(BF16), _row(mu), _row(w0), w2p.astype(BF16), _row(a0),
      a2p.astype(BF16), g2.astype(BF16), _row(k_k), _row(k_a), _row(r_k), ones_bd,
      conv_w.astype(F32), _row(conv_b), _block_diag(wa).astype(BF16), _row(ba),
      _block_diag(wx).astype(BF16), _row(bx), _row(lam))


def _odd_chunk_kernel(x_ref, od_ref, wc_ref, wo_ref, lnw_ref, lnb_ref, ones_ref, nw_ref,
                      o_ref, s_rwkv, h_lru, y_scr):
    v_ = _col_views(od_ref, OD_COLS)
    rt_ref, kt_ref, bt_ref, kh_ref, v_ref = v_["rt"], v_["kt"], v_["bt"], v_["kh"], v_["v"]
    g_ref, bonus_ref, la_ref, lb_ref, gly_ref = v_["g"], v_["bonus"], v_["la"], v_["lb"], v_["gly"]
    nb, tc, d = x_ref.shape

    @pl.when(pl.program_id(0) == 0)
    def _():
        s_rwkv[...] = jnp.zeros(s_rwkv.shape, F32)
        h_lru[...] = jnp.zeros(h_lru.shape, F32)

    cs = RWKV_CHUNK
    r_i = lax.broadcasted_iota(jnp.int32, (cs, 2 * cs), 0)
    c_i = lax.broadcasted_iota(jnp.int32, (cs, 2 * cs), 1) & (cs - 1)
    f_incl = c_i <= r_i
    f_strict = c_i < r_i
    masks = _head_masks(RWKV_N)
    stack = lambda z: _unfold(z, masks)

    def body(c, carry):
        r0 = pl.multiple_of(c * cs, cs)
        rows = pl.ds(r0, cs)
        probs = [(b, p) for b in range(nb) for p in range(RWKV_HEADS // 2)]
        pls = [slice(p * LANES, (p + 1) * LANES) for _, p in probs]
        np_ = len(probs)
        ld = lambda ref: [ref[b, rows, pls[i]] for i, (b, _) in enumerate(probs)]
        rt, kt, bt, kh, vv = ld(rt_ref), ld(kt_ref), ld(bt_ref), ld(kh_ref), ld(v_ref)
        vs = [stack(z) for z in vv]
        ks = [stack(z) for z in kt]
        bs = [stack(z) for z in bt]
        mm = [_mm_nt(jnp.concatenate([kh[i], rt[i]], axis=0), jnp.concatenate([bs[i], ks[i]], axis=0))
              for i in range(np_)]
        m_rb = [jnp.where(f_incl, mm[i][cs:, 0:LANES], 0.0) for i in range(np_)]
        m_bk = [jnp.where(f_strict, mm[i][0:cs, LANES:], 0.0) for i in range(np_)]
        m_rk = [jnp.where(f_incl, mm[i][cs:, LANES:], 0.0) for i in range(np_)]
        t = _inv_unit_lower_folded([jnp.where(f_strict, mm[i][0:cs, 0:LANES], 0.0) for i in range(np_)], _parts1)
        x = [_mm(m_bk[i], vs[i]) for i in range(np_)]
        sol = [_mm(t[i], jnp.concatenate([stack(kh[i]), stack(x[i])], axis=1)) for i in range(np_)]
        sol_s = [jnp.concatenate([stack(z[:, 0:LANES]), stack(z[:, LANES:])], axis=1) for z in sol]
        corr = [_mm(m_rb[i], sol_s[i]) for i in range(np_)]
        y_loc = [_mm(m_rk[i], vs[i]) - corr[i][:, LANES:] for i in range(np_)]
        r_eff = [rt[i] - corr[i][:, 0:LANES] for i in range(np_)]
        vtk = [_mm(_fold(vs[i].T), ks[i]) for i in range(np_)]
        uv_t = [_fold(sol_s[i][:, LANES:].T) for i in range(np_)]
        st = [s_rwkv[b, p] for b, p in probs]
        ut = [_mm_nt(st[i], sol_s[i][:, 0:LANES]) + uv_t[i] for i in range(np_)]
        ys = [_mm_nt(r_eff[i], stack(st[i])) + y_loc[i] for i in range(np_)]
        upd = [_mm(ut[i], bs[i]) for i in range(np_)]
        for i, (b, p) in enumerate(probs):
            y_scr[pl.ds(b * tc + r0, cs), pls[i]] = ys[i]
            s_rwkv[b, p] = (st[i] + vtk[i] - upd[i]) * wc_ref[b, c][:, pls[i]]
        return carry

    lax.fori_loop(0, tc // RWKV_CHUNK, body, 0)

    rows_i = lax.broadcasted_iota(jnp.int32, (tc, LRU_WIDTH), 0) & (SUBLANES - 1)
    a_s, b_s = [], []
    for b in range(nb):
        a = la_ref[b]
        bb = lb_ref[b]
        dd = 1
        while dd < SUBLANES:
            keep = rows_i >= dd
            bb = jnp.where(keep, a * pltpu.roll(bb, dd, 0) + bb, bb)
            a = jnp.where(keep, a * pltpu.roll(a, dd, 0), a)
            dd *= 2
        a_s.append(a)
        b_s.append(bb)
    hrow = [h_lru[b, SUBLANES - 1:SUBLANES, :] for b in range(nb)]
    hgs = [[] for _ in range(nb)]
    for g in range(tc // SUBLANES):
        gs = slice(g * SUBLANES, (g + 1) * SUBLANES)
        for b in range(nb):
            hg = b_s[b][gs] + a_s[b][gs] * hrow[b]
            hrow[b] = hg[SUBLANES - 1:SUBLANES, :]
            hgs[b].append(hg)
    hls = []
    for b in range(nb):
        h_lru[b] = hgs[b][-1]
        hls.extend(hgs[b])
    y_lru = jnp.concatenate(hls, axis=0) * gly_ref[...].reshape(nb * tc, LRU_WIDTH)

    ones_bd = ones_ref[...]
    y = y_scr[...]
    mu = _group_sum(y, ones_bd) * (1.0 / RWKV_N)
    yc = y - mu
    var = _group_sum(yc * yc, ones_bd) * (1.0 / RWKV_N)
    yn = yc * lax.rsqrt(var + RWKV_GN_EPS) * lnw_ref[...] + lnb_ref[...]
    y_rwkv = (yn + bonus_ref[...].reshape(nb * tc, RWKV_W)) * g_ref[...].reshape(nb * tc, RWKV_W)
    out = (jnp.dot(y_rwkv.astype(BF16), wo_ref[0:RWKV_W, :], preferred_element_type=F32)
           + jnp.dot(y_lru.astype(BF16), wo_ref[RWKV_W:, :], preferred_element_type=F32))
    o_ref[...] = x_ref[...] + _rms(out, nw_ref[...]).reshape(nb, tc, d)


def _odd_chunk(x2, batch, seq, prep, w_out, ln_w, ln_b, ones_bd, nw):
    m, d = x2.shape
    tc = min(TILE_CHUNK, seq)
    od, wc = [a.reshape((batch, a.shape[0] // batch) + a.shape[1:]) for a in prep]

    def tok(n):
        return pl.BlockSpec((batch, tc, n), lambda j: (0, j, 0))

    vec = _const_spec((1, RWKV_W))
    out = pl.pallas_call(
        _odd_chunk_kernel,
        grid=(seq // tc,),
        in_specs=[
            tok(d), tok(OD_WIDTH), pl.BlockSpec((batch, tc // RWKV_CHUNK, 1, RWKV_W), lambda j: (0, j, 0, 0)),
            _const_spec((ODD_OUT, d)), vec, vec, _const_spec((RWKV_W, RWKV_W)), _const_spec((1, d)),
        ],
        out_specs=tok(d),
        out_shape=jax.ShapeDtypeStruct((batch, seq, d), F32),
        scratch_shapes=[
            pltpu.VMEM((batch, RWKV_HEADS // 2, RWKV_N, LANES), F32),
            pltpu.VMEM((batch, SUBLANES, LRU_WIDTH), F32),
            pltpu.VMEM((batch * tc, RWKV_W), F32),
        ],
        compiler_params=pltpu.CompilerParams(dimension_semantics=("arbitrary",),
                                             vmem_limit_bytes=VMEM_LIMIT),
        name="odd_chunk",
    )(x2.reshape(batch, seq, d), od, wc, w_out.astype(BF16), _row(ln_w), _row(ln_b), ones_bd, _row(nw))
    return out.reshape(m, d)


def _even_layer(x2, batch, seq, nw, w_in, w_out, lora_w2, lora_b, gla_norm, conv_w, a_log, dt_bias, gdn_norm):
    prep = _even_prep(x2, seq, nw[2], w_in, lora_w2, lora_b, conv_w, a_log, dt_bias)
    return _even_chunk(x2, batch, seq, prep, w_out, gla_norm, gdn_norm, nw[3])


def _odd_layer(x2, batch, seq, nw, w_in, w_out, mu, w0, w2, a0, a2, g2, k_k, k_a, r_k, ln_w, ln_b,
               conv_w, conv_b, wa, ba, wx, bx, lam):
    grp = jnp.arange(RWKV_W) // RWKV_N
    ones_bd = (grp[:, None] == grp[None, :]).astype(BF16)
    prep = _odd_prep(x2, seq, nw[2], w_in, mu, w0, w2, a0, a2, g2, k_k, k_a, r_k, ones_bd,
                     conv_w, conv_b, wa, ba, wx, bx, lam)
    return _odd_chunk(x2, batch, seq, prep, w_out, ln_w, ln_b, ones_bd, nw[3])


def kernel(x, norm_w, ffn_w_gate, ffn_w_up, ffn_w_down, even_w_in, even_w_out, gla_lora_w2, gla_lora_b, gla_norm, gdn_conv, gdn_a_log, gdn_dt_bias, gdn_norm, odd_w_in, odd_w_out, rwkv_mu, rwkv_w0, rwkv_w2, rwkv_a0, rwkv_a2, rwkv_g2, rwkv_k_k, rwkv_k_a, rwkv_r_k, rwkv_ln_w, rwkv_ln_b, lru_conv_w, lru_conv_b, lru_wa, lru_ba, lru_wx, lru_bx, lru_lambda):
    batch, seq, d = x.shape
    depth = norm_w.shape[0]
    x2 = x.reshape(batch * seq, d)
    wg, wu, wd = ffn_w_gate.astype(BF16), ffn_w_up.astype(BF16), ffn_w_down.astype(BF16)
    for i in range(depth):
        j = i // 2
        nw = norm_w[i]
        x2 = _ffn(x2, nw[0], nw[1], wg, wu, wd, i, 0)
        if i % 2 == 0:
            x2 = _even_layer(x2, batch, seq, nw, even_w_in[j], even_w_out[j], gla_lora_w2[j], gla_lora_b[j],
                             gla_norm[j], gdn_conv[j], gdn_a_log[j], gdn_dt_bias[j], gdn_norm[j])
        else:
            x2 = _odd_layer(x2, batch, seq, nw, odd_w_in[j], odd_w_out[j], rwkv_mu[j], rwkv_w0[j], rwkv_w2[j],
                            rwkv_a0[j], rwkv_a2[j], rwkv_g2[j], rwkv_k_k[j], rwkv_k_a[j],
                            rwkv_r_k[j].reshape(-1), rwkv_ln_w[j], rwkv_ln_b[j], lru_conv_w[j], lru_conv_b[j],
                            lru_wa[j], lru_ba[j], lru_wx[j], lru_bx[j], lru_lambda[j])
        x2 = _ffn(x2, nw[4], nw[5], wg, wu, wd, i, 1)
    return x2.reshape(batch, seq, d)
```

```python
import functools

import jax
import jax.numpy as jnp
from jax import lax
from jax.experimental import pallas as pl
from jax.experimental.pallas import tpu as pltpu

F32 = jnp.float32
BF16 = jnp.bfloat16

D_MODEL = 1024
D_FF = 2816
FFN_RES = 0.5
NORM_EPS = 1e-6
CONV_W = 4

GLA_HEADS = 4
GLA_DK = 64
GLA_DV = 128
GLA_LORA = 16
GLA_GATE_NORM = 16.0
GLA_QK = GLA_HEADS * GLA_DK
GLA_V = GLA_HEADS * GLA_DV
GLA_CHUNK = 64

GDN_HEADS = 4
GDN_DK = 128
GDN_DV = 128
GDN_QK = GDN_HEADS * GDN_DK
GDN_V = GDN_HEADS * GDN_DV
GDN_CONV_CH = 2 * GDN_QK + GDN_V
GDN_CHUNK = 128

RWKV_HEADS = 8
RWKV_N = 64
RWKV_W = RWKV_HEADS * RWKV_N
RWKV_W_LORA = 64
RWKV_A_LORA = 64
RWKV_G_LORA = 128
RWKV_GN_EPS = 64e-5
RWKV_SHIFT = 3 * RWKV_W + RWKV_W_LORA + RWKV_A_LORA + RWKV_G_LORA
RWKV_CHUNK = 64
RWKV_DECAY_SCALE = 0.6065306597126334

LRU_WIDTH = 512
LRU_BLOCKS = 8
LRU_BW = LRU_WIDTH // LRU_BLOCKS
LRU_C = 8.0

EVEN_OUT = GLA_V + GDN_V
ODD_OUT = RWKV_W + LRU_WIDTH

LANES = 128
SUBLANES = 8
SMALL_GLR = 0
SMALL_DA = GLA_LORA
SMALL_DB = GLA_LORA + GDN_HEADS

EV_COLS = dict(qe=(0, GLA_QK), kt=(GLA_QK, GLA_QK), kd=(2 * GLA_QK, GLA_QK), gv=(3 * GLA_QK, GLA_V),
               sgg=(3 * GLA_QK + GLA_V, GLA_V), dq=(3 * GLA_QK + 2 * GLA_V, GDN_QK),
               dk=(3 * GLA_QK + 2 * GLA_V + GDN_QK, GDN_QK), dv=(3 * GLA_QK + 2 * GLA_V + 2 * GDN_QK, GDN_V),
               gb=(3 * GLA_QK + 2 * GLA_V + 2 * GDN_QK + GDN_V, LANES),
               sdz=(3 * GLA_QK + 2 * GLA_V + 2 * GDN_QK + GDN_V + LANES, GDN_V))
EV_WIDTH = 3 * GLA_QK + 2 * GLA_V + 2 * GDN_QK + 2 * GDN_V + LANES
OD_NAMES = ("rt", "kt", "bt", "kh", "v", "g", "bonus", "la", "lb", "gly")
OD_COLS = {name: (i * RWKV_W, RWKV_W) for i, name in enumerate(OD_NAMES)}
OD_WIDTH = len(OD_NAMES) * RWKV_W

TILE_FFN = 1024
FFN_PARTS = 4
TILE_PREP = 512
TILE_CHUNK = 128
VMEM_LIMIT = 56 * 1024 * 1024


def _mm(a, b):
    return jnp.dot(a.astype(BF16), b.astype(BF16), preferred_element_type=F32)


def _mm_nt(a, b):
    return lax.dot_general(a.astype(BF16), b.astype(BF16), (((1,), (1,)), ((), ())),
                           preferred_element_type=F32)


def _rms(x, w):
    return x * lax.rsqrt(jnp.mean(x * x, axis=-1, keepdims=True) + NORM_EPS) * w


def _sigmoid(x):
    return 0.5 * jnp.tanh(0.5 * x) + 0.5


def _silu(x):
    h = 0.5 * x
    return h + h * jnp.tanh(h)


def _softplus(x):
    return jnp.maximum(x, 0.0) + jnp.log(1.0 + jnp.exp(-jnp.abs(x)))


def _gelu_tanh(x):
    c = 0.7978845608028654
    return 0.5 * x * (1.0 + jnp.tanh(c * (x + 0.044715 * (x * x * x))))


def _seg_cumsum(x, seg):
    rows = lax.broadcasted_iota(jnp.int32, x.shape, 0) & (seg - 1)
    d = 1
    while d < seg:
        x = x + jnp.where(rows >= d, pltpu.roll(x, d, 0), 0.0)
        d *= 2
    return x


def _shift_rows(x, carry, s):
    sh = pltpu.roll(x, s, 0)
    c = pltpu.roll(carry, s, 0)
    rows = lax.broadcasted_iota(jnp.int32, c.shape, 0)
    head = jnp.where(rows < s, c, sh[0:SUBLANES])
    return jnp.concatenate([head, sh[SUBLANES:]], axis=0)


def _group_sum(x, ones_bd):
    hi = x.astype(BF16)
    lo = (x - hi.astype(F32)).astype(BF16)
    dot = functools.partial(jnp.dot, preferred_element_type=F32)
    return dot(hi, ones_bd) + dot(lo, ones_bd)


def _split(a):
    hi = a.astype(BF16)
    return hi, (a - hi.astype(F32)).astype(BF16)


def _parts1(a):
    return (a.astype(BF16),)


def _parts2(a):
    return _split(a)


def _mm_parts(ap, bp, nt=False):
    dims = (((1,), (1 if nt else 0,)), ((), ()))
    a2 = ap[0] if len(ap) == 1 else jnp.concatenate(ap, axis=1)
    bcat = bp[0] if len(bp) == 1 else jnp.concatenate(bp, axis=0 if nt else 1)
    rhs = bcat if len(ap) == 1 else jnp.concatenate([bcat] * len(ap), axis=1 if nt else 0)
    out = lax.dot_general(a2, rhs, dims, preferred_element_type=F32)
    if len(bp) == 1:
        return out
    n = out.shape[1] // 2
    return out[:, 0:n] + out[:, n:]


def _mm3(a, b, nt=False):
    return _mm_parts(_parts2(a), _parts2(b), nt)


def _head_masks(half):
    lane = lax.broadcasted_iota(jnp.int32, (1, 2 * half), 1)
    m0 = jnp.where(lane < half, 1.0, 0.0)
    return m0, 1.0 - m0


def _unfold(z, masks):
    return jnp.concatenate([z * masks[0], z * masks[1]], axis=0)


def _fold(z):
    c = z.shape[0] // 2
    return z[0:c] + z[c:]


def _inv_unit_lower_folded(lows_f, parts):
    c = lows_f[0].shape[0]
    r = lax.broadcasted_iota(jnp.int32, lows_f[0].shape, 0)
    col = lax.broadcasted_iota(jnp.int32, lows_f[0].shape, 1)
    eye = jnp.where(r == (col & (c - 1)), 1.0, 0.0)
    mb = (jnp.where(col < c, 1.0, 0.0).astype(BF16), jnp.where(col < c, 0.0, 1.0).astype(BF16))
    unf = lambda terms: tuple(jnp.concatenate([z * mb[0], z * mb[1]], axis=0) for z in terms)
    negs = [-low for low in lows_f]
    ts = [eye + p for p in negs]
    sp = [parts(p) for p in negs]
    ps = [_mm_parts(x, unf(x)) for x in sp]
    k = 4
    while k < c:
        sq = [parts(p) for p in ps]
        st = [parts(t) for t in ts]
        rhs = [tuple(jnp.concatenate([uq, ut], axis=1) for uq, ut in zip(unf(x), unf(y))) for x, y in zip(sq, st)]
        pq = [_mm_parts(x, y) for x, y in zip(sq, rhs)]
        ps = [z[:, 0:2 * c] for z in pq]
        ts = [t + z[:, 2 * c:] for t, z in zip(ts, pq)]
        k *= 2
    return [t + _mm_parts(parts(p), unf(parts(t))) for p, t in zip(ps, ts)]


def _const_spec(shape):
    nd = len(shape)
    return pl.BlockSpec(shape, lambda *_: (0,) * nd, pipeline_mode=pl.Buffered(1))


def _row(v):
    return v.reshape(1, -1).astype(F32)


def _col_views(ref, cols):
    lead = (slice(None),) * (len(ref.shape) - 1)
    return {name: ref.at[lead + (slice(off, off + n),)] for name, (off, n) in cols.items()}


def _ffn_kernel(x_ref, nw_ref, wg_ref, wu_ref, wd_ref, o_ref):
    tm = x_ref.shape[0]
    rows = tm // FFN_PARTS
    dot = functools.partial(jnp.dot, preferred_element_type=F32)
    sl = [slice(i * rows, (i + 1) * rows) for i in range(FFN_PARTS)]
    x, h, g, u, a, y = ({} for _ in range(6))
    for t in range(FFN_PARTS + 2):
        if t < FFN_PARTS:
            x[t] = x_ref[sl[t], :]
            h[t] = _rms(x[t], nw_ref[0:1, :]).astype(BF16)
            g[t] = dot(h[t], wg_ref[...])
            u[t] = dot(h[t], wu_ref[...])
        if 1 <= t <= FFN_PARTS:
            a[t - 1] = (_silu(g[t - 1]) * u[t - 1]).astype(BF16)
            y[t - 1] = dot(a[t - 1], wd_ref[...])
        if t >= 2:
            o_ref[sl[t - 2], :] = x[t - 2] + FFN_RES * _rms(y[t - 2], nw_ref[1:2, :])


def _ffn(x2, nw_pre, nw_post, wg, wu, wd, layer, slot):
    m, d = x2.shape
    tm = min(TILE_FFN, m)
    nw = jnp.stack([nw_pre, nw_post]).astype(F32)

    def pick(rows, cols):
        return pl.BlockSpec((None, None, rows, cols), lambda i: (layer, slot, 0, 0), pipeline_mode=pl.Buffered(1))

    return pl.pallas_call(
        _ffn_kernel,
        grid=(m // tm,),
        in_specs=[
            pl.BlockSpec((tm, d), lambda i: (i, 0)),
            _const_spec((2, d)),
            pick(d, D_FF),
            pick(d, D_FF),
            pick(D_FF, d),
        ],
        out_specs=pl.BlockSpec((tm, d), lambda i: (i, 0)),
        out_shape=jax.ShapeDtypeStruct((m, d), F32),
        compiler_params=pltpu.CompilerParams(dimension_semantics=("arbitrary",),
                                             vmem_limit_bytes=VMEM_LIMIT),
        name="ffn",
    )(x2, nw, wg, wu, wd)


def _even_prep_kernel(tiles_per_seq,
                      x_ref, nw_ref, wa_ref, ws_ref, wc_ref, wz_ref, lw2_ref, lb_ref, conv_ref, hp_ref,
                      ev_ref, al_ref, carry_ref):
    v_ = _col_views(ev_ref, EV_COLS)
    qe_ref, kt_ref, kd_ref, gv_ref, sgg_ref = v_["qe"], v_["kt"], v_["kd"], v_["gv"], v_["sgg"]
    dq_ref, dk_ref, dv_ref, gb_ref, sdz_ref = v_["dq"], v_["dk"], v_["dv"], v_["gb"], v_["sdz"]
    first = (pl.program_id(0) % tiles_per_seq) == 0
    tm = x_ref.shape[0]
    h = _rms(x_ref[...], nw_ref[...]).astype(BF16)

    pa = jnp.dot(h, wa_ref[...], preferred_element_type=F32)
    ps = jnp.dot(h, ws_ref[...], preferred_element_type=F32)
    pc = jnp.dot(h, wc_ref[...], preferred_element_type=F32)
    pz = jnp.dot(h, wz_ref[...], preferred_element_type=F32)
    z = _mm(ps, lw2_ref[...]) + lb_ref[...]
    log_a = -_softplus(-z) * (1.0 / GLA_GATE_NORM)
    b = _seg_cumsum(log_a, GLA_CHUNK)
    gk = pa[:, GLA_QK:2 * GLA_QK]
    qe_ref[...] = pa[:, 0:GLA_QK] * (GLA_DK ** -0.5) * jnp.exp(b)
    kt_ref[...] = gk * jnp.exp(-b)
    for c in range(tm // GLA_CHUNK):
        cs = slice(c * GLA_CHUNK, (c + 1) * GLA_CHUNK)
        b_last = b[(c + 1) * GLA_CHUNK - 1:(c + 1) * GLA_CHUNK, :]
        al_ref[c] = jnp.exp(b_last)
        kd_ref[cs, :] = gk[cs] * jnp.exp(b_last - b[cs])
    gv_ref[...] = pa[:, 2 * GLA_QK:2 * GLA_QK + GLA_V]
    sgg_ref[...] = _silu(pa[:, 2 * GLA_QK + GLA_V:])

    carry = jnp.where(first, 0.0, carry_ref[...])
    conv = pc * conv_ref[CONV_W - 1:CONV_W, :]
    for s in range(1, CONV_W):
        conv = conv + _shift_rows(pc, carry, s) * conv_ref[CONV_W - 1 - s:CONV_W - s, :]
    carry_ref[...] = pc[tm - SUBLANES:tm, :]
    c = _silu(conv)
    for hd in range(GDN_HEADS):
        lo, hi = hd * GDN_DK, (hd + 1) * GDN_DK
        cq = c[:, lo:hi]
        ck = c[:, GDN_QK + lo:GDN_QK + hi]
        dq_ref[:, lo:hi] = cq * lax.rsqrt(jnp.sum(cq * cq, -1, keepdims=True) + NORM_EPS) * (GDN_DK ** -0.5)
        dk_ref[:, lo:hi] = ck * lax.rsqrt(jnp.sum(ck * ck, -1, keepdims=True) + NORM_EPS)
    dv_ref[...] = c[:, 2 * GDN_QK:]

    g = -jnp.exp(hp_ref[0:1, :]) * _softplus(ps + hp_ref[1:2, :])
    gc = _seg_cumsum(g, GDN_CHUNK)
    lane = lax.broadcasted_iota(jnp.int32, ps.shape, 1)
    is_g = (lane >= SMALL_DA) & (lane < SMALL_DB)
    gb_ref[...] = jnp.where(is_g, gc, _sigmoid(ps))

    sdz_ref[...] = _silu(pz)


def _even_prep(x2, seq, nw, w_in, lora_w2, lora_b, conv_w, a_log, dt_bias):
    m, d = x2.shape
    tm = min(TILE_PREP, seq)
    o = 0
    w_a = w_in[:, 0:2 * GLA_QK + 2 * GLA_V]
    o = 2 * GLA_QK + 2 * GLA_V
    w_glr = w_in[:, o:o + GLA_LORA]
    o += GLA_LORA
    w_c = w_in[:, o:o + GDN_CONV_CH]
    o += GDN_CONV_CH
    w_z = w_in[:, o:o + GDN_V]
    o += GDN_V
    w_da = w_in[:, o:o + GDN_HEADS]
    w_db = w_in[:, o + GDN_HEADS:o + 2 * GDN_HEADS]
    n_small = GLA_LORA + 2 * GDN_HEADS
    w_s = jnp.concatenate([w_glr, w_da, w_db, jnp.zeros((d, LANES - n_small), F32)], axis=1)
    lw2 = jnp.concatenate([lora_w2, jnp.zeros((LANES - GLA_LORA, GLA_QK), F32)], axis=0)
    pad_l = jnp.zeros((SMALL_DA,), F32)
    pad_r = jnp.zeros((LANES - SMALL_DB,), F32)
    hp = jnp.stack([jnp.concatenate([pad_l, a_log.astype(F32), pad_r]),
                    jnp.concatenate([pad_l, dt_bias.astype(F32), pad_r])])

    def tok(n):
        return pl.BlockSpec((tm, n), lambda i: (i, 0))

    out_specs = [tok(EV_WIDTH), pl.BlockSpec((tm // GLA_CHUNK, 1, GLA_QK), lambda i: (i, 0, 0))]
    out_shape = [jax.ShapeDtypeStruct((m, EV_WIDTH), F32), jax.ShapeDtypeStruct((m // GLA_CHUNK, 1, GLA_QK), F32)]
    return pl.pallas_call(
        functools.partial(_even_prep_kernel, seq // tm),
        grid=(m // tm,),
        in_specs=[
            tok(d), _const_spec((1, d)),
            _const_spec(w_a.shape), _const_spec(w_s.shape), _const_spec(w_c.shape), _const_spec(w_z.shape),
            _const_spec(lw2.shape), _const_spec((1, GLA_QK)), _const_spec(conv_w.shape), _const_spec(hp.shape),
        ],
        out_specs=out_specs,
        out_shape=out_shape,
        scratch_shapes=[pltpu.VMEM((SUBLANES, GDN_CONV_CH), F32)],
        compiler_params=pltpu.CompilerParams(dimension_semantics=("arbitrary",),
                                             vmem_limit_bytes=VMEM_LIMIT),
        name="even_prep",
    )(x2, _row(nw), w_a.astype(BF16), w_s.astype(BF16), w_c.astype(BF16), w_z.astype(BF16),
      lw2.astype(BF16), _row(lora_b), conv_w.astype(F32), hp)


def _even_chunk_kernel(x_ref, ev_ref, al_ref, wo_ref, gn_ref, dn_ref, nw_ref,
                       o_ref, s_gla, s_gdn, o_scr):
    v_ = _col_views(ev_ref, EV_COLS)
    qe_ref, kt_ref, kd_ref, gv_ref, sgg_ref = v_["qe"], v_["kt"], v_["kd"], v_["gv"], v_["sgg"]
    dq_ref, dk_ref, dv_ref, gb_ref, sdz_ref = v_["dq"], v_["dk"], v_["dv"], v_["gb"], v_["sdz"]
    nb, tc, d = x_ref.shape

    @pl.when(pl.program_id(0) == 0)
    def _():
        s_gla[...] = jnp.zeros(s_gla.shape, F32)
        s_gdn[...] = jnp.zeros(s_gdn.shape, F32)

    n = GDN_CHUNK
    r_i = lax.broadcasted_iota(jnp.int32, (n, n), 0)
    c_i = lax.broadcasted_iota(jnp.int32, (n, n), 1)
    tril = c_i <= r_i
    strict = c_i < r_i
    same_half = (c_i >= n // 2) == (r_i >= n // 2)
    hmasks = _head_masks(n // 2)
    bd_incl = ((c_i & (GLA_CHUNK - 1)) <= (r_i & (GLA_CHUNK - 1))) & ((c_i >= GLA_CHUNK) == (r_i >= GLA_CHUNK))
    lane = lax.broadcasted_iota(jnp.int32, (1, LANES), 1)
    m0 = jnp.where(lane < GLA_DK, 1.0, 0.0)
    m1 = 1.0 - m0

    def body(s, carry):
        r0 = pl.multiple_of(s * n, n)
        rows = pl.ds(r0, n)
        probs = [(b, hd) for b in range(nb) for hd in range(GDN_HEADS)]
        np_ = len(probs)
        hls = [slice(hd * GDN_DK, (hd + 1) * GDN_DK) for _, hd in probs]
        ld = lambda ref: [ref[b, rows, hls[i]] for i, (b, _) in enumerate(probs)]
        q, k, v = ld(dq_ref), ld(dk_ref), ld(dv_ref)
        gbt = [gb_ref[b, rows, :] for b in range(nb)]
        gc = [jnp.broadcast_to(gbt[b][:, SMALL_DA + hd:SMALL_DA + hd + 1], (n, n)) for b, hd in probs]
        beta = [jnp.broadcast_to(gbt[b][:, SMALL_DB + hd:SMALL_DB + hd + 1], (n, n)) for b, hd in probs]
        decay = [jnp.where(tril, jnp.exp(jnp.where(tril, g - g.T, 0.0)), 0.0) for g in gc]
        kb = [k[i] * beta[i] for i in range(np_)]
        low = [jnp.where(strict, _mm3(kb[i], k[i], nt=True) * decay[i], 0.0) for i in range(np_)]
        att = [_mm_nt(q[i], k[i]) * decay[i] for i in range(np_)]
        low_bd = [jnp.where(same_half, z, 0.0) for z in low]
        t_bd = [_unfold(z, hmasks) for z in _inv_unit_lower_folded([_fold(z) for z in low_bd], _parts2)]
        w_off = [_mm3(t_bd[i][n // 2:], low[i] - low_bd[i]) for i in range(np_)]
        w_off = [_mm3(w_off[i], t_bd[i]) for i in range(np_)]
        t = [jnp.concatenate([t_bd[i][0:n // 2], t_bd[i][n // 2:] - w_off[i]], axis=0) for i in range(np_)]
        eg = [jnp.exp(g) for g in gc]
        sol = [_mm3(t[i], jnp.concatenate([v[i] * beta[i], kb[i] * eg[i]], axis=1)) for i in range(np_)]
        g_last = [g[n - 1:n, :] for g in gc]
        k_dec_t = [(k[i] * jnp.exp(g_last[i] - gc[i])).T for i in range(np_)]
        st = [s_gdn[b, hd] for b, hd in probs]
        v_new = [sol[i][:, 0:GDN_DV] - _mm3(sol[i][:, GDN_DV:], st[i]) for i in range(np_)]
        o = [_mm(q[i] * eg[i], st[i]) + _mm(att[i], v_new[i]) for i in range(np_)]
        upd = [_mm3(k_dec_t[i], v_new[i]) for i in range(np_)]
        for i, (b, hd) in enumerate(probs):
            o_scr[pl.ds(b * tc + r0, n), GLA_V + hd * GDN_DV:GLA_V + (hd + 1) * GDN_DV] = o[i]
            s_gdn[b, hd] = st[i] * jnp.exp(g_last[i]) + upd[i]

        gprobs = [(b, p) for b in range(nb) for p in range(GLA_HEADS // 2)]
        ng = len(gprobs)
        pls = [slice(p * LANES, (p + 1) * LANES) for _, p in gprobs]
        for cc in range(n // GLA_CHUNK):
            crow = pl.ds(r0 + cc * GLA_CHUNK, GLA_CHUNK)
            ldg = lambda ref: [ref[b, crow, pls[i]] for i, (b, _) in enumerate(gprobs)]
            qe, kt, kd = ldg(qe_ref), ldg(kt_ref), ldg(kd_ref)
            vs = [jnp.concatenate([gv_ref[b, crow, (2 * p) * GLA_DV:(2 * p + 1) * GLA_DV],
                                   gv_ref[b, crow, (2 * p + 1) * GLA_DV:(2 * p + 2) * GLA_DV]], axis=0)
                  for b, p in gprobs]
            qs = [jnp.concatenate([z * m0, z * m1], axis=0) for z in qe]
            ks = [jnp.concatenate([z * m0, z * m1], axis=0) for z in kd]
            att = [jnp.where(bd_incl, _mm_nt(qs[i], jnp.concatenate([kt[i], kt[i]], axis=0)), 0.0)
                   for i in range(ng)]
            sg = [s_gla[b, p] for b, p in gprobs]
            og = [_mm(att[i], vs[i]) + _mm_nt(qs[i], sg[i]) for i in range(ng)]
            ug = [_mm(vs[i].T, ks[i]) for i in range(ng)]
            for i, (b, p) in enumerate(gprobs):
                ocrow = pl.ds(b * tc + r0 + cc * GLA_CHUNK, GLA_CHUNK)
                al = al_ref[b, s * (n // GLA_CHUNK) + cc][:, pls[i]]
                s_gla[b, p] = sg[i] * al + ug[i]
                o_scr[ocrow, (2 * p) * GLA_DV:(2 * p + 1) * GLA_DV] = og[i][0:GLA_CHUNK]
                o_scr[ocrow, (2 * p + 1) * GLA_DV:(2 * p + 2) * GLA_DV] = og[i][GLA_CHUNK:]
        return carry

    lax.fori_loop(0, tc // n, body, 0)

    sgg = sgg_ref[...].reshape(nb * tc, GLA_V)
    sdz = sdz_ref[...].reshape(nb * tc, GDN_V)
    for hd in range(GLA_HEADS + GDN_HEADS):
        hl = slice(hd * LANES, (hd + 1) * LANES)
        oh = o_scr[:, hl]
        nw = gn_ref[...] if hd < GLA_HEADS else dn_ref[...]
        gate = sgg[:, hl] if hd < GLA_HEADS else sdz[:, (hd - GLA_HEADS) * LANES:(hd - GLA_HEADS + 1) * LANES]
        o_scr[:, hl] = oh * lax.rsqrt(jnp.mean(oh * oh, -1, keepdims=True) + NORM_EPS) * nw * gate
    y = jnp.dot(o_scr[...].astype(BF16), wo_ref[...], preferred_element_type=F32)
    o_ref[...] = x_ref[...] + _rms(y, nw_ref[...]).reshape(nb, tc, d)


def _even_chunk(x2, batch, seq, prep, w_out, gla_norm, gdn_norm, nw):
    m, d = x2.shape
    tc = min(TILE_CHUNK, seq)
    ev, al = [a.reshape((batch, a.shape[0] // batch) + a.shape[1:]) for a in prep]

    def tok(n):
        return pl.BlockSpec((batch, tc, n), lambda j: (0, j, 0))

    out = pl.pallas_call(
        _even_chunk_kernel,
        grid=(seq // tc,),
        in_specs=[
            tok(d), tok(EV_WIDTH), pl.BlockSpec((batch, tc // GLA_CHUNK, 1, GLA_QK), lambda j: (0, j, 0, 0)),
            _const_spec((EVEN_OUT, d)), _const_spec((1, GLA_DV)), _const_spec((1, GDN_DV)), _const_spec((1, d)),
        ],
        out_specs=tok(d),
        out_shape=jax.ShapeDtypeStruct((batch, seq, d), F32),
        scratch_shapes=[
            pltpu.VMEM((batch, GLA_HEADS // 2, GLA_DV, LANES), F32),
            pltpu.VMEM((batch, GDN_HEADS, GDN_DK, GDN_DV), F32),
            pltpu.VMEM((batch * tc, EVEN_OUT), F32),
        ],
        compiler_params=pltpu.CompilerParams(dimension_semantics=("arbitrary",),
                                             vmem_limit_bytes=VMEM_LIMIT),
        name="even_chunk",
    )(x2.reshape(batch, seq, d), ev, al, w_out.astype(BF16), _row(gla_norm), _row(gdn_norm), _row(nw))
    return out.reshape(m, d)


def _odd_prep_kernel(tiles_per_seq,
                     x_ref, nw_ref, wp_ref, wl_ref, mu_ref, w0_ref, w2_ref, a0_ref, a2_ref, g2_ref,
                     kk_ref, ka_ref, rk_ref, ones_ref, cw_ref, cb_ref, wa_ref, ba_ref, wx_ref, bx_ref, lam_ref,
                     od_ref, wc_ref, carry_p, carry_l):
    v_ = _col_views(od_ref, OD_COLS)
    rt_ref, kt_ref, bt_ref, kh_ref, v_ref = v_["rt"], v_["kt"], v_["bt"], v_["kh"], v_["v"]
    g_ref, bonus_ref, la_ref, lb_ref, gly_ref = v_["g"], v_["bonus"], v_["la"], v_["lb"], v_["gly"]
    first = (pl.program_id(0) % tiles_per_seq) == 0
    tm = x_ref.shape[0]
    h = _rms(x_ref[...], nw_ref[...]).astype(BF16)

    ps = jnp.dot(h, wp_ref[...], preferred_element_type=F32)
    carry = jnp.where(first, 0.0, carry_p[...])
    prev = _shift_rows(ps, carry, 1)
    carry_p[...] = ps[tm - SUBLANES:tm, :]
    ps = ps + (prev - ps) * mu_ref[...]
    r = ps[:, 0:RWKV_W]
    k = ps[:, RWKV_W:2 * RWKV_W]
    v = ps[:, 2 * RWKV_W:3 * RWKV_W]
    lora = ps[:, 3 * RWKV_W:3 * RWKV_W + LANES]
    gl = ps[:, 3 * RWKV_W + LANES:]
    logw = -(RWKV_DECAY_SCALE * _sigmoid(w0_ref[...] + _mm(jnp.tanh(lora), w2_ref[...])))
    a = _sigmoid(a0_ref[...] + _mm(lora, a2_ref[...]))
    g_ref[...] = _mm(_sigmoid(gl), g2_ref[...])
    ones_bd = ones_ref[...]
    kk = k * kk_ref[...]
    kk = kk * lax.rsqrt(_group_sum(kk * kk, ones_bd) + NORM_EPS)
    k = k * (1.0 + (a - 1.0) * ka_ref[...])
    bonus_ref[...] = _group_sum(r * k * rk_ref[...], ones_bd) * v
    gcum = _seg_cumsum(logw, RWKV_CHUNK)
    for c in range(tm // RWKV_CHUNK):
        wc_ref[c] = jnp.exp(gcum[(c + 1) * RWKV_CHUNK - 1:(c + 1) * RWKV_CHUNK, :])
    e_neg = jnp.exp(-gcum)
    rt_ref[...] = r * jnp.exp(gcum)
    kt_ref[...] = k * e_neg
    bt_ref[...] = kk * a * e_neg
    kh_ref[...] = kk * jnp.exp(gcum - logw)
    v_ref[...] = v

    pq = jnp.dot(h, wl_ref[...], preferred_element_type=F32)
    lx = pq[:, 0:LRU_WIDTH]
    carry2 = jnp.where(first, 0.0, carry_l[...])
    xb = lx * cw_ref[CONV_W - 1:CONV_W, :] + cb_ref[...]
    for s in range(1, CONV_W):
        xb = xb + _shift_rows(lx, carry2, s) * cw_ref[CONV_W - 1 - s:CONV_W - s, :]
    carry_l[...] = lx[tm - SUBLANES:tm, :]
    gate_r = _sigmoid(_mm(xb, wa_ref[...]) + ba_ref[...])
    gate_i = _sigmoid(_mm(xb, wx_ref[...]) + bx_ref[...])
    log_a = -LRU_C * gate_r * _softplus(-lam_ref[...])
    mult = jnp.sqrt(jnp.maximum(-jnp.tanh(log_a) * (jnp.exp(2.0 * log_a) + 1.0), 0.0))
    la_ref[...] = jnp.exp(log_a)
    lb_ref[...] = mult * gate_i * xb
    gly_ref[...] = _gelu_tanh(pq[:, LRU_WIDTH:])


def _block_diag(w):
    nb, n, _ = w.shape
    eye = jnp.eye(nb, dtype=w.dtype)
    return (eye[:, None, :, None] * w[:, :, None, :]).reshape(nb * n, nb * n)


def _odd_prep(x2, seq, nw, w_in, mu, w0, w2, a0, a2, g2, k_k, k_a, r_k, ones_bd,
              conv_w, conv_b, wa, ba, wx, bx, lam):
    m, d = x2.shape
    tm = min(TILE_PREP, seq)
    w_p = w_in[:, 0:RWKV_SHIFT]
    w_l = w_in[:, RWKV_SHIFT:]
    w2p = jnp.concatenate([w2, jnp.zeros((RWKV_A_LORA, RWKV_W), F32)], axis=0)
    a2p = jnp.concatenate([jnp.zeros((RWKV_W_LORA, RWKV_W), F32), a2], axis=0)

    def tok(n):
        return pl.BlockSpec((tm, n), lambda i: (i, 0))

    wspec = pl.BlockSpec((tm // RWKV_CHUNK, 1, RWKV_W), lambda i: (i, 0, 0))
    out_specs = [tok(OD_WIDTH), wspec]
    out_shape = [jax.ShapeDtypeStruct((m, OD_WIDTH), F32), jax.ShapeDtypeStruct((m // RWKV_CHUNK, 1, RWKV_W), F32)]
    vec = _const_spec((1, RWKV_W))
    sq = _const_spec((RWKV_W, RWKV_W))
    return pl.pallas_call(
        functools.partial(_odd_prep_kernel, seq // tm),
        grid=(m // tm,),
        in_specs=[
            tok(d), _const_spec((1, d)), _const_spec(w_p.shape), _const_spec(w_l.shape),
            _const_spec((1, RWKV_SHIFT)), vec, _const_spec(w2p.shape), vec, _const_spec(a2p.shape),
            _const_spec(g2.shape), vec, vec, vec, sq,
            _const_spec(conv_w.shape), vec, sq, vec, sq, vec, vec,
        ],
        out_specs=out_specs,
        out_shape=out_shape,
        scratch_shapes=[pltpu.VMEM((SUBLANES, RWKV_SHIFT), F32), pltpu.VMEM((SUBLANES, LRU_WIDTH), F32)],
        compiler_params=pltpu.CompilerParams(dimension_semantics=("arbitrary",),
                                             vmem_limit_bytes=VMEM_LIMIT),
        name="odd_prep",
    )(x2, _row(nw), w_p.astype(BF16), w_l.astype(BF16), _row(mu), _row(w0), w2p.astype(BF16), _row(a0),
      a2p.astype(BF16), g2.astype(BF16), _row(k_k), _row(k_a), _row(r_k), ones_bd,
      conv_w.astype(F32), _row(conv_b), _block_diag(wa).astype(BF16), _row(ba),
      _block_diag(wx).astype(BF16), _row(bx), _row(lam))


def _odd_chunk_kernel(x_ref, od_ref, wc_ref, wo_ref, lnw_ref, lnb_ref, ones_ref, nw_ref,
                      o_ref, s_rwkv, h_lru, y_scr):
    v_ = _col_views(od_ref, OD_COLS)
    rt_ref, kt_ref, bt_ref, kh_ref, v_ref = v_["rt"], v_["kt"], v_["bt"], v_["kh"], v_["v"]
    g_ref, bonus_ref, la_ref, lb_ref, gly_ref = v_["g"], v_["bonus"], v_["la"], v_["lb"], v_["gly"]
    nb, tc, d = x_ref.shape

    @pl.when(pl.program_id(0) == 0)
    def _():
        s_rwkv[...] = jnp.zeros(s_rwkv.shape, F32)
        h_lru[...] = jnp.zeros(h_lru.shape, F32)

    cs = RWKV_CHUNK
    r_i = lax.broadcasted_iota(jnp.int32, (cs, 2 * cs), 0)
    c_i = lax.broadcasted_iota(jnp.int32, (cs, 2 * cs), 1) & (cs - 1)
    f_incl = c_i <= r_i
    f_strict = c_i < r_i
    masks = _head_masks(RWKV_N)
    stack = lambda z: _unfold(z, masks)

    def rwkv_chunk(c):
        r0 = c * cs
        rows = pl.ds(r0, cs)
        probs = [(b, p) for b in range(nb) for p in range(RWKV_HEADS // 2)]
        pls = [slice(p * LANES, (p + 1) * LANES) for _, p in probs]
        np_ = len(probs)
        ld = lambda ref: [ref[b, rows, pls[i]] for i, (b, _) in enumerate(probs)]
        rt, kt, bt, kh, vv = ld(rt_ref), ld(kt_ref), ld(bt_ref), ld(kh_ref), ld(v_ref)
        vs = [stack(z) for z in vv]
        ks = [stack(z) for z in kt]
        bs = [stack(z) for z in bt]
        mm = [_mm_nt(jnp.concatenate([kh[i], rt[i]], axis=0), jnp.concatenate([bs[i], ks[i]], axis=0))
              for i in range(np_)]
        m_rb = [jnp.where(f_incl, mm[i][cs:, 0:LANES], 0.0) for i in range(np_)]
        m_bk = [jnp.where(f_strict, mm[i][0:cs, LANES:], 0.0) for i in range(np_)]
        m_rk = [jnp.where(f_incl, mm[i][cs:, LANES:], 0.0) for i in range(np_)]
        t = _inv_unit_lower_folded([jnp.where(f_strict, mm[i][0:cs, 0:LANES], 0.0) for i in range(np_)], _parts1)
        x = [_mm(m_bk[i], vs[i]) for i in range(np_)]
        sol = [_mm(t[i], jnp.concatenate([stack(kh[i]), stack(x[i])], axis=1)) for i in range(np_)]
        sol_s = [jnp.concatenate([stack(z[:, 0:LANES]), stack(z[:, LANES:])], axis=1) for z in sol]
        corr = [_mm(m_rb[i], sol_s[i]) for i in range(np_)]
        y_loc = [_mm(m_rk[i], vs[i]) - corr[i][:, LANES:] for i in range(np_)]
        r_eff = [rt[i] - corr[i][:, 0:LANES] for i in range(np_)]
        vtk = [_mm(_fold(vs[i].T), ks[i]) for i in range(np_)]
        uv_t = [_fold(sol_s[i][:, LANES:].T) for i in range(np_)]
        st = [s_rwkv[b, p] for b, p in probs]
        ut = [_mm_nt(st[i], sol_s[i][:, 0:LANES]) + uv_t[i] for i in range(np_)]
        ys = [_mm_nt(r_eff[i], stack(st[i])) + y_loc[i] for i in range(np_)]
        upd = [_mm(ut[i], bs[i]) for i in range(np_)]
        for i, (b, p) in enumerate(probs):
            y_scr[pl.ds(b * tc + r0, cs), pls[i]] = ys[i]
            s_rwkv[b, p] = (st[i] + vtk[i] - upd[i]) * wc_ref[b, c][:, pls[i]]

    def lru_scan():
        rows_i = lax.broadcasted_iota(jnp.int32, (tc, LRU_WIDTH), 0) & (SUBLANES - 1)
        a_s, b_s = [], []
        for b in range(nb):
            a = la_ref[b]
            bb = lb_ref[b]
            dd = 1
            while dd < SUBLANES:
                keep = rows_i >= dd
                bb = jnp.where(keep, a * pltpu.roll(bb, dd, 0) + bb, bb)
                a = jnp.where(keep, a * pltpu.roll(a, dd, 0), a)
                dd *= 2
            a_s.append(a)
            b_s.append(bb)
        hrow = [h_lru[b, SUBLANES - 1:SUBLANES, :] for b in range(nb)]
        hgs = [[] for _ in range(nb)]
        for g in range(tc // SUBLANES):
            gs = slice(g * SUBLANES, (g + 1) * SUBLANES)
            for b in range(nb):
                hg = b_s[b][gs] + a_s[b][gs] * hrow[b]
                hrow[b] = hg[SUBLANES - 1:SUBLANES, :]
                hgs[b].append(hg)
        out = []
        for b in range(nb):
            h_lru[b] = hgs[b][-1]
            out.extend(hgs[b])
        return out

    hls = None
    for c in range(tc // RWKV_CHUNK):
        rwkv_chunk(c)
        if c == 0:
            hls = lru_scan()
    y_lru = jnp.concatenate(hls, axis=0) * gly_ref[...].reshape(nb * tc, LRU_WIDTH)

    ones_bd = ones_ref[...]
    y = y_scr[...]
    mu = _group_sum(y, ones_bd) * (1.0 / RWKV_N)
    yc = y - mu
    var = _group_sum(yc * yc, ones_bd) * (1.0 / RWKV_N)
    yn = yc * lax.rsqrt(var + RWKV_GN_EPS) * lnw_ref[...] + lnb_ref[...]
    y_rwkv = (yn + bonus_ref[...].reshape(nb * tc, RWKV_W)) * g_ref[...].reshape(nb * tc, RWKV_W)
    out = (jnp.dot(y_rwkv.astype(BF16), wo_ref[0:RWKV_W, :], preferred_element_type=F32)
           + jnp.dot(y_lru.astype(BF16), wo_ref[RWKV_W:, :], preferred_element_type=F32))
    o_ref[...] = x_ref[...] + _rms(out, nw_ref[...]).reshape(nb, tc, d)


def _odd_chunk(x2, batch, seq, prep, w_out, ln_w, ln_b, ones_bd, nw):
    m, d = x2.shape
    tc = min(TILE_CHUNK, seq)
    od, wc = [a.reshape((batch, a.shape[0] // batch) + a.shape[1:]) for a in prep]

    def tok(n):
        return pl.BlockSpec((batch, tc, n), lambda j: (0, j, 0))

    vec = _const_spec((1, RWKV_W))
    out = pl.pallas_call(
        _odd_chunk_kernel,
        grid=(seq // tc,),
        in_specs=[
            tok(d), tok(OD_WIDTH), pl.BlockSpec((batch, tc // RWKV_CHUNK, 1, RWKV_W), lambda j: (0, j, 0, 0)),
            _const_spec((ODD_OUT, d)), vec, vec, _const_spec((RWKV_W, RWKV_W)), _const_spec((1, d)),
        ],
        out_specs=tok(d),
        out_shape=jax.ShapeDtypeStruct((batch, seq, d), F32),
        scratch_shapes=[
            pltpu.VMEM((batch, RWKV_HEADS // 2, RWKV_N, LANES), F32),
            pltpu.VMEM((batch, SUBLANES, LRU_WIDTH), F32),
            pltpu.VMEM((batch * tc, RWKV_W), F32),
        ],
        compiler_params=pltpu.CompilerParams(dimension_semantics=("arbitrary",),
                                             vmem_limit_bytes=VMEM_LIMIT),
        name="odd_chunk",
    )(x2.reshape(batch, seq, d), od, wc, w_out.astype(BF16), _row(ln_w), _row(ln_b), ones_bd, _row(nw))
    return out.reshape(m, d)


def _even_layer(x2, batch, seq, nw, w_in, w_out, lora_w2, lora_b, gla_norm, conv_w, a_log, dt_bias, gdn_norm):
    prep = _even_prep(x2, seq, nw[2], w_in, lora_w2, lora_b, conv_w, a_log, dt_bias)
    return _even_chunk(x2, batch, seq, prep, w_out, gla_norm, gdn_norm, nw[3])


def _odd_layer(x2, batch, seq, nw, w_in, w_out, mu, w0, w2, a0, a2, g2, k_k, k_a, r_k, ln_w, ln_b,
               conv_w, conv_b, wa, ba, wx, bx, lam):
    grp = jnp.arange(RWKV_W) // RWKV_N
    ones_bd = (grp[:, None] == grp[None, :]).astype(BF16)
    prep = _odd_prep(x2, seq, nw[2], w_in, mu, w0, w2, a0, a2, g2, k_k, k_a, r_k, ones_bd,
                     conv_w, conv_b, wa, ba, wx, bx, lam)
    return _odd_chunk(x2, batch, seq, prep, w_out, ln_w, ln_b, ones_bd, nw[3])


def kernel(x, norm_w, ffn_w_gate, ffn_w_up, ffn_w_down, even_w_in, even_w_out, gla_lora_w2, gla_lora_b, gla_norm, gdn_conv, gdn_a_log, gdn_dt_bias, gdn_norm, odd_w_in, odd_w_out, rwkv_mu, rwkv_w0, rwkv_w2, rwkv_a0, rwkv_a2, rwkv_g2, rwkv_k_k, rwkv_k_a, rwkv_r_k, rwkv_ln_w, rwkv_ln_b, lru_conv_w, lru_conv_b, lru_wa, lru_ba, lru_wx, lru_bx, lru_lambda):
    batch, seq, d = x.shape
    depth = norm_w.shape[0]
    x2 = x.reshape(batch * seq, d)
    wg, wu, wd = ffn_w_gate.astype(BF16), ffn_w_up.astype(BF16), ffn_w_down.astype(BF16)
    for i in range(depth):
        j = i // 2
        nw = norm_w[i]
        x2 = _ffn(x2, nw[0], nw[1], wg, wu, wd, i, 0)
        if i % 2 == 0:
            x2 = _even_layer(x2, batch, seq, nw, even_w_in[j], even_w_out[j], gla_lora_w2[j], gla_lora_b[j],
                             gla_norm[j], gdn_conv[j], gdn_a_log[j], gdn_dt_bias[j], gdn_norm[j])
        else:
            x2 = _odd_layer(x2, batch, seq, nw, odd_w_in[j], odd_w_out[j], rwkv_mu[j], rwkv_w0[j], rwkv_w2[j],
                            rwkv_a0[j], rwkv_a2[j], rwkv_g2[j], rwkv_k_k[j], rwkv_k_a[j],
                            rwkv_r_k[j].reshape(-1), rwkv_ln_w[j], rwkv_ln_b[j], lru_conv_w[j], lru_conv_b[j],
                            lru_wa[j], lru_ba[j], lru_wx[j], lru_bx[j], lru_lambda[j])
        x2 = _ffn(x2, nw[4], nw[5], wg, wu, wd, i, 1)
    return x2.reshape(batch, seq, d)
```

```python
import functools

import jax
import jax.numpy as jnp
from jax import lax
from jax.experimental import pallas as pl
from jax.experimental.pallas import tpu as pltpu

F32 = jnp.float32
BF16 = jnp.bfloat16

D_MODEL = 1024
D_FF = 2816
FFN_RES = 0.5
NORM_EPS = 1e-6
CONV_W = 4

GLA_HEADS = 4
GLA_DK = 64
GLA_DV = 128
GLA_LORA = 16
GLA_GATE_NORM = 16.0
GLA_QK = GLA_HEADS * GLA_DK
GLA_V = GLA_HEADS * GLA_DV
GLA_CHUNK = 64

GDN_HEADS = 4
GDN_DK = 128
GDN_DV = 128
GDN_QK = GDN_HEADS * GDN_DK
GDN_V = GDN_HEADS * GDN_DV
GDN_CONV_CH = 2 * GDN_QK + GDN_V
GDN_CHUNK = 128

RWKV_HEADS = 8
RWKV_N = 64
RWKV_W = RWKV_HEADS * RWKV_N
RWKV_W_LORA = 64
RWKV_A_LORA = 64
RWKV_G_LORA = 128
RWKV_GN_EPS = 64e-5
RWKV_SHIFT = 3 * RWKV_W + RWKV_W_LORA + RWKV_A_LORA + RWKV_G_LORA
RWKV_CHUNK = 64
RWKV_DECAY_SCALE = 0.6065306597126334

LRU_WIDTH = 512
LRU_BLOCKS = 8
LRU_BW = LRU_WIDTH // LRU_BLOCKS
LRU_C = 8.0

EVEN_OUT = GLA_V + GDN_V
ODD_OUT = RWKV_W + LRU_WIDTH

LANES = 128
SUBLANES = 8
SMALL_GLR = 0
SMALL_DA = GLA_LORA
SMALL_DB = GLA_LORA + GDN_HEADS

EV_COLS = dict(qe=(0, GLA_QK), kt=(GLA_QK, GLA_QK), kd=(2 * GLA_QK, GLA_QK), gv=(3 * GLA_QK, GLA_V),
               sgg=(3 * GLA_QK + GLA_V, GLA_V), dq=(3 * GLA_QK + 2 * GLA_V, GDN_QK),
               dk=(3 * GLA_QK + 2 * GLA_V + GDN_QK, GDN_QK), dv=(3 * GLA_QK + 2 * GLA_V + 2 * GDN_QK, GDN_V),
               gb=(3 * GLA_QK + 2 * GLA_V + 2 * GDN_QK + GDN_V, LANES),
               sdz=(3 * GLA_QK + 2 * GLA_V + 2 * GDN_QK + GDN_V + LANES, GDN_V))
EV_WIDTH = 3 * GLA_QK + 2 * GLA_V + 2 * GDN_QK + 2 * GDN_V + LANES
OD_NAMES = ("rt", "kt", "bt", "kh", "v", "g", "bonus", "la", "lb", "gly")
OD_COLS = {name: (i * RWKV_W, RWKV_W) for i, name in enumerate(OD_NAMES)}
OD_WIDTH = len(OD_NAMES) * RWKV_W

TILE_FFN = 1024
FFN_PARTS = 4
TILE_PREP = 512
EVEN_PREP_PARTS = 4
ODD_PREP_PARTS = 2
TILE_CHUNK = 128
VMEM_LIMIT = 56 * 1024 * 1024


def _mm(a, b):
    return jnp.dot(a.astype(BF16), b.astype(BF16), preferred_element_type=F32)


def _mm_nt(a, b):
    return lax.dot_general(a.astype(BF16), b.astype(BF16), (((1,), (1,)), ((), ())),
                           preferred_element_type=F32)


def _rms(x, w):
    return x * lax.rsqrt(jnp.mean(x * x, axis=-1, keepdims=True) + NORM_EPS) * w


def _sigmoid(x):
    return 0.5 * jnp.tanh(0.5 * x) + 0.5


def _silu(x):
    h = 0.5 * x
    return h + h * jnp.tanh(h)


def _softplus(x):
    return jnp.maximum(x, 0.0) + jnp.log(1.0 + jnp.exp(-jnp.abs(x)))


def _gelu_tanh(x):
    c = 0.7978845608028654
    return 0.5 * x * (1.0 + jnp.tanh(c * (x + 0.044715 * (x * x * x))))


def _seg_cumsum(x, seg):
    rows = lax.broadcasted_iota(jnp.int32, x.shape, 0) & (seg - 1)
    d = 1
    while d < seg:
        x = x + jnp.where(rows >= d, pltpu.roll(x, d, 0), 0.0)
        d *= 2
    return x


def _shift_rows(x, carry, s):
    sh = pltpu.roll(x, s, 0)
    c = pltpu.roll(carry, s, 0)
    rows = lax.broadcasted_iota(jnp.int32, c.shape, 0)
    head = jnp.where(rows < s, c, sh[0:SUBLANES])
    return jnp.concatenate([head, sh[SUBLANES:]], axis=0)


def _group_sum(x, ones_bd):
    hi = x.astype(BF16)
    lo = (x - hi.astype(F32)).astype(BF16)
    dot = functools.partial(jnp.dot, preferred_element_type=F32)
    return dot(hi, ones_bd) + dot(lo, ones_bd)


def _split(a):
    hi = a.astype(BF16)
    return hi, (a - hi.astype(F32)).astype(BF16)


def _parts1(a):
    return (a.astype(BF16),)


def _parts2(a):
    return _split(a)


def _mm_parts(ap, bp, nt=False):
    dims = (((1,), (1 if nt else 0,)), ((), ()))
    a2 = ap[0] if len(ap) == 1 else jnp.concatenate(ap, axis=1)
    bcat = bp[0] if len(bp) == 1 else jnp.concatenate(bp, axis=0 if nt else 1)
    rhs = bcat if len(ap) == 1 else jnp.concatenate([bcat] * len(ap), axis=1 if nt else 0)
    out = lax.dot_general(a2, rhs, dims, preferred_element_type=F32)
    if len(bp) == 1:
        return out
    n = out.shape[1] // 2
    return out[:, 0:n] + out[:, n:]


def _mm3(a, b, nt=False):
    return _mm_parts(_parts2(a), _parts2(b), nt)


def _head_masks(half):
    lane = lax.broadcasted_iota(jnp.int32, (1, 2 * half), 1)
    m0 = jnp.where(lane < half, 1.0, 0.0)
    return m0, 1.0 - m0


def _unfold(z, masks):
    return jnp.concatenate([z * masks[0], z * masks[1]], axis=0)


def _fold(z):
    c = z.shape[0] // 2
    return z[0:c] + z[c:]


def _inv_unit_lower_folded(lows_f, parts):
    c = lows_f[0].shape[0]
    r = lax.broadcasted_iota(jnp.int32, lows_f[0].shape, 0)
    col = lax.broadcasted_iota(jnp.int32, lows_f[0].shape, 1)
    eye = jnp.where(r == (col & (c - 1)), 1.0, 0.0)
    mb = (jnp.where(col < c, 1.0, 0.0).astype(BF16), jnp.where(col < c, 0.0, 1.0).astype(BF16))
    unf = lambda terms: tuple(jnp.concatenate([z * mb[0], z * mb[1]], axis=0) for z in terms)
    negs = [-low for low in lows_f]
    ts = [eye + p for p in negs]
    sp = [parts(p) for p in negs]
    ps = [_mm_parts(x, unf(x)) for x in sp]
    k = 4
    while k < c:
        sq = [parts(p) for p in ps]
        st = [parts(t) for t in ts]
        rhs = [tuple(jnp.concatenate([uq, ut], axis=1) for uq, ut in zip(unf(x), unf(y))) for x, y in zip(sq, st)]
        pq = [_mm_parts(x, y) for x, y in zip(sq, rhs)]
        ps = [z[:, 0:2 * c] for z in pq]
        ts = [t + z[:, 2 * c:] for t, z in zip(ts, pq)]
        k *= 2
    return [t + _mm_parts(parts(p), unf(parts(t))) for p, t in zip(ps, ts)]


def _const_spec(shape):
    nd = len(shape)
    return pl.BlockSpec(shape, lambda *_: (0,) * nd, pipeline_mode=pl.Buffered(1))


def _row(v):
    return v.reshape(1, -1).astype(F32)


def _col_views(ref, cols):
    lead = (slice(None),) * (len(ref.shape) - 1)
    return {name: ref.at[lead + (slice(off, off + n),)] for name, (off, n) in cols.items()}


def _ffn_kernel(x_ref, nw_ref, wg_ref, wu_ref, wd_ref, o_ref):
    tm = x_ref.shape[0]
    rows = tm // FFN_PARTS
    dot = functools.partial(jnp.dot, preferred_element_type=F32)
    sl = [slice(i * rows, (i + 1) * rows) for i in range(FFN_PARTS)]
    x, h, g, u, a, y = ({} for _ in range(6))
    for t in range(FFN_PARTS + 2):
        if t < FFN_PARTS:
            x[t] = x_ref[sl[t], :]
            h[t] = _rms(x[t], nw_ref[0:1, :]).astype(BF16)
            g[t] = dot(h[t], wg_ref[...])
            u[t] = dot(h[t], wu_ref[...])
        if 1 <= t <= FFN_PARTS:
            a[t - 1] = (_silu(g[t - 1]) * u[t - 1]).astype(BF16)
            y[t - 1] = dot(a[t - 1], wd_ref[...])
        if t >= 2:
            o_ref[sl[t - 2], :] = x[t - 2] + FFN_RES * _rms(y[t - 2], nw_ref[1:2, :])


def _ffn(x2, nw_pre, nw_post, wg, wu, wd, layer, slot):
    m, d = x2.shape
    tm = min(TILE_FFN, m)
    nw = jnp.stack([nw_pre, nw_post]).astype(F32)

    def pick(rows, cols):
        return pl.BlockSpec((None, None, rows, cols), lambda i: (layer, slot, 0, 0), pipeline_mode=pl.Buffered(1))

    return pl.pallas_call(
        _ffn_kernel,
        grid=(m // tm,),
        in_specs=[
            pl.BlockSpec((tm, d), lambda i: (i, 0)),
            _const_spec((2, d)),
            pick(d, D_FF),
            pick(d, D_FF),
            pick(D_FF, d),
        ],
        out_specs=pl.BlockSpec((tm, d), lambda i: (i, 0)),
        out_shape=jax.ShapeDtypeStruct((m, d), F32),
        compiler_params=pltpu.CompilerParams(dimension_semantics=("arbitrary",),
                                             vmem_limit_bytes=VMEM_LIMIT),
        name="ffn",
    )(x2, nw, wg, wu, wd)


def _even_prep_part(r0, rows, proj, carry, lw2_ref, lb_ref, conv_ref, hp_ref, v_, al_ref):
    pa, ps, pc, pz = proj
    rs = slice(r0, r0 + rows)
    z = _mm(ps, lw2_ref[...]) + lb_ref[...]
    log_a = -_softplus(-z) * (1.0 / GLA_GATE_NORM)
    b = _seg_cumsum(log_a, GLA_CHUNK)
    gk = pa[:, GLA_QK:2 * GLA_QK]
    v_["qe"][rs, :] = pa[:, 0:GLA_QK] * (GLA_DK ** -0.5) * jnp.exp(b)
    v_["kt"][rs, :] = gk * jnp.exp(-b)
    for c in range(rows // GLA_CHUNK):
        cs = slice(c * GLA_CHUNK, (c + 1) * GLA_CHUNK)
        b_last = b[(c + 1) * GLA_CHUNK - 1:(c + 1) * GLA_CHUNK, :]
        al_ref[r0 // GLA_CHUNK + c] = jnp.exp(b_last)
        v_["kd"][r0 + c * GLA_CHUNK:r0 + (c + 1) * GLA_CHUNK, :] = gk[cs] * jnp.exp(b_last - b[cs])
    v_["gv"][rs, :] = pa[:, 2 * GLA_QK:2 * GLA_QK + GLA_V]
    v_["sgg"][rs, :] = _silu(pa[:, 2 * GLA_QK + GLA_V:])

    conv = pc * conv_ref[CONV_W - 1:CONV_W, :]
    for s in range(1, CONV_W):
        conv = conv + _shift_rows(pc, carry, s) * conv_ref[CONV_W - 1 - s:CONV_W - s, :]
    c = _silu(conv)
    for hd in range(GDN_HEADS):
        lo, hi = hd * GDN_DK, (hd + 1) * GDN_DK
        cq = c[:, lo:hi]
        ck = c[:, GDN_QK + lo:GDN_QK + hi]
        v_["dq"][rs, lo:hi] = cq * lax.rsqrt(jnp.sum(cq * cq, -1, keepdims=True) + NORM_EPS) * (GDN_DK ** -0.5)
        v_["dk"][rs, lo:hi] = ck * lax.rsqrt(jnp.sum(ck * ck, -1, keepdims=True) + NORM_EPS)
    v_["dv"][rs, :] = c[:, 2 * GDN_QK:]

    g = -jnp.exp(hp_ref[0:1, :]) * _softplus(ps + hp_ref[1:2, :])
    gc = _seg_cumsum(g, GDN_CHUNK)
    lane = lax.broadcasted_iota(jnp.int32, ps.shape, 1)
    is_g = (lane >= SMALL_DA) & (lane < SMALL_DB)
    v_["gb"][rs, :] = jnp.where(is_g, gc, _sigmoid(ps))

    v_["sdz"][rs, :] = _silu(pz)
    return pc[rows - SUBLANES:rows, :]


def _even_prep_kernel(tiles_per_seq,
                      x_ref, nw_ref, wa_ref, ws_ref, wc_ref, wz_ref, lw2_ref, lb_ref, conv_ref, hp_ref,
                      ev_ref, al_ref, carry_ref):
    v_ = _col_views(ev_ref, EV_COLS)
    first = (pl.program_id(0) % tiles_per_seq) == 0
    tm = x_ref.shape[0]
    rows = tm // EVEN_PREP_PARTS
    dot = functools.partial(jnp.dot, preferred_element_type=F32)
    carry = jnp.where(first, 0.0, carry_ref[...])
    proj = {}
    for t in range(EVEN_PREP_PARTS + 1):
        if t < EVEN_PREP_PARTS:
            h = _rms(x_ref[t * rows:(t + 1) * rows, :], nw_ref[...]).astype(BF16)
            proj[t] = (dot(h, wa_ref[...]), dot(h, ws_ref[...]), dot(h, wc_ref[...]), dot(h, wz_ref[...]))
        if t >= 1:
            carry = _even_prep_part((t - 1) * rows, rows, proj.pop(t - 1), carry,
                                    lw2_ref, lb_ref, conv_ref, hp_ref, v_, al_ref)
    carry_ref[...] = carry


def _even_prep(x2, seq, nw, w_in, lora_w2, lora_b, conv_w, a_log, dt_bias):
    m, d = x2.shape
    tm = min(TILE_PREP, seq)
    o = 0
    w_a = w_in[:, 0:2 * GLA_QK + 2 * GLA_V]
    o = 2 * GLA_QK + 2 * GLA_V
    w_glr = w_in[:, o:o + GLA_LORA]
    o += GLA_LORA
    w_c = w_in[:, o:o + GDN_CONV_CH]
    o += GDN_CONV_CH
    w_z = w_in[:, o:o + GDN_V]
    o += GDN_V
    w_da = w_in[:, o:o + GDN_HEADS]
    w_db = w_in[:, o + GDN_HEADS:o + 2 * GDN_HEADS]
    n_small = GLA_LORA + 2 * GDN_HEADS
    w_s = jnp.concatenate([w_glr, w_da, w_db, jnp.zeros((d, LANES - n_small), F32)], axis=1)
    lw2 = jnp.concatenate([lora_w2, jnp.zeros((LANES - GLA_LORA, GLA_QK), F32)], axis=0)
    pad_l = jnp.zeros((SMALL_DA,), F32)
    pad_r = jnp.zeros((LANES - SMALL_DB,), F32)
    hp = jnp.stack([jnp.concatenate([pad_l, a_log.astype(F32), pad_r]),
                    jnp.concatenate([pad_l, dt_bias.astype(F32), pad_r])])

    def tok(n):
        return pl.BlockSpec((tm, n), lambda i: (i, 0))

    out_specs = [tok(EV_WIDTH), pl.BlockSpec((tm // GLA_CHUNK, 1, GLA_QK), lambda i: (i, 0, 0))]
    out_shape = [jax.ShapeDtypeStruct((m, EV_WIDTH), F32), jax.ShapeDtypeStruct((m // GLA_CHUNK, 1, GLA_QK), F32)]
    return pl.pallas_call(
        functools.partial(_even_prep_kernel, seq // tm),
        grid=(m // tm,),
        in_specs=[
            tok(d), _const_spec((1, d)),
            _const_spec(w_a.shape), _const_spec(w_s.shape), _const_spec(w_c.shape), _const_spec(w_z.shape),
            _const_spec(lw2.shape), _const_spec((1, GLA_QK)), _const_spec(conv_w.shape), _const_spec(hp.shape),
        ],
        out_specs=out_specs,
        out_shape=out_shape,
        scratch_shapes=[pltpu.VMEM((SUBLANES, GDN_CONV_CH), F32)],
        compiler_params=pltpu.CompilerParams(dimension_semantics=("arbitrary",),
                                             vmem_limit_bytes=VMEM_LIMIT),
        name="even_prep",
    )(x2, _row(nw), w_a.astype(BF16), w_s.astype(BF16), w_c.astype(BF16), w_z.astype(BF16),
      lw2.astype(BF16), _row(lora_b), conv_w.astype(F32), hp)


def _even_chunk_kernel(x_ref, ev_ref, al_ref, wo_ref, gn_ref, dn_ref, nw_ref,
                       o_ref, s_gla, s_gdn, o_scr):
    v_ = _col_views(ev_ref, EV_COLS)
    qe_ref, kt_ref, kd_ref, gv_ref, sgg_ref = v_["qe"], v_["kt"], v_["kd"], v_["gv"], v_["sgg"]
    dq_ref, dk_ref, dv_ref, gb_ref, sdz_ref = v_["dq"], v_["dk"], v_["dv"], v_["gb"], v_["sdz"]
    nb, tc, d = x_ref.shape

    @pl.when(pl.program_id(0) == 0)
    def _():
        s_gla[...] = jnp.zeros(s_gla.shape, F32)
        s_gdn[...] = jnp.zeros(s_gdn.shape, F32)

    n = GDN_CHUNK
    r_i = lax.broadcasted_iota(jnp.int32, (n, n), 0)
    c_i = lax.broadcasted_iota(jnp.int32, (n, n), 1)
    tril = c_i <= r_i
    strict = c_i < r_i
    same_half = (c_i >= n // 2) == (r_i >= n // 2)
    hmasks = _head_masks(n // 2)
    bd_incl = ((c_i & (GLA_CHUNK - 1)) <= (r_i & (GLA_CHUNK - 1))) & ((c_i >= GLA_CHUNK) == (r_i >= GLA_CHUNK))
    lane = lax.broadcasted_iota(jnp.int32, (1, LANES), 1)
    m0 = jnp.where(lane < GLA_DK, 1.0, 0.0)
    m1 = 1.0 - m0

    def body(s, carry):
        r0 = pl.multiple_of(s * n, n)
        rows = pl.ds(r0, n)
        probs = [(b, hd) for b in range(nb) for hd in range(GDN_HEADS)]
        np_ = len(probs)
        hls = [slice(hd * GDN_DK, (hd + 1) * GDN_DK) for _, hd in probs]
        ld = lambda ref: [ref[b, rows, hls[i]] for i, (b, _) in enumerate(probs)]
        q, k, v = ld(dq_ref), ld(dk_ref), ld(dv_ref)
        gbt = [gb_ref[b, rows, :] for b in range(nb)]
        gc = [jnp.broadcast_to(gbt[b][:, SMALL_DA + hd:SMALL_DA + hd + 1], (n, n)) for b, hd in probs]
        beta = [jnp.broadcast_to(gbt[b][:, SMALL_DB + hd:SMALL_DB + hd + 1], (n, n)) for b, hd in probs]
        decay = [jnp.where(tril, jnp.exp(jnp.where(tril, g - g.T, 0.0)), 0.0) for g in gc]
        kb = [k[i] * beta[i] for i in range(np_)]
        low = [jnp.where(strict, _mm3(kb[i], k[i], nt=True) * decay[i], 0.0) for i in range(np_)]
        att = [_mm_nt(q[i], k[i]) * decay[i] for i in range(np_)]
        low_bd = [jnp.where(same_half, z, 0.0) for z in low]
        t_bd = [_unfold(z, hmasks) for z in _inv_unit_lower_folded([_fold(z) for z in low_bd], _parts2)]
        w_off = [_mm3(t_bd[i][n // 2:], low[i] - low_bd[i]) for i in range(np_)]
        w_off = [_mm3(w_off[i], t_bd[i]) for i in range(np_)]
        t = [jnp.concatenate([t_bd[i][0:n // 2], t_bd[i][n // 2:] - w_off[i]], axis=0) for i in range(np_)]
        eg = [jnp.exp(g) for g in gc]
        sol = [_mm3(t[i], jnp.concatenate([v[i] * beta[i], kb[i] * eg[i]], axis=1)) for i in range(np_)]
        g_last = [g[n - 1:n, :] for g in gc]
        k_dec_t = [(k[i] * jnp.exp(g_last[i] - gc[i])).T for i in range(np_)]
        st = [s_gdn[b, hd] for b, hd in probs]
        v_new = [sol[i][:, 0:GDN_DV] - _mm3(sol[i][:, GDN_DV:], st[i]) for i in range(np_)]
        o = [_mm(q[i] * eg[i], st[i]) + _mm(att[i], v_new[i]) for i in range(np_)]
        upd = [_mm3(k_dec_t[i], v_new[i]) for i in range(np_)]
        for i, (b, hd) in enumerate(probs):
            o_scr[pl.ds(b * tc + r0, n), GLA_V + hd * GDN_DV:GLA_V + (hd + 1) * GDN_DV] = o[i]
            s_gdn[b, hd] = st[i] * jnp.exp(g_last[i]) + upd[i]

        gprobs = [(b, p) for b in range(nb) for p in range(GLA_HEADS // 2)]
        ng = len(gprobs)
        pls = [slice(p * LANES, (p + 1) * LANES) for _, p in gprobs]
        for cc in range(n // GLA_CHUNK):
            crow = pl.ds(r0 + cc * GLA_CHUNK, GLA_CHUNK)
            ldg = lambda ref: [ref[b, crow, pls[i]] for i, (b, _) in enumerate(gprobs)]
            qe, kt, kd = ldg(qe_ref), ldg(kt_ref), ldg(kd_ref)
            vs = [jnp.concatenate([gv_ref[b, crow, (2 * p) * GLA_DV:(2 * p + 1) * GLA_DV],
                                   gv_ref[b, crow, (2 * p + 1) * GLA_DV:(2 * p + 2) * GLA_DV]], axis=0)
                  for b, p in gprobs]
            qs = [jnp.concatenate([z * m0, z * m1], axis=0) for z in qe]
            ks = [jnp.concatenate([z * m0, z * m1], axis=0) for z in kd]
            att = [jnp.where(bd_incl, _mm_nt(qs[i], jnp.concatenate([kt[i], kt[i]], axis=0)), 0.0)
                   for i in range(ng)]
            sg = [s_gla[b, p] for b, p in gprobs]
            og = [_mm(att[i], vs[i]) + _mm_nt(qs[i], sg[i]) for i in range(ng)]
            ug = [_mm(vs[i].T, ks[i]) for i in range(ng)]
            for i, (b, p) in enumerate(gprobs):
                ocrow = pl.ds(b * tc + r0 + cc * GLA_CHUNK, GLA_CHUNK)
                al = al_ref[b, s * (n // GLA_CHUNK) + cc][:, pls[i]]
                s_gla[b, p] = sg[i] * al + ug[i]
                o_scr[ocrow, (2 * p) * GLA_DV:(2 * p + 1) * GLA_DV] = og[i][0:GLA_CHUNK]
                o_scr[ocrow, (2 * p + 1) * GLA_DV:(2 * p + 2) * GLA_DV] = og[i][GLA_CHUNK:]
        return carry

    lax.fori_loop(0, tc // n, body, 0)

    sgg = sgg_ref[...].reshape(nb * tc, GLA_V)
    sdz = sdz_ref[...].reshape(nb * tc, GDN_V)
    for hd in range(GLA_HEADS + GDN_HEADS):
        hl = slice(hd * LANES, (hd + 1) * LANES)
        oh = o_scr[:, hl]
        nw = gn_ref[...] if hd < GLA_HEADS else dn_ref[...]
        gate = sgg[:, hl] if hd < GLA_HEADS else sdz[:, (hd - GLA_HEADS) * LANES:(hd - GLA_HEADS + 1) * LANES]
        o_scr[:, hl] = oh * lax.rsqrt(jnp.mean(oh * oh, -1, keepdims=True) + NORM_EPS) * nw * gate
    y = jnp.dot(o_scr[...].astype(BF16), wo_ref[...], preferred_element_type=F32)
    o_ref[...] = x_ref[...] + _rms(y, nw_ref[...]).reshape(nb, tc, d)


def _even_chunk(x2, batch, seq, prep, w_out, gla_norm, gdn_norm, nw):
    m, d = x2.shape
    tc = min(TILE_CHUNK, seq)
    ev, al = [a.reshape((batch, a.shape[0] // batch) + a.shape[1:]) for a in prep]

    def tok(n):
        return pl.BlockSpec((batch, tc, n), lambda j: (0, j, 0))

    out = pl.pallas_call(
        _even_chunk_kernel,
        grid=(seq // tc,),
        in_specs=[
            tok(d), tok(EV_WIDTH), pl.BlockSpec((batch, tc // GLA_CHUNK, 1, GLA_QK), lambda j: (0, j, 0, 0)),
            _const_spec((EVEN_OUT, d)), _const_spec((1, GLA_DV)), _const_spec((1, GDN_DV)), _const_spec((1, d)),
        ],
        out_specs=tok(d),
        out_shape=jax.ShapeDtypeStruct((batch, seq, d), F32),
        scratch_shapes=[
            pltpu.VMEM((batch, GLA_HEADS // 2, GLA_DV, LANES), F32),
            pltpu.VMEM((batch, GDN_HEADS, GDN_DK, GDN_DV), F32),
            pltpu.VMEM((batch * tc, EVEN_OUT), F32),
        ],
        compiler_params=pltpu.CompilerParams(dimension_semantics=("arbitrary",),
                                             vmem_limit_bytes=VMEM_LIMIT),
        name="even_chunk",
    )(x2.reshape(batch, seq, d), ev, al, w_out.astype(BF16), _row(gla_norm), _row(gdn_norm), _row(nw))
    return out.reshape(m, d)


def _odd_prep_part(r0, rows, proj, carries, p_, v_, wc_ref):
    ps, pq = proj
    carry, carry2 = carries
    rs = slice(r0, r0 + rows)
    prev = _shift_rows(ps, carry, 1)
    new_carry = ps[rows - SUBLANES:rows, :]
    ps = ps + (prev - ps) * p_["mu"][...]
    r = ps[:, 0:RWKV_W]
    k = ps[:, RWKV_W:2 * RWKV_W]
    v = ps[:, 2 * RWKV_W:3 * RWKV_W]
    lora = ps[:, 3 * RWKV_W:3 * RWKV_W + LANES]
    gl = ps[:, 3 * RWKV_W + LANES:]
    logw = -(RWKV_DECAY_SCALE * _sigmoid(p_["w0"][...] + _mm(jnp.tanh(lora), p_["w2"][...])))
    a = _sigmoid(p_["a0"][...] + _mm(lora, p_["a2"][...]))
    v_["g"][rs, :] = _mm(_sigmoid(gl), p_["g2"][...])
    ones_bd = p_["ones"][...]
    kk = k * p_["kk"][...]
    kk = kk * lax.rsqrt(_group_sum(kk * kk, ones_bd) + NORM_EPS)
    k = k * (1.0 + (a - 1.0) * p_["ka"][...])
    v_["bonus"][rs, :] = _group_sum(r * k * p_["rk"][...], ones_bd) * v
    gcum = _seg_cumsum(logw, RWKV_CHUNK)
    for c in range(rows // RWKV_CHUNK):
        wc_ref[r0 // RWKV_CHUNK + c] = jnp.exp(gcum[(c + 1) * RWKV_CHUNK - 1:(c + 1) * RWKV_CHUNK, :])
    e_neg = jnp.exp(-gcum)
    v_["rt"][rs, :] = r * jnp.exp(gcum)
    v_["kt"][rs, :] = k * e_neg
    v_["bt"][rs, :] = kk * a * e_neg
    v_["kh"][rs, :] = kk * jnp.exp(gcum - logw)
    v_["v"][rs, :] = v

    lx = pq[:, 0:LRU_WIDTH]
    xb = lx * p_["cw"][CONV_W - 1:CONV_W, :] + p_["cb"][...]
    for s in range(1, CONV_W):
        xb = xb + _shift_rows(lx, carry2, s) * p_["cw"][CONV_W - 1 - s:CONV_W - s, :]
    gate_r = _sigmoid(_mm(xb, p_["wa"][...]) + p_["ba"][...])
    gate_i = _sigmoid(_mm(xb, p_["wx"][...]) + p_["bx"][...])
    log_a = -LRU_C * gate_r * _softplus(-p_["lam"][...])
    mult = jnp.sqrt(jnp.maximum(-jnp.tanh(log_a) * (jnp.exp(2.0 * log_a) + 1.0), 0.0))
    v_["la"][rs, :] = jnp.exp(log_a)
    v_["lb"][rs, :] = mult * gate_i * xb
    v_["gly"][rs, :] = _gelu_tanh(pq[:, LRU_WIDTH:])
    return new_carry, lx[rows - SUBLANES:rows, :]


def _odd_prep_kernel(tiles_per_seq,
                     x_ref, nw_ref, wp_ref, wl_ref, mu_ref, w0_ref, w2_ref, a0_ref, a2_ref, g2_ref,
                     kk_ref, ka_ref, rk_ref, ones_ref, cw_ref, cb_ref, wa_ref, ba_ref, wx_ref, bx_ref, lam_ref,
                     od_ref, wc_ref, carry_p, carry_l):
    v_ = _col_views(od_ref, OD_COLS)
    p_ = dict(mu=mu_ref, w0=w0_ref, w2=w2_ref, a0=a0_ref, a2=a2_ref, g2=g2_ref, kk=kk_ref, ka=ka_ref, rk=rk_ref,
              ones=ones_ref, cw=cw_ref, cb=cb_ref, wa=wa_ref, ba=ba_ref, wx=wx_ref, bx=bx_ref, lam=lam_ref)
    first = (pl.program_id(0) % tiles_per_seq) == 0
    tm = x_ref.shape[0]
    rows = tm // ODD_PREP_PARTS
    dot = functools.partial(jnp.dot, preferred_element_type=F32)
    carries = (jnp.where(first, 0.0, carry_p[...]), jnp.where(first, 0.0, carry_l[...]))
    proj = {}
    for t in range(ODD_PREP_PARTS + 1):
        if t < ODD_PREP_PARTS:
            h = _rms(x_ref[t * rows:(t + 1) * rows, :], nw_ref[...]).astype(BF16)
            proj[t] = (dot(h, wp_ref[...]), dot(h, wl_ref[...]))
        if t >= 1:
            carries = _odd_prep_part((t - 1) * rows, rows, proj.pop(t - 1), carries, p_, v_, wc_ref)
    carry_p[...] = carries[0]
    carry_l[...] = carries[1]


def _block_diag(w):
    nb, n, _ = w.shape
    eye = jnp.eye(nb, dtype=w.dtype)
    return (eye[:, None, :, None] * w[:, :, None, :]).reshape(nb * n, nb * n)


def _odd_prep(x2, seq, nw, w_in, mu, w0, w2, a0, a2, g2, k_k, k_a, r_k, ones_bd,
              conv_w, conv_b, wa, ba, wx, bx, lam):
    m, d = x2.shape
    tm = min(TILE_PREP, seq)
    w_p = w_in[:, 0:RWKV_SHIFT]
    w_l = w_in[:, RWKV_SHIFT:]
    w2p = jnp.concatenate([w2, jnp.zeros((RWKV_A_LORA, RWKV_W), F32)], axis=0)
    a2p = jnp.concatenate([jnp.zeros((RWKV_W_LORA, RWKV_W), F32), a2], axis=0)

    def tok(n):
        return pl.BlockSpec((tm, n), lambda i: (i, 0))

    wspec = pl.BlockSpec((tm // RWKV_CHUNK, 1, RWKV_W), lambda i: (i, 0, 0))
    out_specs = [tok(OD_WIDTH), wspec]
    out_shape = [jax.ShapeDtypeStruct((m, OD_WIDTH), F32), jax.ShapeDtypeStruct((m // RWKV_CHUNK, 1, RWKV_W), F32)]
    vec = _const_spec((1, RWKV_W))
    sq = _const_spec((RWKV_W, RWKV_W))
    return pl.pallas_call(
        functools.partial(_odd_prep_kernel, seq // tm),
        grid=(m // tm,),
        in_specs=[
            tok(d), _const_spec((1, d)), _const_spec(w_p.shape), _const_spec(w_l.shape),
            _const_spec((1, RWKV_SHIFT)), vec, _const_spec(w2p.shape), vec, _const_spec(a2p.shape),
            _const_spec(g2.shape), vec, vec, vec, sq,
            _const_spec(conv_w.shape), vec, sq, vec, sq, vec, vec,
        ],
        out_specs=out_specs,
        out_shape=out_shape,
        scratch_shapes=[pltpu.VMEM((SUBLANES, RWKV_SHIFT), F32), pltpu.VMEM((SUBLANES, LRU_WIDTH), F32)],
        compiler_params=pltpu.CompilerParams(dimension_semantics=("arbitrary",),
                                             vmem_limit_bytes=VMEM_LIMIT),
        name="odd_prep",
    )(x2, _row(nw), w_p.astype(BF16), w_l.astype(BF16), _row(mu), _row(w0), w2p.astype(BF16), _row(a0),
      a2p.astype(BF16), g2.astype(BF16), _row(k_k), _row(k_a), _row(r_k), ones_bd,
      conv_w.astype(F32), _row(conv_b), _block_diag(wa).astype(BF16), _row(ba),
      _block_diag(wx).astype(BF16), _row(bx), _row(lam))


def _odd_chunk_kernel(x_ref, od_ref, wc_ref, wo_ref, lnw_ref, lnb_ref, ones_ref, nw_ref,
                      o_ref, s_rwkv, h_lru, y_scr):
    v_ = _col_views(od_ref, OD_COLS)
    rt_ref, kt_ref, bt_ref, kh_ref, v_ref = v_["rt"], v_["kt"], v_["bt"], v_["kh"], v_["v"]
    g_ref, bonus_ref, la_ref, lb_ref, gly_ref = v_["g"], v_["bonus"], v_["la"], v_["lb"], v_["gly"]
    nb, tc, d = x_ref.shape

    @pl.when(pl.program_id(0) == 0)
    def _():
        s_rwkv[...] = jnp.zeros(s_rwkv.shape, F32)
        h_lru[...] = jnp.zeros(h_lru.shape, F32)

    cs = RWKV_CHUNK
    r_i = lax.broadcasted_iota(jnp.int32, (cs, 2 * cs), 0)
    c_i = lax.broadcasted_iota(jnp.int32, (cs, 2 * cs), 1) & (cs - 1)
    f_incl = c_i <= r_i
    f_strict = c_i < r_i
    masks = _head_masks(RWKV_N)
    stack = lambda z: _unfold(z, masks)

    def rwkv_chunk(c):
        r0 = c * cs
        rows = pl.ds(r0, cs)
        probs = [(b, p) for b in range(nb) for p in range(RWKV_HEADS // 2)]
        pls = [slice(p * LANES, (p + 1) * LANES) for _, p in probs]
        np_ = len(probs)
        ld = lambda ref: [ref[b, rows, pls[i]] for i, (b, _) in enumerate(probs)]
        rt, kt, bt, kh, vv = ld(rt_ref), ld(kt_ref), ld(bt_ref), ld(kh_ref), ld(v_ref)
        vs = [stack(z) for z in vv]
        ks = [stack(z) for z in kt]
        bs = [stack(z) for z in bt]
        mm = [_mm_nt(jnp.concatenate([kh[i], rt[i]], axis=0), jnp.concatenate([bs[i], ks[i]], axis=0))
              for i in range(np_)]
        m_rb = [jnp.where(f_incl, mm[i][cs:, 0:LANES], 0.0) for i in range(np_)]
        m_bk = [jnp.where(f_strict, mm[i][0:cs, LANES:], 0.0) for i in range(np_)]
        m_rk = [jnp.where(f_incl, mm[i][cs:, LANES:], 0.0) for i in range(np_)]
        t = _inv_unit_lower_folded([jnp.where(f_strict, mm[i][0:cs, 0:LANES], 0.0) for i in range(np_)], _parts1)
        x = [_mm(m_bk[i], vs[i]) for i in range(np_)]
        sol = [_mm(t[i], jnp.concatenate([stack(kh[i]), stack(x[i])], axis=1)) for i in range(np_)]
        sol_s = [jnp.concatenate([stack(z[:, 0:LANES]), stack(z[:, LANES:])], axis=1) for z in sol]
        corr = [_mm(m_rb[i], sol_s[i]) for i in range(np_)]
        y_loc = [_mm(m_rk[i], vs[i]) - corr[i][:, LANES:] for i in range(np_)]
        r_eff = [rt[i] - corr[i][:, 0:LANES] for i in range(np_)]
        vtk = [_mm(_fold(vs[i].T), ks[i]) for i in range(np_)]
        uv_t = [_fold(sol_s[i][:, LANES:].T) for i in range(np_)]
        st = [s_rwkv[b, p] for b, p in probs]
        ut = [_mm_nt(st[i], sol_s[i][:, 0:LANES]) + uv_t[i] for i in range(np_)]
        ys = [_mm_nt(r_eff[i], stack(st[i])) + y_loc[i] for i in range(np_)]
        upd = [_mm(ut[i], bs[i]) for i in range(np_)]
        for i, (b, p) in enumerate(probs):
            y_scr[pl.ds(b * tc + r0, cs), pls[i]] = ys[i]
            s_rwkv[b, p] = (st[i] + vtk[i] - upd[i]) * wc_ref[b, c][:, pls[i]]

    def lru_scan():
        rows_i = lax.broadcasted_iota(jnp.int32, (tc, LRU_WIDTH), 0) & (SUBLANES - 1)
        a_s, b_s = [], []
        for b in range(nb):
            a = la_ref[b]
            bb = lb_ref[b]
            dd = 1
            while dd < SUBLANES:
                keep = rows_i >= dd
                bb = jnp.where(keep, a * pltpu.roll(bb, dd, 0) + bb, bb)
                a = jnp.where(keep, a * pltpu.roll(a, dd, 0), a)
                dd *= 2
            a_s.append(a)
            b_s.append(bb)
        hrow = [h_lru[b, SUBLANES - 1:SUBLANES, :] for b in range(nb)]
        hgs = [[] for _ in range(nb)]
        for g in range(tc // SUBLANES):
            gs = slice(g * SUBLANES, (g + 1) * SUBLANES)
            for b in range(nb):
                hg = b_s[b][gs] + a_s[b][gs] * hrow[b]
                hrow[b] = hg[SUBLANES - 1:SUBLANES, :]
                hgs[b].append(hg)
        out = []
        for b in range(nb):
            h_lru[b] = hgs[b][-1]
            out.extend(hgs[b])
        return out

    hls = None
    for c in range(tc // RWKV_CHUNK):
        rwkv_chunk(c)
        if c == 0:
            hls = lru_scan()
    y_lru = jnp.concatenate(hls, axis=0) * gly_ref[...].reshape(nb * tc, LRU_WIDTH)

    ones_bd = ones_ref[...]
    y = y_scr[...]
    mu = _group_sum(y, ones_bd) * (1.0 / RWKV_N)
    yc = y - mu
    var = _group_sum(yc * yc, ones_bd) * (1.0 / RWKV_N)
    yn = yc * lax.rsqrt(var + RWKV_GN_EPS) * lnw_ref[...] + lnb_ref[...]
    y_rwkv = (yn + bonus_ref[...].reshape(nb * tc, RWKV_W)) * g_ref[...].reshape(nb * tc, RWKV_W)
    out = (jnp.dot(y_rwkv.astype(BF16), wo_ref[0:RWKV_W, :], preferred_element_type=F32)
           + jnp.dot(y_lru.astype(BF16), wo_ref[RWKV_W:, :], preferred_element_type=F32))
    o_ref[...] = x_ref[...] + _rms(out, nw_ref[...]).reshape(nb, tc, d)


def _odd_chunk(x2, batch, seq, prep, w_out, ln_w, ln_b, ones_bd, nw):
    m, d = x2.shape
    tc = min(TILE_CHUNK, seq)
    od, wc = [a.reshape((batch, a.shape[0] // batch) + a.shape[1:]) for a in prep]

    def tok(n):
        return pl.BlockSpec((batch, tc, n), lambda j: (0, j, 0))

    vec = _const_spec((1, RWKV_W))
    out = pl.pallas_call(
        _odd_chunk_kernel,
        grid=(seq // tc,),
        in_specs=[
            tok(d), tok(OD_WIDTH), pl.BlockSpec((batch, tc // RWKV_CHUNK, 1, RWKV_W), lambda j: (0, j, 0, 0)),
            _const_spec((ODD_OUT, d)), vec, vec, _const_spec((RWKV_W, RWKV_W)), _const_spec((1, d)),
        ],
        out_specs=tok(d),
        out_shape=jax.ShapeDtypeStruct((batch, seq, d), F32),
        scratch_shapes=[
            pltpu.VMEM((batch, RWKV_HEADS // 2, RWKV_N, LANES), F32),
            pltpu.VMEM((batch, SUBLANES, LRU_WIDTH), F32),
            pltpu.VMEM((batch * tc, RWKV_W), F32),
        ],
        compiler_params=pltpu.CompilerParams(dimension_semantics=("arbitrary",),
                                             vmem_limit_bytes=VMEM_LIMIT),
        name="odd_chunk",
    )(x2.reshape(batch, seq, d), od, wc, w_out.astype(BF16), _row(ln_w), _row(ln_b), ones_bd, _row(nw))
    return out.reshape(m, d)


def _even_layer(x2, batch, seq, nw, w_in, w_out, lora_w2, lora_b, gla_norm, conv_w, a_log, dt_bias, gdn_norm):
    prep = _even_prep(x2, seq, nw[2], w_in, lora_w2, lora_b, conv_w, a_log, dt_bias)
    return _even_chunk(x2, batch, seq, prep, w_out, gla_norm, gdn_norm, nw[3])


def _odd_layer(x2, batch, seq, nw, w_in, w_out, mu, w0, w2, a0, a2, g2, k_k, k_a, r_k, ln_w, ln_b,
               conv_w, conv_b, wa, ba, wx, bx, lam):
    grp = jnp.arange(RWKV_W) // RWKV_N
    ones_bd = (grp[:, None] == grp[None, :]).astype(BF16)
    prep = _odd_prep(x2, seq, nw[2], w_in, mu, w0, w2, a0, a2, g2, k_k, k_a, r_k, ones_bd,
                     conv_w, conv_b, wa, ba, wx, bx, lam)
    return _odd_chunk(x2, batch, seq, prep, w_out, ln_w, ln_b, ones_bd, nw[3])


def kernel(x, norm_w, ffn_w_gate, ffn_w_up, ffn_w_down, even_w_in, even_w_out, gla_lora_w2, gla_lora_b, gla_norm, gdn_conv, gdn_a_log, gdn_dt_bias, gdn_norm, odd_w_in, odd_w_out, rwkv_mu, rwkv_w0, rwkv_w2, rwkv_a0, rwkv_a2, rwkv_g2, rwkv_k_k, rwkv_k_a, rwkv_r_k, rwkv_ln_w, rwkv_ln_b, lru_conv_w, lru_conv_b, lru_wa, lru_ba, lru_wx, lru_bx, lru_lambda):
    batch, seq, d = x.shape
    depth = norm_w.shape[0]
    x2 = x.reshape(batch * seq, d)
    wg, wu, wd = ffn_w_gate.astype(BF16), ffn_w_up.astype(BF16), ffn_w_down.astype(BF16)
    for i in range(depth):
        j = i // 2
        nw = norm_w[i]
        x2 = _ffn(x2, nw[0], nw[1], wg, wu, wd, i, 0)
        if i % 2 == 0:
            x2 = _even_layer(x2, batch, seq, nw, even_w_in[j], even_w_out[j], gla_lora_w2[j], gla_lora_b[j],
                             gla_norm[j], gdn_conv[j], gdn_a_log[j], gdn_dt_bias[j], gdn_norm[j])
        else:
            x2 = _odd_layer(x2, batch, seq, nw, odd_w_in[j], odd_w_out[j], rwkv_mu[j], rwkv_w0[j], rwkv_w2[j],
                            rwkv_a0[j], rwkv_a2[j], rwkv_g2[j], rwkv_k_k[j], rwkv_k_a[j],
                            rwkv_r_k[j].reshape(-1), rwkv_ln_w[j], rwkv_ln_b[j], lru_conv_w[j], lru_conv_b[j],
                            lru_wa[j], lru_ba[j], lru_wx[j], lru_bx[j], lru_lambda[j])
        x2 = _ffn(x2, nw[4], nw[5], wg, wu, wd, i, 1)
    return x2.reshape(batch, seq, d)
```
